```python
import math
import jax
import jax.numpy as jnp
from jax import lax
import numpy as np

D_MODEL = 2048
BATCH = 16
SEQ = 2048
DEPTH = 2

GRID_W = 64
D_HYENA = 1024
SHORT_CONV = 3
FILTER_EMB = 33
FILTER_HIDDEN = 64
DECAY_TARGET = 1e-2
FAST_DECAY_PCT = 0.3
SLOW_DECAY_PCT = 1.5
HEAD_DIM = 64
N_Q_HEADS = 16
N_KV_HEADS = 4
D_ATTN = N_Q_HEADS * HEAD_DIM
D_KV = N_KV_HEADS * HEAD_DIM
Q_BLOCK = 128
ROPE_THETA = 10000.0
N_BRANCH = 2
D_IN = 3 * D_HYENA + D_ATTN + 2 * D_KV + N_BRANCH * D_MODEL
D_FF = 5632
N_EXPERTS = 8
TOP_K = 2
D_FF_EXPERT = 7168
MOE_BLOCK = 1024
N_DENSE = (DEPTH + 1) // 2
N_MOE = DEPTH // 2
DEEPNORM_ALPHA = (2 * DEPTH) ** 0.25
DEEPNORM_BETA = (8 * DEPTH) ** -0.25
LN_EPS = 1e-5
RMS_EPS = 1e-6

F32 = jnp.float32

kernel_name = 'hybrid_hyena_gqa_moe_encoder'


def layer_norm(x, g, b):
    xf = x.astype(F32)
    mu = jnp.mean(xf, axis=-1, keepdims=True)
    xc = xf - mu
    var = jnp.mean(xc * xc, axis=-1, keepdims=True)
    return (xc * lax.rsqrt(var + LN_EPS) * g.astype(F32) + b.astype(F32)).astype(x.dtype)


def rms_norm(x, g):
    xf = x.astype(F32)
    return (xf * lax.rsqrt(jnp.mean(xf * xf, axis=-1, keepdims=True) + RMS_EPS) * g.astype(F32)).astype(x.dtype)


def centred_short_conv(u, w, b):
    L = u.shape[1]
    p = SHORT_CONV // 2
    up = jnp.pad(u, ((0, 0), (p, SHORT_CONV - 1 - p), (0, 0)))
    return sum(up[:, j:j + L] * w[j] for j in range(SHORT_CONV)) + b


def hyena_filters(L, fw1, fb1, fw2, fb2, fw3, freq):
    t = jnp.linspace(0.0, 1.0, L, dtype=F32)[:, None]
    bands = (FILTER_EMB - 1) // 2
    w = 2.0 * math.pi * jnp.arange(L, dtype=F32)[:, None] / L
    f = jnp.linspace(1e-4, bands - 1, bands, dtype=F32)[None, :]
    z = jnp.concatenate([t, jnp.cos(f * w), -jnp.sin(f * w)], axis=-1)
    fr = freq.astype(F32)
    a = jnp.sin(fr[0] * (z @ fw1.astype(F32) + fb1.astype(F32)))
    a = jnp.sin(fr[1] * (a @ fw2.astype(F32) + fb2.astype(F32)))
    hk = a @ fw3.astype(F32)
    min_decay = math.log(DECAY_TARGET) / SLOW_DECAY_PCT
    max_decay = math.log(DECAY_TARGET) / FAST_DECAY_PCT
    deltas = jnp.abs(jnp.linspace(min_decay, max_decay, D_HYENA, dtype=F32))
    window = jnp.exp(-t * deltas[None, :])
    return hk[:, :D_HYENA] * window, hk[:, D_HYENA:] * window


def bidir_long_conv(v, h_fwd, h_bwd, bias):
    B, L, C = v.shape
    k = jnp.concatenate([h_fwd, jnp.zeros((1, C), F32), h_bwd[:0:-1]], axis=0)
    k_f = jnp.fft.rfft(k, n=2 * L, axis=0)
    vf = v.astype(F32)
    v_f = jnp.fft.rfft(vf, n=2 * L, axis=1)
    y = jnp.fft.irfft(v_f * k_f[None], n=2 * L, axis=1)[:, :L]
    return (y + vf * bias.astype(F32)).astype(v.dtype)


def axial_rope(L):
    rows = L // GRID_W
    row = jnp.repeat(jnp.arange(rows, dtype=F32), GRID_W)
    col = jnp.tile(jnp.arange(GRID_W, dtype=F32), rows)
    n_pairs = HEAD_DIM // 4
    inv = ROPE_THETA ** (-jnp.arange(n_pairs, dtype=F32) / n_pairs)
    ang = jnp.concatenate([row[:, None] * inv, col[:, None] * inv], axis=-1)
    return jnp.cos(ang), jnp.sin(ang)


def apply_rope(x, cos, sin):
    xf = x.astype(F32)
    half = HEAD_DIM // 2
    x1, x2 = xf[..., :half], xf[..., half:]
    c = cos[None, :, None, :]
    s = sin[None, :, None, :]
    return jnp.concatenate([x1 * c - x2 * s, x2 * c + x1 * s], axis=-1).astype(x.dtype)


def block_gqa(q, k, v):
    B, L = q.shape[:2]
    G = N_Q_HEADS // N_KV_HEADS
    nblk = L // Q_BLOCK
    qb = q.reshape(B, nblk, Q_BLOCK, N_KV_HEADS, G, HEAD_DIM).transpose(1, 0, 2, 3, 4, 5)
    scale = HEAD_DIM ** -0.5

    def attend(q_blk):
        s = jnp.einsum('bqkgd,bskd->bkgqs', q_blk, k).astype(F32) * scale
        p = jax.nn.softmax(s, axis=-1).astype(v.dtype)
        return jnp.einsum('bkgqs,bskd->bqkgd', p, v)

    o = lax.map(attend, qb)
    return o.transpose(1, 0, 2, 3, 4, 5).reshape(B, L, D_ATTN)


def hybrid_mixer(h, w_in, conv_w, conv_b, fw1, fb1, fw2, fb2, fw3, freq, hy_bias,
                 q_g, k_g, w_hy_br, w_att_br, w_o):
    B, L, _ = h.shape
    proj = h @ w_in
    o0 = 3 * D_HYENA
    o1 = o0 + D_ATTN
    o2 = o1 + D_KV
    o3 = o2 + D_KV
    u = centred_short_conv(proj[..., :o0], conv_w, conv_b)
    x0, x1, hv = jnp.split(u, 3, axis=-1)
    h_fwd, h_bwd = hyena_filters(L, fw1, fb1, fw2, fb2, fw3, freq)
    y_hy = x0 * bidir_long_conv(x1 * hv, h_fwd, h_bwd, hy_bias)
    cos, sin = axial_rope(L)
    q = apply_rope(rms_norm(proj[..., o0:o1].reshape(B, L, N_Q_HEADS, HEAD_DIM), q_g), cos, sin)
    k = apply_rope(rms_norm(proj[..., o1:o2].reshape(B, L, N_KV_HEADS, HEAD_DIM), k_g), cos, sin)
    v = proj[..., o2:o3].reshape(B, L, N_KV_HEADS, HEAD_DIM)
    y_att = block_gqa(q, k, v)
    gates = jax.nn.sigmoid(proj[..., o3:].astype(F32)).astype(h.dtype).reshape(B, L, N_BRANCH, D_MODEL)
    merged = gates[:, :, 0] * (y_hy @ w_hy_br) + gates[:, :, 1] * (y_att @ w_att_br)
    return merged @ w_o


def dense_swiglu(h, w_gate, w_up, w_down):
    return (jax.nn.silu(h @ w_gate) * (h @ w_up)) @ w_down


def moe_swiglu(h, router_w, router_b, w_gate, w_up, w_down):
    B, L, D = h.shape
    T = B * L
    xt = h.reshape(T, D)
    logits = (xt @ router_w).astype(F32) + router_b.astype(F32)
    top_logit, top_e = lax.top_k(logits, TOP_K)
    top_w = jax.nn.softmax(top_logit, axis=-1)
    A = T * TOP_K
    flat_e = top_e.reshape(A).astype(jnp.int32)
    flat_tok = jnp.arange(A, dtype=jnp.int32) // TOP_K
    flat_w = top_w.reshape(A)
    order = jnp.argsort(flat_e)
    se = flat_e[order]
    counts = jnp.zeros((N_EXPERTS,), jnp.int32).at[flat_e].add(1)
    padded = (counts + MOE_BLOCK - 1) // MOE_BLOCK * MOE_BLOCK
    pad_end = jnp.cumsum(padded)
    pad_start = pad_end - padded
    start = jnp.cumsum(counts) - counts
    dest = pad_start[se] + jnp.arange(A, dtype=jnp.int32) - start[se]
    nblk = -(-(A + N_EXPERTS * (MOE_BLOCK - 1)) // MOE_BLOCK)
    P = nblk * MOE_BLOCK
    src_tok = jnp.zeros((P,), jnp.int32).at[dest].set(flat_tok[order])
    src_w = jnp.zeros((P,), F32).at[dest].set(flat_w[order])
    blk_e = jnp.minimum(jnp.searchsorted(pad_end, jnp.arange(nblk, dtype=jnp.int32) * MOE_BLOCK, side='right'),
                        N_EXPERTS - 1)
    xb = xt[src_tok].reshape(nblk, MOE_BLOCK, D)

    def expert_block(args):
        x_blk, e = args
        return (jax.nn.silu(x_blk @ w_gate[e]) * (x_blk @ w_up[e])) @ w_down[e]

    yb = lax.map(expert_block, (xb, blk_e)).reshape(P, D)
    out = jnp.zeros((T, D), h.dtype).at[src_tok].add(yb * src_w[:, None].astype(h.dtype))
    return out.reshape(B, L, D)


def setup_inputs(seed: int = 0) -> dict:
    key = jax.random.key(seed)
    ks = iter(jax.random.split(key, 40))

    def nrm(shape, scale):
        return jax.random.normal(next(ks), shape, F32) * scale

    D = D_MODEL
    return {
        'x': nrm((BATCH, SEQ, D), 1.0),
        'ln_in_g': 1.0 + nrm((D,), 0.02),
        'ln_in_b': nrm((D,), 0.02),
        'w_in': nrm((DEPTH, D, D_IN), D ** -0.5),
        'hy_conv_w': nrm((DEPTH, SHORT_CONV, 3 * D_HYENA), SHORT_CONV ** -0.5),
        'hy_conv_b': nrm((DEPTH, 3 * D_HYENA), 0.02),
        'hy_fw1': nrm((DEPTH, FILTER_EMB, FILTER_HIDDEN), FILTER_EMB ** -0.5),
        'hy_fb1': nrm((DEPTH, FILTER_HIDDEN), 0.1),
        'hy_fw2': nrm((DEPTH, FILTER_HIDDEN, FILTER_HIDDEN), FILTER_HIDDEN ** -0.5),
        'hy_fb2': nrm((DEPTH, FILTER_HIDDEN), 0.1),
        'hy_fw3': nrm((DEPTH, FILTER_HIDDEN, 2 * D_HYENA), 0.05 * FILTER_HIDDEN ** -0.5),
        'hy_freq': 1.0 + nrm((DEPTH, 2, FILTER_HIDDEN), 0.1),
        'hy_bias': nrm((DEPTH, D_HYENA), 1.0),
        'q_norm_g': 1.0 + nrm((DEPTH, HEAD_DIM), 0.02),
        'k_norm_g': 1.0 + nrm((DEPTH, HEAD_DIM), 0.02),
        'w_hy_br': nrm((DEPTH, D_HYENA, D), D_HYENA ** -0.5),
        'w_att_br': nrm((DEPTH, D_ATTN, D), D_ATTN ** -0.5),
        'w_o': nrm((DEPTH, D, D), D ** -0.5 * DEEPNORM_BETA),
        'ln_mix_g': 1.0 + nrm((DEPTH, D), 0.02),
        'ln_mix_b': nrm((DEPTH, D), 0.02),
        'ffn_w_gate': nrm((N_DENSE, D, D_FF), D ** -0.5),
        'ffn_w_up': nrm((N_DENSE, D, D_FF), D ** -0.5),
        'ffn_w_down': nrm((N_DENSE, D_FF, D), D_FF ** -0.5 * DEEPNORM_BETA),
        'router_w': nrm((N_MOE, D, N_EXPERTS), D ** -0.5),
        'router_b': nrm((N_MOE, N_EXPERTS), 0.01),
        'exp_w_gate': nrm((N_MOE, N_EXPERTS, D, D_FF_EXPERT), D ** -0.5),
        'exp_w_up': nrm((N_MOE, N_EXPERTS, D, D_FF_EXPERT), D ** -0.5),
        'exp_w_down': nrm((N_MOE, N_EXPERTS, D_FF_EXPERT, D), D_FF_EXPERT ** -0.5 * DEEPNORM_BETA),
        'ln_ffn_g': 1.0 + nrm((DEPTH, D), 0.02),
        'ln_ffn_b': nrm((DEPTH, D), 0.02),
    }


def reference(x, ln_in_g, ln_in_b, w_in, hy_conv_w, hy_conv_b, hy_fw1, hy_fb1, hy_fw2, hy_fb2, hy_fw3,
              hy_freq, hy_bias, q_norm_g, k_norm_g, w_hy_br, w_att_br, w_o, ln_mix_g, ln_mix_b,
              ffn_w_gate, ffn_w_up, ffn_w_down, router_w, router_b, exp_w_gate, exp_w_up, exp_w_down,
              ln_ffn_g, ln_ffn_b):
    h = layer_norm(x, ln_in_g, ln_in_b)
    for i in range(DEPTH):
        mix = hybrid_mixer(h, w_in[i], hy_conv_w[i], hy_conv_b[i], hy_fw1[i], hy_fb1[i], hy_fw2[i],
                           hy_fb2[i], hy_fw3[i], hy_freq[i], hy_bias[i], q_norm_g[i], k_norm_g[i],
                           w_hy_br[i], w_att_br[i], w_o[i])
        h = layer_norm(DEEPNORM_ALPHA * h + mix, ln_mix_g[i], ln_mix_b[i])
        j = i // 2
        if i % 2 == 0:
            f = dense_swiglu(h, ffn_w_gate[j], ffn_w_up[j], ffn_w_down[j])
        else:
            f = moe_swiglu(h, router_w[j], router_b[j], exp_w_gate[j], exp_w_up[j], exp_w_down[j])
        h = layer_norm(DEEPNORM_ALPHA * h + f, ln_ffn_g[i], ln_ffn_b[i])
    return h
```

```python
import functools
import math

import jax
import jax.numpy as jnp
from jax import lax
from jax.experimental import pallas as pl
from jax.experimental.pallas import tpu as pltpu

F32 = jnp.float32
BF16 = jnp.bfloat16

GRID_W = 64
ROPE_THETA = 10000.0
TOP_K = 2
DECAY_TARGET = 1e-2
FAST_DECAY_PCT = 0.3
SLOW_DECAY_PCT = 1.5
LN_EPS = 1e-5
RMS_EPS = 1e-6

V7X_VMEM_BYTES = 64 * 1024 * 1024
V7X_LANES = 128
V7X_MXU_DIM = 256
VMEM_BUDGET = V7X_VMEM_BYTES - 8 * 1024 * 1024


def _params(semantics, vmem_bytes):
    return pltpu.CompilerParams(dimension_semantics=semantics,
                                vmem_limit_bytes=min(int(vmem_bytes), VMEM_BUDGET))


def _dot(a, b):
    return jnp.dot(a, b, preferred_element_type=F32)


def _split_bf16(a):
    hi = a.astype(BF16)
    lo = (a - hi.astype(F32)).astype(BF16)
    return hi, lo


def _dot3(a_hi, a_lo, b_hi, b_lo):
    return _dot(a_hi, b_hi) + (_dot(a_hi, b_lo) + _dot(a_lo, b_hi))


def _layer_norm_rows(x, g, b):
    mu = jnp.mean(x, axis=-1, keepdims=True)
    xc = x - mu
    var = jnp.mean(xc * xc, axis=-1, keepdims=True)
    return xc * lax.rsqrt(var + LN_EPS) * g + b


def _resid_ln_kernel(alpha, n_add, *refs):
    h_ref = refs[0]
    add_refs = refs[1:1 + n_add]
    g_ref, b_ref, of_ref, ob_ref = refs[1 + n_add:]
    x = h_ref[...]
    if n_add:
        acc = add_refs[0][...]
        for r in add_refs[1:]:
            acc = acc + r[...]
        x = alpha * x + acc
    y = _layer_norm_rows(x, g_ref[...], b_ref[...])
    of_ref[...] = y
    ob_ref[...] = y.astype(BF16)


def _resid_ln(h, addends, g, b, alpha, tm=256):
    T, D = h.shape
    n_add = len(addends)
    row = pl.BlockSpec((tm, D), lambda i: (i, 0))
    vec = pl.BlockSpec((1, D), lambda i: (0, 0))
    return pl.pallas_call(
        functools.partial(_resid_ln_kernel, alpha, n_add),
        grid=(T // tm,),
        in_specs=[row] * (1 + n_add) + [vec, vec],
        out_specs=[row, row],
        out_shape=[jax.ShapeDtypeStruct((T, D), F32), jax.ShapeDtypeStruct((T, D), BF16)],
        compiler_params=_params(("parallel",), (2 * (1 + n_add) * 4 + 12 + 16) * tm * D),
        name="resid_ln",
    )(h, *addends, g.reshape(1, D), b.reshape(1, D))


def _hy_proj_kernel(h_ref, w0_ref, w1_ref, w2_ref, cw0_ref, cw1_ref, cw2_ref,
                    cb0_ref, cb1_ref, cb2_ref, x0_ref, vv_ref):
    h = h_ref[...]
    L = h.shape[0]
    row = lax.broadcasted_iota(jnp.int32, (L, 1), 0)

    def stream(w_ref, cw_ref, cb_ref):
        p = _dot(h, w_ref[...])
        prev = jnp.where(row == 0, 0.0, pltpu.roll(p, 1, 0))
        nxt = jnp.where(row == L - 1, 0.0, pltpu.roll(p, L - 1, 0))
        cw = cw_ref[...]
        return prev * cw[0:1] + p * cw[1:2] + nxt * cw[2:3] + cb_ref[...]

    x0_ref[...] = stream(w0_ref, cw0_ref, cb0_ref)
    vv_ref[...] = stream(w1_ref, cw1_ref, cb1_ref) * stream(w2_ref, cw2_ref, cb2_ref)


def _hy_proj(hb3, w_in_bf, conv_w, conv_b3, layer, dh, tn=256):
    B, L, D = hb3.shape
    nj = dh // tn

    def wspec(s):
        return pl.BlockSpec((None, D, tn), lambda j, b: (layer, 0, s * nj + j))

    def cwspec(s):
        return pl.BlockSpec((None, conv_w.shape[1], tn), lambda j, b: (layer, 0, s * nj + j))

    def cbspec(s):
        return pl.BlockSpec((None, 1, tn), lambda j, b: (layer, 0, s * nj + j))

    out = pl.BlockSpec((None, L, tn), lambda j, b: (b, 0, j))
    return pl.pallas_call(
        _hy_proj_kernel,
        grid=(nj, B),
        in_specs=[pl.BlockSpec((None, L, D), lambda j, b: (b, 0, 0)),
                  wspec(0), wspec(1), wspec(2), cwspec(0), cwspec(1), cwspec(2),
                  cbspec(0), cbspec(1), cbspec(2)],
        out_specs=[out, out],
        out_shape=[jax.ShapeDtypeStruct((B, L, dh), F32)] * 2,
        compiler_params=_params(("parallel", "parallel"),
                                4 * L * D + 12 * D * tn + 16 * L * tn + 40 * L * tn),
        name="hyena_proj_conv",
    )(hb3, w_in_bf, w_in_bf, w_in_bf, conv_w, conv_w, conv_w, conv_b3, conv_b3, conv_b3)


def _qkv_kernel(n_q, n_kv, hd, x_ref, w_ref, bd_ref, gain_ref, scale_ref, cos_ref, sin_ref,
                q_ref, k_ref, v_ref):
    n_rot = (n_q + n_kv) * hd
    p = _dot(x_ref[...], w_ref[...])
    pr = p[:, :n_rot]
    sq = (pr * pr).astype(BF16)
    bd = bd_ref[...]
    gw = bd.shape[0]
    ms = jnp.concatenate([_dot(sq[:, c:c + gw], bd) for c in range(0, n_rot, gw)], axis=1)
    xn = pr * lax.rsqrt(ms + RMS_EPS) * gain_ref[...]
    reps = n_rot // V7X_LANES
    cos = jnp.concatenate([cos_ref[...]] * reps, axis=1)
    sin = jnp.concatenate([sin_ref[...]] * reps, axis=1)
    lane = lax.broadcasted_iota(jnp.int32, (1, n_rot), 1)
    first_half = (lane % hd) < (hd // 2)
    half = hd // 2
    swapped = jnp.where(first_half, pltpu.roll(xn, n_rot - half, 1), pltpu.roll(xn, half, 1))
    rot = ((xn * cos + swapped * sin) * scale_ref[...]).astype(BF16)
    for hh in range(n_q):
        q_ref[hh] = rot[:, hh * hd:(hh + 1) * hd]
    for hh in range(n_kv):
        k_ref[hh] = rot[:, (n_q + hh) * hd:(n_q + hh + 1) * hd]
        v_ref[hh] = p[:, n_rot + hh * hd:n_rot + (hh + 1) * hd].astype(BF16)


def _qkv_proj(hb, w_qkv, bd, gain, scale, cos_t, sin_t, B, L, n_q, n_kv, hd, tm=512):
    T, D = hb.shape
    nw = w_qkv.shape[1]
    n_rot = (n_q + n_kv) * hd
    tm = min(tm, L)
    tpb = L // tm
    const = lambda i: (0, 0)
    return pl.pallas_call(
        functools.partial(_qkv_kernel, n_q, n_kv, hd),
        grid=(T // tm,),
        in_specs=[pl.BlockSpec((tm, D), lambda i: (i, 0)),
                  pl.BlockSpec((D, nw), const),
                  pl.BlockSpec(bd.shape, const),
                  pl.BlockSpec((1, n_rot), const),
                  pl.BlockSpec((1, n_rot), const),
                  pl.BlockSpec((tm, V7X_LANES), lambda i: (i % tpb, 0)),
                  pl.BlockSpec((tm, V7X_LANES), lambda i: (i % tpb, 0))],
        out_specs=[pl.BlockSpec((None, n_q, tm, hd), lambda i: (i // tpb, 0, i % tpb, 0)),
                   pl.BlockSpec((None, n_kv, tm, hd), lambda i: (i // tpb, 0, i % tpb, 0)),
                   pl.BlockSpec((None, n_kv, tm, hd), lambda i: (i // tpb, 0, i % tpb, 0))],
        out_shape=[jax.ShapeDtypeStruct((B, n_q, L, hd), BF16),
                   jax.ShapeDtypeStruct((B, n_kv, L, hd), BF16),
                   jax.ShapeDtypeStruct((B, n_kv, L, hd), BF16)],
        compiler_params=_params(("parallel",),
                                4 * tm * D + 4 * D * nw + 8 * tm * (n_q + 2 * n_kv) * V7X_LANES
                                + 48 * tm * nw),
        name="qkv_proj_norm_rope",
    )(hb, w_qkv, bd, gain, scale, cos_t, sin_t)


def _attn_kernel(group, hd, q_ref, k_ref, v_ref, o_ref):
    k = k_ref[...]
    v = v_ref[...]
    for g in range(group):
        s = lax.dot_general(q_ref[g], k, (((1,), (1,)), ((), ())),
                            preferred_element_type=F32)
        m = jnp.max(s, axis=-1, keepdims=True)
        e = jnp.exp(s - m)
        denom = jnp.sum(e, axis=-1, keepdims=True)
        o = _dot(e.astype(BF16), v)
        o_ref[:, g * hd:(g + 1) * hd] = (o / denom).astype(o_ref.dtype)


def _attention(q, k, v, tq=512):
    B, n_q, L, hd = q.shape
    n_kv = k.shape[1]
    group = n_q // n_kv
    tq = min(tq, L)
    return pl.pallas_call(
        functools.partial(_attn_kernel, group, hd),
        grid=(B, n_kv, L // tq),
        in_specs=[pl.BlockSpec((None, group, tq, hd), lambda b, j, t: (b, j, t, 0)),
                  pl.BlockSpec((None, None, L, hd), lambda b, j, t: (b, j, 0, 0)),
                  pl.BlockSpec((None, None, L, hd), lambda b, j, t: (b, j, 0, 0))],
        out_specs=pl.BlockSpec((None, tq, group * hd), lambda b, j, t: (b, t, j)),
        out_shape=jax.ShapeDtypeStruct((B, L, n_q * hd), BF16),
        compiler_params=_params(("parallel", "parallel", "parallel"),
                                8 * L * V7X_LANES + 8 * group * tq * V7X_LANES + 40 * tq * L),
        name="gqa_attention",
    )(q, k, v)


def _filter_kernel(z_ref, fw1_ref, fb1_ref, fw2_ref, fb2_ref, fr_ref, w3f_ref, w3b_ref, dl_ref,
                   hs_ref, hd_ref):
    z = z_ref[...]
    L = z.shape[0]
    fr = fr_ref[...]

    def hp_dot(a, b):
        a_hi, a_lo = _split_bf16(a)
        b_hi, b_lo = _split_bf16(b)
        return _dot3(a_hi, a_lo, b_hi, b_lo)

    a = jnp.sin(fr[0:1] * (hp_dot(z, fw1_ref[...]) + fb1_ref[...]))
    a = jnp.sin(fr[1:2] * (hp_dot(a, fw2_ref[...]) + fb2_ref[...]))
    window = jnp.exp(-z[:, 0:1] * dl_ref[...])
    h_fwd = hp_dot(a, w3f_ref[...]) * window
    h_bwd = hp_dot(a, w3b_ref[...]) * window
    row = lax.broadcasted_iota(jnp.int32, (L, 1), 0)
    h_bwd0 = jnp.where(row == 0, 0.0, h_bwd)
    hs_ref[...] = h_fwd + h_bwd0
    hd_ref[...] = h_bwd0 - h_fwd


def _filters(z, fw1p, fb1, fw2, fb2, fr, fw3, deltas, dh, tn=256):
    L, zw = z.shape
    nh = fw2.shape[0]
    nj = dh // tn
    const = lambda j: (0, 0)
    out = pl.BlockSpec((L, tn), lambda j: (0, j))
    return pl.pallas_call(
        _filter_kernel,
        grid=(nj,),
        in_specs=[pl.BlockSpec((L, zw), const), pl.BlockSpec((zw, nh), const),
                  pl.BlockSpec((1, nh), const), pl.BlockSpec((nh, nh), const),
                  pl.BlockSpec((1, nh), const), pl.BlockSpec((2, nh), const),
                  pl.BlockSpec((nh, tn), lambda j: (0, j)),
                  pl.BlockSpec((nh, tn), lambda j: (0, nj + j)),
                  pl.BlockSpec((1, tn), lambda j: (0, j))],
        out_specs=[out, out],
        out_shape=[jax.ShapeDtypeStruct((L, dh), F32)] * 2,
        compiler_params=_params(("parallel",), 32 * 1024 * 1024),
        name="hyena_filters",
    )(z, fw1p, fb1, fw2, fb2, fr, fw3, fw3, deltas)


def _spectrum_kernel(ch_ref, cl_ref, sh_ref, sl_ref, hs_ref, hd_ref, kre_ref, kim_ref):
    hs_hi, hs_lo = _split_bf16(hs_ref[...])
    hd_hi, hd_lo = _split_bf16(hd_ref[...])
    kre_ref[...] = _dot3(ch_ref[...], cl_ref[...], hs_hi, hs_lo)
    kim_ref[...] = _dot3(sh_ref[...], sl_ref[...], hd_hi, hd_lo)


def _spectrum(c0h, c0l, s0h, s0l, hs, hd, tk=512, tn=256):
    L, dh = hs.shape
    tk = min(tk, L)
    mat = pl.BlockSpec((tk, L), lambda j, i: (i, 0))
    col = pl.BlockSpec((L, tn), lambda j, i: (0, j))
    out = pl.BlockSpec((tk, tn), lambda j, i: (i, j))
    return pl.pallas_call(
        _spectrum_kernel,
        grid=(dh // tn, L // tk),
        in_specs=[mat, mat, mat, mat, col, col],
        out_specs=[out, out],
        out_shape=[jax.ShapeDtypeStruct((L, dh), F32)] * 2,
        compiler_params=_params(("parallel", "parallel"), 40 * 1024 * 1024),
        name="filter_spectrum",
    )(c0h, c0l, s0h, s0l, hs, hd)


def _longconv_kernel(inv_scale, cs_ref, ss_ref, v_ref, x0_ref, kre_ref, kim_ref, bias_ref, y_ref):
    cs = cs_ref[...]
    ss = ss_ref[...]
    v = v_ref[...]
    vb = v.astype(BF16)
    xc = _dot(cs, vb)
    xs = _dot(ss, vb)
    kre = kre_ref[...]
    kim = kim_ref[...]
    yre = (xc * kre + xs * kim).astype(BF16)
    yim = (xc * kim - xs * kre).astype(BF16)
    conv = (_dot(cs, yre) - _dot(ss, yim)) * inv_scale
    y_ref[...] = (x0_ref[...] * (conv + bias_ref[...] * v)).astype(y_ref.dtype)


def _longconv(cs, ss, vv, x0, kre, kim, bias, tc=256):
    B, L, dh = vv.shape
    mat = pl.BlockSpec((L, L), lambda j, b: (0, 0), pipeline_mode=pl.Buffered(1))
    act = pl.BlockSpec((None, L, tc), lambda j, b: (b, 0, j))
    spec = pl.BlockSpec((L, tc), lambda j, b: (0, j))
    return pl.pallas_call(
        functools.partial(_longconv_kernel, 1.0 / L),
        grid=(dh // tc, B),
        in_specs=[mat, mat, act, act, spec, spec, pl.BlockSpec((1, tc), lambda j, b: (0, j))],
        out_specs=act,
        out_shape=jax.ShapeDtypeStruct((B, L, dh), BF16),
        compiler_params=_params(("parallel", "parallel"),
                                4 * L * L + 16 * L * tc + 16 * L * tc + 4 * L * tc + 40 * L * tc),
        name="hyena_longconv",
    )(cs, ss, vv, x0, kre, kim, bias)


def _merge_kernel(h_ref, yh_ref, ya_ref, wgh_ref, wga_ref, wh_ref, wa_ref, o_ref):
    h = h_ref[...]
    gh = jax.nn.sigmoid(_dot(h, wgh_ref[...]))
    ga = jax.nn.sigmoid(_dot(h, wga_ref[...]))
    o_ref[...] = (gh * _dot(yh_ref[...], wh_ref[...])
                  + ga * _dot(ya_ref[...], wa_ref[...])).astype(o_ref.dtype)


def _merge(hb, y_hy, y_att, w_in_bf, w_hy, w_att, layer, gate_col0, tm=1024, tn=256):
    T, D = hb.shape
    dh, da = y_hy.shape[1], y_att.shape[1]
    tm = min(tm, T)
    g0 = gate_col0 // tn
    nj = D // tn
    return pl.pallas_call(
        _merge_kernel,
        grid=(T // tm, nj),
        in_specs=[pl.BlockSpec((tm, D), lambda i, j: (i, 0)),
                  pl.BlockSpec((tm, dh), lambda i, j: (i, 0)),
                  pl.BlockSpec((tm, da), lambda i, j: (i, 0)),
                  pl.BlockSpec((None, D, tn), lambda i, j: (layer, 0, g0 + j)),
                  pl.BlockSpec((None, D, tn), lambda i, j: (layer, 0, g0 + nj + j)),
                  pl.BlockSpec((None, dh, tn), lambda i, j: (layer, 0, j)),
                  pl.BlockSpec((None, da, tn), lambda i, j: (layer, 0, j))],
        out_specs=pl.BlockSpec((tm, tn), lambda i, j: (i, j)),
        out_shape=jax.ShapeDtypeStruct((T, D), BF16),
        compiler_params=_params(("parallel", "parallel"),
                                4 * tm * (D + dh + da) + 4 * tn * (2 * D + dh + da) + 4 * tm * tn
                                + 32 * tm * tn),
        name="gated_merge",
    )(hb, y_hy, y_att, w_in_bf, w_in_bf, w_hy, w_att)


def _oproj_ln_kernel(alpha, x_ref, w_ref, h_ref, g_ref, b_ref, of_ref, ob_ref):
    y = alpha * h_ref[...] + _dot(x_ref[...], w_ref[...])
    y = _layer_norm_rows(y, g_ref[...], b_ref[...])
    of_ref[...] = y
    ob_ref[...] = y.astype(BF16)


def _oproj_ln(xb, w_o, h, g, b, layer, alpha, tm=256):
    T, D = h.shape
    row = lambda i: (i, 0)
    vec = pl.BlockSpec((None, 1, D), lambda i: (layer, 0, 0))
    return pl.pallas_call(
        functools.partial(_oproj_ln_kernel, alpha),
        grid=(T // tm,),
        in_specs=[pl.BlockSpec((tm, D), row),
                  pl.BlockSpec((None, D, D), lambda i: (layer, 0, 0)),
                  pl.BlockSpec((tm, D), row), vec, vec],
        out_specs=[pl.BlockSpec((tm, D), row), pl.BlockSpec((tm, D), row)],
        out_shape=[jax.ShapeDtypeStruct((T, D), F32), jax.ShapeDtypeStruct((T, D), BF16)],
        compiler_params=_params(("parallel",), 4 * D * D + 24 * tm * D + 24 * tm * D),
        name="oproj_resid_ln",
    )(xb, w_o, h, g, b)


def _swiglu_partial(x_ref, wg_ref, wu_ref, wd_ref):
    x = x_ref[...]
    g = _dot(x, wg_ref[...].astype(BF16))
    u = _dot(x, wu_ref[...].astype(BF16))
    a = (g * jax.nn.sigmoid(g) * u).astype(BF16)
    return _dot(a, wd_ref[...].astype(BF16))


def _ffn_dense_kernel(alpha, x_ref, wg_ref, wu_ref, wd_ref, h_ref, g_ref, b_ref,
                      of_ref, ob_ref, acc_ref):
    f = pl.program_id(1)
    part = _swiglu_partial(x_ref, wg_ref, wu_ref, wd_ref)

    @pl.when(f == 0)
    def _():
        acc_ref[...] = part

    @pl.when(f > 0)
    def _():
        acc_ref[...] += part

    @pl.when(f == pl.num_programs(1) - 1)
    def _():
        y = _layer_norm_rows(alpha * h_ref[...] + acc_ref[...], g_ref[...], b_ref[...])
        of_ref[...] = y
        ob_ref[...] = y.astype(BF16)


def _ffn_dense(xb, wg, wu, wd, h, g, b, j, layer, alpha, tm=512, tf=512):
    T, D = xb.shape
    F = wg.shape[-1]
    row = lambda i, f: (i, 0)
    vec = pl.BlockSpec((None, 1, D), lambda i, f: (layer, 0, 0))
    return pl.pallas_call(
        functools.partial(_ffn_dense_kernel, alpha),
        grid=(T // tm, F // tf),
        in_specs=[pl.BlockSpec((tm, D), row),
                  pl.BlockSpec((None, D, tf), lambda i, f: (j, 0, f)),
                  pl.BlockSpec((None, D, tf), lambda i, f: (j, 0, f)),
                  pl.BlockSpec((None, tf, D), lambda i, f: (j, f, 0)),
                  pl.BlockSpec((tm, D), row), vec, vec],
        out_specs=[pl.BlockSpec((tm, D), row), pl.BlockSpec((tm, D), row)],
        out_shape=[jax.ShapeDtypeStruct((T, D), F32), jax.ShapeDtypeStruct((T, D), BF16)],
        scratch_shapes=[pltpu.VMEM((tm, D), F32)],
        compiler_params=_params(("parallel", "arbitrary"),
                                4 * tm * D + 12 * D * tf + 8 * tm * D + 12 * tm * D + 4 * tm * D
                                + 16 * tm * tf + 8 * tm * D),
        name="ffn_dense_swiglu_ln",
    )(xb, wg, wu, wd, h, g, b)


def _ffn_group_kernel(be_ref, nv_ref, x_ref, wg_ref, wu_ref, wd_ref, sw_ref, o_ref):
    i = pl.program_id(0)
    f = pl.program_id(1)
    nf = pl.num_programs(1)

    @pl.when(i < nv_ref[0])
    def _():
        part = _swiglu_partial(x_ref, wg_ref, wu_ref, wd_ref)

        @pl.when(f == 0)
        def _():
            o_ref[...] = part

        @pl.when(jnp.logical_and(f > 0, f < nf - 1))
        def _():
            o_ref[...] += part

        @pl.when(jnp.logical_and(f > 0, f == nf - 1))
        def _():
            o_ref[...] = (o_ref[...] + part) * sw_ref[...]


def _ffn_grouped(blk_e, n_valid, xg, wg, wu, wd, sw, j, tm, tf=256):
    P, D = xg.shape
    F = wg.shape[-1]
    nf = F // tf

    def rows(i, f, be, nv):
        return (jnp.minimum(i, nv[0] - 1), 0)

    def hidden(i, f, nv):
        return jnp.where(i < nv[0], f, nf - 1)

    return pl.pallas_call(
        _ffn_group_kernel,
        grid_spec=pltpu.PrefetchScalarGridSpec(
            num_scalar_prefetch=2,
            grid=(P // tm, nf),
            in_specs=[pl.BlockSpec((tm, D), rows),
                      pl.BlockSpec((None, None, D, tf),
                                   lambda i, f, be, nv: (j, be[i], 0, hidden(i, f, nv))),
                      pl.BlockSpec((None, None, D, tf),
                                   lambda i, f, be, nv: (j, be[i], 0, hidden(i, f, nv))),
                      pl.BlockSpec((None, None, tf, D),
                                   lambda i, f, be, nv: (j, be[i], hidden(i, f, nv), 0)),
                      pl.BlockSpec((tm, 1), rows)],
            out_specs=pl.BlockSpec((tm, D), rows)),
        out_shape=jax.ShapeDtypeStruct((P, D), F32),
        compiler_params=_params(("arbitrary", "arbitrary"),
                                4 * tm * D + 24 * D * tf + 8 * tm * D + 1024 * tm
                                + 6 * D * tf + 16 * tm * tf + 8 * tm * D),
        name="moe_grouped_swiglu",
    )(blk_e, n_valid, xg, wg, wu, wd, sw)


def _router_kernel(x_ref, wh_ref, wl_ref, b_ref, e_ref, w_ref):
    x_hi, x_lo = _split_bf16(x_ref[...])
    nt = (((1,), (1,)), ((), ()))

    def dg(a, b):
        return lax.dot_general(a, b, nt, preferred_element_type=F32)

    wh = wh_ref[...]
    wl = wl_ref[...]
    logits = dg(wh, x_hi) + (dg(wh, x_lo) + dg(wl, x_hi)) + b_ref[...]
    n_e = logits.shape[0]
    eid = lax.broadcasted_iota(jnp.int32, logits.shape, 0)
    m1 = jnp.max(logits, axis=0, keepdims=True)
    i1 = jnp.min(jnp.where(logits == m1, eid, n_e), axis=0, keepdims=True)
    rest = jnp.where(eid == i1, -jnp.inf, logits)
    m2 = jnp.max(rest, axis=0, keepdims=True)
    i2 = jnp.min(jnp.where(rest == m2, eid, n_e), axis=0, keepdims=True)
    t = jnp.exp(m2 - m1)
    w1 = 1.0 / (1.0 + t)
    e_ref[...] = jnp.concatenate([i1, i2], axis=0)
    w_ref[...] = jnp.concatenate([w1, t * w1], axis=0)


def _router(h, rw_hi, rw_lo, rb, tm=512):
    T, D = h.shape
    E = rw_hi.shape[0]
    const = lambda i: (0, 0)
    out = pl.BlockSpec((TOP_K, tm), lambda i: (0, i))
    return pl.pallas_call(
        _router_kernel,
        grid=(T // tm,),
        in_specs=[pl.BlockSpec((tm, D), lambda i: (i, 0)), pl.BlockSpec((E, D), const),
                  pl.BlockSpec((E, D), const), pl.BlockSpec((E, 1), const)],
        out_specs=[out, out],
        out_shape=[jax.ShapeDtypeStruct((TOP_K, T), jnp.int32),
                   jax.ShapeDtypeStruct((TOP_K, T), F32)],
        compiler_params=_params(("parallel",), 32 * 1024 * 1024),
        name="moe_router_top2",
    )(h, rw_hi, rw_lo, rb)


def _dft_tables(L):
    n2 = 2 * L
    k = jnp.arange(L, dtype=jnp.int32)[:, None]
    n = jnp.arange(L, dtype=jnp.int32)[None, :]
    ang = (((2 * k + 1) * (2 * n + 1)) % (4 * n2)).astype(F32) * (math.pi / (2 * n2))
    ang0 = (((2 * k + 1) * n) % (2 * n2)).astype(F32) * (math.pi / n2)
    return jnp.cos(ang), jnp.sin(ang), jnp.cos(ang0), jnp.sin(ang0)


def _rope_tables(L, hd):
    rows = L // GRID_W
    row = jnp.repeat(jnp.arange(rows, dtype=F32), GRID_W)
    col = jnp.tile(jnp.arange(GRID_W, dtype=F32), rows)
    n_pairs = hd // 4
    inv = ROPE_THETA ** (-jnp.arange(n_pairs, dtype=F32) / n_pairs)
    ang = jnp.concatenate([row[:, None] * inv, col[:, None] * inv], axis=-1)
    cos, sin = jnp.cos(ang), jnp.sin(ang)
    reps = V7X_LANES // hd
    return (jnp.tile(jnp.concatenate([cos, cos], axis=-1), (1, reps)),
            jnp.tile(jnp.concatenate([-sin, sin], axis=-1), (1, reps)))


def _filter_features(L, n_emb):
    t = jnp.linspace(0.0, 1.0, L, dtype=F32)[:, None]
    bands = (n_emb - 1) // 2
    w = 2.0 * math.pi * jnp.arange(L, dtype=F32)[:, None] / L
    f = jnp.linspace(1e-4, bands - 1, bands, dtype=F32)[None, :]
    z = jnp.concatenate([t, jnp.cos(f * w), -jnp.sin(f * w)], axis=-1)
    return jnp.pad(z, ((0, 0), (0, V7X_LANES - n_emb)))


def _moe_block_rows(T):
    return min(1024, max(V7X_MXU_DIM, T // 4))


def _moe(hf, hb, router_w, router_b, wg, wu, wd, j):
    T, D = hf.shape
    E = router_w.shape[1]
    rw_hi, rw_lo = _split_bf16(router_w.T)
    top_e, top_w = _router(hf, rw_hi, rw_lo, router_b.reshape(E, 1))

    tm = _moe_block_rows(T)
    A = T * TOP_K
    flat_e = top_e.T.reshape(A)
    flat_w = top_w.T.reshape(A)
    onehot = (flat_e[:, None] == jnp.arange(E, dtype=jnp.int32)[None, :]).astype(jnp.int32)
    csum = jnp.cumsum(onehot, axis=0)
    counts = csum[-1]
    rank = jnp.take_along_axis(csum - onehot, flat_e[:, None], axis=1)[:, 0]
    padded = (counts + tm - 1) // tm * tm
    pad_end = jnp.cumsum(padded)
    pad_start = pad_end - padded
    dest = pad_start[flat_e] + rank
    nblk = -(-(A + E * (tm - 1)) // tm)
    P = nblk * tm
    src_tok = jnp.zeros((P,), jnp.int32).at[dest].set(jnp.arange(A, dtype=jnp.int32) // TOP_K)
    src_w = jnp.zeros((P,), F32).at[dest].set(flat_w)
    n_valid = (pad_end[-1] // tm).astype(jnp.int32)
    blk_first = jnp.arange(nblk, dtype=jnp.int32) * tm
    blk_e = jnp.minimum(jnp.searchsorted(pad_end, blk_first, side='right'), E - 1).astype(jnp.int32)
    blk_e = jnp.where(jnp.arange(nblk) < n_valid, blk_e, blk_e[jnp.maximum(n_valid - 1, 0)])

    xg = hb[src_tok]
    yg = _ffn_grouped(blk_e, n_valid.reshape(1), xg, wg, wu, wd, src_w.reshape(P, 1), j, tm)
    d2 = dest.reshape(T, TOP_K)
    return yg[d2[:, 0]], yg[d2[:, 1]]


def kernel(x, ln_in_g, ln_in_b, w_in, hy_conv_w, hy_conv_b, hy_fw1, hy_fb1, hy_fw2, hy_fb2, hy_fw3, hy_freq, hy_bias, q_norm_g, k_norm_g, w_hy_br, w_att_br, w_o, ln_mix_g, ln_mix_b, ffn_w_gate, ffn_w_up, ffn_w_down, router_w, router_b, exp_w_gate, exp_w_up, exp_w_down, ln_ffn_g, ln_ffn_b):
    B, L, D = x.shape
    T = B * L
    depth = w_in.shape[0]
    dh = w_hy_br.shape[1]
    d_attn = w_att_br.shape[1]
    hd = q_norm_g.shape[-1]
    n_q = d_attn // hd
    d_kv = (w_in.shape[2] - 3 * dh - d_attn - 2 * D) // 2
    n_kv = d_kv // hd
    alpha = (2 * depth) ** 0.25
    qkv0 = 3 * dh
    gate0 = qkv0 + d_attn + 2 * d_kv
    n_rot = (n_q + n_kv) * hd
    assert L % GRID_W == 0 and V7X_LANES % hd == 0 and n_rot % V7X_MXU_DIM == 0

    cs, ss, c0, s0 = _dft_tables(L)
    cs_bf, ss_bf = cs.astype(BF16), ss.astype(BF16)
    c0h, c0l = _split_bf16(c0)
    s0h, s0l = _split_bf16(s0)
    cos_t, sin_t = _rope_tables(L, hd)
    z = _filter_features(L, hy_fw1.shape[1])
    min_decay = math.log(DECAY_TARGET) / SLOW_DECAY_PCT
    max_decay = math.log(DECAY_TARGET) / FAST_DECAY_PCT
    deltas = jnp.abs(jnp.linspace(min_decay, max_decay, dh, dtype=F32)).reshape(1, dh)
    lane = jnp.arange(V7X_MXU_DIM)
    bd = jnp.where((lane[:, None] // hd) == (lane[None, :] // hd), 1.0 / hd, 0.0).astype(BF16)
    qk_scale = jnp.concatenate([jnp.full((n_q * hd,), hd ** -0.5, F32),
                                jnp.ones((n_kv * hd,), F32)]).reshape(1, n_rot)

    w_in_bf = w_in.astype(BF16)
    w_hy_bf = w_hy_br.astype(BF16)
    w_att_bf = w_att_br.astype(BF16)
    w_o_bf = w_o.astype(BF16)
    ffn_g_bf = ffn_w_gate.astype(BF16)
    ffn_u_bf = ffn_w_up.astype(BF16)
    ffn_d_bf = ffn_w_down.astype(BF16)
    conv_b3 = hy_conv_b.reshape(depth, 1, 3 * dh)
    ln_mix_g3, ln_mix_b3 = ln_mix_g.reshape(depth, 1, D), ln_mix_b.reshape(depth, 1, D)
    ln_ffn_g3, ln_ffn_b3 = ln_ffn_g.reshape(depth, 1, D), ln_ffn_b.reshape(depth, 1, D)

    hf, hb = _resid_ln(x.reshape(T, D), [], ln_in_g, ln_in_b, alpha)
    for i in range(depth):
        x0, vv = _hy_proj(hb.reshape(B, L, D), w_in_bf, hy_conv_w, conv_b3, i, dh)
        fw1p = jnp.pad(hy_fw1[i], ((0, V7X_LANES - hy_fw1.shape[1]), (0, 0)))
        hs, hdf = _filters(z, fw1p, hy_fb1[i].reshape(1, -1), hy_fw2[i], hy_fb2[i].reshape(1, -1),
                           hy_freq[i], hy_fw3[i], deltas, dh)
        kre, kim = _spectrum(c0h, c0l, s0h, s0l, hs, hdf)
        y_hy = _longconv(cs_bf, ss_bf, vv, x0, kre, kim, hy_bias[i].reshape(1, dh))
        gain = jnp.concatenate([jnp.tile(q_norm_g[i], n_q), jnp.tile(k_norm_g[i], n_kv)]).reshape(1, n_rot)
        w_qkv = lax.slice_in_dim(w_in_bf[i], qkv0, gate0, axis=1)
        q, k, v = _qkv_proj(hb, w_qkv, bd, gain, qk_scale, cos_t, sin_t, B, L, n_q, n_kv, hd)
        y_att = _attention(q, k, v)
        merged = _merge(hb, y_hy.reshape(T, dh), y_att.reshape(T, d_attn), w_in_bf, w_hy_bf, w_att_bf,
                        i, gate0)
        hf, hb = _oproj_ln(merged, w_o_bf, hf, ln_mix_g3, ln_mix_b3, i, alpha)
        j = i // 2
        if i % 2 == 0:
            hf, hb = _ffn_dense(hb, ffn_g_bf, ffn_u_bf, ffn_d_bf, hf, ln_ffn_g3, ln_ffn_b3, j, i, alpha)
        else:
            ya, yb = _moe(hf, hb, router_w[j], router_b[j], exp_w_gate, exp_w_up, exp_w_down, j)
            hf, hb = _resid_ln(hf, [ya, yb], ln_ffn_g[i], ln_ffn_b[i], alpha)
    return hf.reshape(B, L, D)
```

```python
import functools
import math

import jax
import jax.numpy as jnp
from jax import lax
from jax.experimental import pallas as pl
from jax.experimental.pallas import tpu as pltpu

F32 = jnp.float32
BF16 = jnp.bfloat16

GRID_W = 64
ROPE_THETA = 10000.0
TOP_K = 2
DECAY_TARGET = 1e-2
FAST_DECAY_PCT = 0.3
SLOW_DECAY_PCT = 1.5
LN_EPS = 1e-5
RMS_EPS = 1e-6

V7X_VMEM_BYTES = 64 * 1024 * 1024
V7X_LANES = 128
V7X_MXU_DIM = 256
VMEM_BUDGET = V7X_VMEM_BYTES - 8 * 1024 * 1024


def _params(semantics, vmem_bytes):
    return pltpu.CompilerParams(dimension_semantics=semantics,
                                vmem_limit_bytes=min(int(vmem_bytes), VMEM_BUDGET))


def _dot(a, b):
    return jnp.dot(a, b, preferred_element_type=F32)


def _split_bf16(a):
    hi = a.astype(BF16)
    lo = (a - hi.astype(F32)).astype(BF16)
    return hi, lo


def _dot3(a_hi, a_lo, b_hi, b_lo):
    return _dot(a_hi, b_hi) + (_dot(a_hi, b_lo) + _dot(a_lo, b_hi))


def _layer_norm_rows(x, g, b):
    mu = jnp.mean(x, axis=-1, keepdims=True)
    xc = x - mu
    var = jnp.mean(xc * xc, axis=-1, keepdims=True)
    return xc * lax.rsqrt(var + LN_EPS) * g + b


def _resid_ln_kernel(alpha, n_add, *refs):
    h_ref = refs[0]
    add_refs = refs[1:1 + n_add]
    g_ref, b_ref, of_ref, ob_ref = refs[1 + n_add:]
    x = h_ref[...]
    if n_add:
        acc = add_refs[0][...]
        for r in add_refs[1:]:
            acc = acc + r[...]
        x = alpha * x + acc
    y = _layer_norm_rows(x, g_ref[...], b_ref[...])
    of_ref[...] = y
    ob_ref[...] = y.astype(BF16)


def _resid_ln(h, addends, g, b, alpha, tm=256):
    T, D = h.shape
    n_add = len(addends)
    row = pl.BlockSpec((tm, D), lambda i: (i, 0))
    vec = pl.BlockSpec((1, D), lambda i: (0, 0))
    return pl.pallas_call(
        functools.partial(_resid_ln_kernel, alpha, n_add),
        grid=(T // tm,),
        in_specs=[row] * (1 + n_add) + [vec, vec],
        out_specs=[row, row],
        out_shape=[jax.ShapeDtypeStruct((T, D), F32), jax.ShapeDtypeStruct((T, D), BF16)],
        compiler_params=_params(("parallel",), (2 * (1 + n_add) * 4 + 12 + 16) * tm * D),
        name="resid_ln",
    )(h, *addends, g.reshape(1, D), b.reshape(1, D))


def _moe_combine_ln_kernel(alpha, h_ref, ya_ref, yb_ref, w_ref, g_ref, b_ref, of_ref, ob_ref):
    w = w_ref[...]
    f = ya_ref[...] * w[:, 0:1] + yb_ref[...] * w[:, 1:2]
    y = _layer_norm_rows(alpha * h_ref[...] + f, g_ref[...], b_ref[...])
    of_ref[...] = y
    ob_ref[...] = y.astype(BF16)


def _moe_combine_ln(h, ya, yb, w, g, b, alpha, tm=256):
    T, D = h.shape
    row = pl.BlockSpec((tm, D), lambda i: (i, 0))
    vec = pl.BlockSpec((1, D), lambda i: (0, 0))
    return pl.pallas_call(
        functools.partial(_moe_combine_ln_kernel, alpha),
        grid=(T // tm,),
        in_specs=[row, row, row, pl.BlockSpec((tm, w.shape[1]), lambda i: (i, 0)), vec, vec],
        out_specs=[row, row],
        out_shape=[jax.ShapeDtypeStruct((T, D), F32), jax.ShapeDtypeStruct((T, D), BF16)],
        compiler_params=_params(("parallel",), 64 * tm * D),
        name="moe_combine_ln",
    )(h, ya, yb, w, g.reshape(1, D), b.reshape(1, D))


def _hy_proj_kernel(h_ref, w0_ref, w1_ref, w2_ref, cw0_ref, cw1_ref, cw2_ref,
                    cb0_ref, cb1_ref, cb2_ref, x0_ref, vv_ref):
    h = h_ref[...]
    L = h.shape[0]
    row = lax.broadcasted_iota(jnp.int32, (L, 1), 0)

    def stream(w_ref, cw_ref, cb_ref):
        p = _dot(h, w_ref[...])
        prev = jnp.where(row == 0, 0.0, pltpu.roll(p, 1, 0))
        nxt = jnp.where(row == L - 1, 0.0, pltpu.roll(p, L - 1, 0))
        cw = cw_ref[...]
        return prev * cw[0:1] + p * cw[1:2] + nxt * cw[2:3] + cb_ref[...]

    x0_ref[...] = stream(w0_ref, cw0_ref, cb0_ref)
    vv_ref[...] = stream(w1_ref, cw1_ref, cb1_ref) * stream(w2_ref, cw2_ref, cb2_ref)


def _hy_proj(hb3, w_in_bf, conv_w, conv_b3, layer, dh, tn=256):
    B, L, D = hb3.shape
    nj = dh // tn

    def wspec(s):
        return pl.BlockSpec((None, D, tn), lambda j, b: (layer, 0, s * nj + j))

    def cwspec(s):
        return pl.BlockSpec((None, conv_w.shape[1], tn), lambda j, b: (layer, 0, s * nj + j))

    def cbspec(s):
        return pl.BlockSpec((None, 1, tn), lambda j, b: (layer, 0, s * nj + j))

    out = pl.BlockSpec((None, L, tn), lambda j, b: (b, 0, j))
    return pl.pallas_call(
        _hy_proj_kernel,
        grid=(nj, B),
        in_specs=[pl.BlockSpec((None, L, D), lambda j, b: (b, 0, 0)),
                  wspec(0), wspec(1), wspec(2), cwspec(0), cwspec(1), cwspec(2),
                  cbspec(0), cbspec(1), cbspec(2)],
        out_specs=[out, out],
        out_shape=[jax.ShapeDtypeStruct((B, L, dh), F32)] * 2,
        compiler_params=_params(("parallel", "parallel"),
                                4 * L * D + 12 * D * tn + 16 * L * tn + 40 * L * tn),
        name="hyena_proj_conv",
    )(hb3, w_in_bf, w_in_bf, w_in_bf, conv_w, conv_w, conv_w, conv_b3, conv_b3, conv_b3)


def _qkv_kernel(n_q, n_kv, hd, x_ref, w_ref, bd_ref, gain_ref, scale_ref, cos_ref, sin_ref,
                q_ref, k_ref, v_ref):
    n_rot = (n_q + n_kv) * hd
    p = _dot(x_ref[...], w_ref[...])
    pr = p[:, :n_rot]
    sq = (pr * pr).astype(BF16)
    bd = bd_ref[...]
    gw = bd.shape[0]
    ms = jnp.concatenate([_dot(sq[:, c:c + gw], bd) for c in range(0, n_rot, gw)], axis=1)
    xn = pr * lax.rsqrt(ms + RMS_EPS) * gain_ref[...]
    reps = n_rot // V7X_LANES
    cos = jnp.concatenate([cos_ref[...]] * reps, axis=1)
    sin = jnp.concatenate([sin_ref[...]] * reps, axis=1)
    lane = lax.broadcasted_iota(jnp.int32, (1, n_rot), 1)
    first_half = (lane % hd) < (hd // 2)
    half = hd // 2
    swapped = jnp.where(first_half, pltpu.roll(xn, n_rot - half, 1), pltpu.roll(xn, half, 1))
    rot = ((xn * cos + swapped * sin) * scale_ref[...]).astype(BF16)
    for hh in range(n_q):
        q_ref[hh] = rot[:, hh * hd:(hh + 1) * hd]
    for hh in range(n_kv):
        k_ref[hh] = rot[:, (n_q + hh) * hd:(n_q + hh + 1) * hd]
        v_ref[hh] = p[:, n_rot + hh * hd:n_rot + (hh + 1) * hd].astype(BF16)


def _qkv_proj(hb, w_qkv, bd, gain, scale, cos_t, sin_t, B, L, n_q, n_kv, hd, tm=512):
    T, D = hb.shape
    nw = w_qkv.shape[1]
    n_rot = (n_q + n_kv) * hd
    tm = min(tm, L)
    tpb = L // tm
    const = lambda i: (0, 0)
    return pl.pallas_call(
        functools.partial(_qkv_kernel, n_q, n_kv, hd),
        grid=(T // tm,),
        in_specs=[pl.BlockSpec((tm, D), lambda i: (i, 0)),
                  pl.BlockSpec((D, nw), const),
                  pl.BlockSpec(bd.shape, const),
                  pl.BlockSpec((1, n_rot), const),
                  pl.BlockSpec((1, n_rot), const),
                  pl.BlockSpec((tm, V7X_LANES), lambda i: (i % tpb, 0)),
                  pl.BlockSpec((tm, V7X_LANES), lambda i: (i % tpb, 0))],
        out_specs=[pl.BlockSpec((None, n_q, tm, hd), lambda i: (i // tpb, 0, i % tpb, 0)),
                   pl.BlockSpec((None, n_kv, tm, hd), lambda i: (i // tpb, 0, i % tpb, 0)),
                   pl.BlockSpec((None, n_kv, tm, hd), lambda i: (i // tpb, 0, i % tpb, 0))],
        out_shape=[jax.ShapeDtypeStruct((B, n_q, L, hd), BF16),
                   jax.ShapeDtypeStruct((B, n_kv, L, hd), BF16),
                   jax.ShapeDtypeStruct((B, n_kv, L, hd), BF16)],
        compiler_params=_params(("parallel",),
                                4 * tm * D + 4 * D * nw + 8 * tm * (n_q + 2 * n_kv) * V7X_LANES
                                + 48 * tm * nw),
        name="qkv_proj_norm_rope",
    )(hb, w_qkv, bd, gain, scale, cos_t, sin_t)


def _attn_kernel(group, hd, q_ref, k_ref, v_ref, o_ref):
    k = k_ref[...]
    v = v_ref[...]
    for g in range(group):
        s = lax.dot_general(q_ref[g], k, (((1,), (1,)), ((), ())),
                            preferred_element_type=F32)
        m = jnp.max(s, axis=-1, keepdims=True)
        e = jnp.exp(s - m)
        denom = jnp.sum(e, axis=-1, keepdims=True)
        o = _dot(e.astype(BF16), v)
        o_ref[:, g * hd:(g + 1) * hd] = (o / denom).astype(o_ref.dtype)


def _attention(q, k, v, tq=512):
    B, n_q, L, hd = q.shape
    n_kv = k.shape[1]
    group = n_q // n_kv
    tq = min(tq, L)
    return pl.pallas_call(
        functools.partial(_attn_kernel, group, hd),
        grid=(B, n_kv, L // tq),
        in_specs=[pl.BlockSpec((None, group, tq, hd), lambda b, j, t: (b, j, t, 0)),
                  pl.BlockSpec((None, None, L, hd), lambda b, j, t: (b, j, 0, 0)),
                  pl.BlockSpec((None, None, L, hd), lambda b, j, t: (b, j, 0, 0))],
        out_specs=pl.BlockSpec((None, tq, group * hd), lambda b, j, t: (b, t, j)),
        out_shape=jax.ShapeDtypeStruct((B, L, n_q * hd), BF16),
        compiler_params=_params(("parallel", "parallel", "parallel"),
                                8 * L * V7X_LANES + 8 * group * tq * V7X_LANES + 40 * tq * L),
        name="gqa_attention",
    )(q, k, v)


def _filter_kernel(z_ref, fw1_ref, fb1_ref, fw2_ref, fb2_ref, fr_ref, w3f_ref, w3b_ref, dl_ref,
                   hs_ref, hd_ref):
    z = z_ref[...]
    L = z.shape[0]
    fr = fr_ref[...]

    def hp_dot(a, b):
        a_hi, a_lo = _split_bf16(a)
        b_hi, b_lo = _split_bf16(b)
        return _dot3(a_hi, a_lo, b_hi, b_lo)

    a = jnp.sin(fr[0:1] * (hp_dot(z, fw1_ref[...]) + fb1_ref[...]))
    a = jnp.sin(fr[1:2] * (hp_dot(a, fw2_ref[...]) + fb2_ref[...]))
    window = jnp.exp(-z[:, 0:1] * dl_ref[...])
    h_fwd = hp_dot(a, w3f_ref[...]) * window
    h_bwd = hp_dot(a, w3b_ref[...]) * window
    row = lax.broadcasted_iota(jnp.int32, (L, 1), 0)
    h_bwd0 = jnp.where(row == 0, 0.0, h_bwd)
    hs_ref[...] = h_fwd + h_bwd0
    hd_ref[...] = h_bwd0 - h_fwd


def _filters(z, fw1p, fb1, fw2, fb2, fr, fw3, deltas, dh, tn=256):
    L, zw = z.shape
    nh = fw2.shape[0]
    nj = dh // tn
    const = lambda j: (0, 0)
    out = pl.BlockSpec((L, tn), lambda j: (0, j))
    return pl.pallas_call(
        _filter_kernel,
        grid=(nj,),
        in_specs=[pl.BlockSpec((L, zw), const), pl.BlockSpec((zw, nh), const),
                  pl.BlockSpec((1, nh), const), pl.BlockSpec((nh, nh), const),
                  pl.BlockSpec((1, nh), const), pl.BlockSpec((2, nh), const),
                  pl.BlockSpec((nh, tn), lambda j: (0, j)),
                  pl.BlockSpec((nh, tn), lambda j: (0, nj + j)),
                  pl.BlockSpec((1, tn), lambda j: (0, j))],
        out_specs=[out, out],
        out_shape=[jax.ShapeDtypeStruct((L, dh), F32)] * 2,
        compiler_params=_params(("parallel",), 32 * 1024 * 1024),
        name="hyena_filters",
    )(z, fw1p, fb1, fw2, fb2, fr, fw3, fw3, deltas)


def _spectrum_kernel(ch_ref, cl_ref, sh_ref, sl_ref, hs_ref, hd_ref, kre_ref, kim_ref):
    hs_hi, hs_lo = _split_bf16(hs_ref[...])
    hd_hi, hd_lo = _split_bf16(hd_ref[...])
    kre_ref[...] = _dot3(ch_ref[...], cl_ref[...], hs_hi, hs_lo)
    kim_ref[...] = _dot3(sh_ref[...], sl_ref[...], hd_hi, hd_lo)


def _spectrum(c0h, c0l, s0h, s0l, hs, hd, tk=512, tn=256):
    L, dh = hs.shape
    tk = min(tk, L)
    mat = pl.BlockSpec((tk, L), lambda j, i: (i, 0))
    col = pl.BlockSpec((L, tn), lambda j, i: (0, j))
    out = pl.BlockSpec((tk, tn), lambda j, i: (i, j))
    return pl.pallas_call(
        _spectrum_kernel,
        grid=(dh // tn, L // tk),
        in_specs=[mat, mat, mat, mat, col, col],
        out_specs=[out, out],
        out_shape=[jax.ShapeDtypeStruct((L, dh), F32)] * 2,
        compiler_params=_params(("parallel", "parallel"), 40 * 1024 * 1024),
        name="filter_spectrum",
    )(c0h, c0l, s0h, s0l, hs, hd)


def _longconv_kernel(inv_scale, cs_ref, ss_ref, v_ref, x0_ref, kre_ref, kim_ref, bias_ref, y_ref):
    cs = cs_ref[...]
    ss = ss_ref[...]
    v = v_ref[...]
    vb = v.astype(BF16)
    xc = _dot(cs, vb)
    xs = _dot(ss, vb)
    kre = kre_ref[...]
    kim = kim_ref[...]
    yre = (xc * kre + xs * kim).astype(BF16)
    yim = (xc * kim - xs * kre).astype(BF16)
    conv = (_dot(cs, yre) - _dot(ss, yim)) * inv_scale
    y_ref[...] = (x0_ref[...] * (conv + bias_ref[...] * v)).astype(y_ref.dtype)


def _longconv(cs, ss, vv, x0, kre, kim, bias, tc=256):
    B, L, dh = vv.shape
    mat = pl.BlockSpec((L, L), lambda j, b: (0, 0), pipeline_mode=pl.Buffered(1))
    act = pl.BlockSpec((None, L, tc), lambda j, b: (b, 0, j))
    spec = pl.BlockSpec((L, tc), lambda j, b: (0, j))
    return pl.pallas_call(
        functools.partial(_longconv_kernel, 1.0 / L),
        grid=(dh // tc, B),
        in_specs=[mat, mat, act, act, spec, spec, pl.BlockSpec((1, tc), lambda j, b: (0, j))],
        out_specs=act,
        out_shape=jax.ShapeDtypeStruct((B, L, dh), BF16),
        compiler_params=_params(("parallel", "parallel"),
                                4 * L * L + 16 * L * tc + 16 * L * tc + 4 * L * tc + 40 * L * tc),
        name="hyena_longconv",
    )(cs, ss, vv, x0, kre, kim, bias)


def _merge_kernel(h_ref, yh_ref, ya_ref, wgh_ref, wga_ref, wh_ref, wa_ref, o_ref):
    h = h_ref[...]
    gh = jax.nn.sigmoid(_dot(h, wgh_ref[...]))
    ga = jax.nn.sigmoid(_dot(h, wga_ref[...]))
    o_ref[...] = (gh * _dot(yh_ref[...], wh_ref[...])
                  + ga * _dot(ya_ref[...], wa_ref[...])).astype(o_ref.dtype)


def _merge(hb, y_hy, y_att, w_in_bf, w_hy, w_att, layer, gate_col0, tm=1024, tn=256):
    T, D = hb.shape
    dh, da = y_hy.shape[1], y_att.shape[1]
    tm = min(tm, T)
    g0 = gate_col0 // tn
    nj = D // tn
    return pl.pallas_call(
        _merge_kernel,
        grid=(T // tm, nj),
        in_specs=[pl.BlockSpec((tm, D), lambda i, j: (i, 0)),
                  pl.BlockSpec((tm, dh), lambda i, j: (i, 0)),
                  pl.BlockSpec((tm, da), lambda i, j: (i, 0)),
                  pl.BlockSpec((None, D, tn), lambda i, j: (layer, 0, g0 + j)),
                  pl.BlockSpec((None, D, tn), lambda i, j: (layer, 0, g0 + nj + j)),
                  pl.BlockSpec((None, dh, tn), lambda i, j: (layer, 0, j)),
                  pl.BlockSpec((None, da, tn), lambda i, j: (layer, 0, j))],
        out_specs=pl.BlockSpec((tm, tn), lambda i, j: (i, j)),
        out_shape=jax.ShapeDtypeStruct((T, D), BF16),
        compiler_params=_params(("parallel", "parallel"),
                                4 * tm * (D + dh + da) + 4 * tn * (2 * D + dh + da) + 4 * tm * tn
                                + 32 * tm * tn),
        name="gated_merge",
    )(hb, y_hy, y_att, w_in_bf, w_in_bf, w_hy, w_att)


def _oproj_ln_kernel(alpha, x_ref, w_ref, h_ref, g_ref, b_ref, of_ref, ob_ref):
    y = alpha * h_ref[...] + _dot(x_ref[...], w_ref[...])
    y = _layer_norm_rows(y, g_ref[...], b_ref[...])
    of_ref[...] = y
    ob_ref[...] = y.astype(BF16)


def _oproj_ln(xb, w_o, h, g, b, layer, alpha, tm=256):
    T, D = h.shape
    row = lambda i: (i, 0)
    vec = pl.BlockSpec((None, 1, D), lambda i: (layer, 0, 0))
    return pl.pallas_call(
        functools.partial(_oproj_ln_kernel, alpha),
        grid=(T // tm,),
        in_specs=[pl.BlockSpec((tm, D), row),
                  pl.BlockSpec((None, D, D), lambda i: (layer, 0, 0)),
                  pl.BlockSpec((tm, D), row), vec, vec],
        out_specs=[pl.BlockSpec((tm, D), row), pl.BlockSpec((tm, D), row)],
        out_shape=[jax.ShapeDtypeStruct((T, D), F32), jax.ShapeDtypeStruct((T, D), BF16)],
        compiler_params=_params(("parallel",), 4 * D * D + 24 * tm * D + 24 * tm * D),
        name="oproj_resid_ln",
    )(xb, w_o, h, g, b)


def _swiglu_partial(x_ref, wg_ref, wu_ref, wd_ref):
    x = x_ref[...]
    g = _dot(x, wg_ref[...].astype(BF16))
    u = _dot(x, wu_ref[...].astype(BF16))
    a = (g * jax.nn.sigmoid(g) * u).astype(BF16)
    return _dot(a, wd_ref[...].astype(BF16))


def _ffn_dense_kernel(alpha, x_ref, wg_ref, wu_ref, wd_ref, h_ref, g_ref, b_ref,
                      of_ref, ob_ref, acc_ref):
    f = pl.program_id(1)

    @pl.when(f == 0)
    def _():
        acc_ref[...] = jnp.zeros_like(acc_ref)

    acc_ref[...] += _swiglu_partial(x_ref, wg_ref, wu_ref, wd_ref)

    @pl.when(f == pl.num_programs(1) - 1)
    def _():
        y = _layer_norm_rows(alpha * h_ref[...] + acc_ref[...], g_ref[...], b_ref[...])
        of_ref[...] = y
        ob_ref[...] = y.astype(BF16)


def _ffn_dense(xb, wg, wu, wd, h, g, b, j, layer, alpha, tm=512, tf=512):
    T, D = xb.shape
    F = wg.shape[-1]
    row = lambda i, f: (i, 0)
    vec = pl.BlockSpec((None, 1, D), lambda i, f: (layer, 0, 0))
    return pl.pallas_call(
        functools.partial(_ffn_dense_kernel, alpha),
        grid=(T // tm, F // tf),
        in_specs=[pl.BlockSpec((tm, D), row),
                  pl.BlockSpec((None, D, tf), lambda i, f: (j, 0, f)),
                  pl.BlockSpec((None, D, tf), lambda i, f: (j, 0, f)),
                  pl.BlockSpec((None, tf, D), lambda i, f: (j, f, 0)),
                  pl.BlockSpec((tm, D), row), vec, vec],
        out_specs=[pl.BlockSpec((tm, D), row), pl.BlockSpec((tm, D), row)],
        out_shape=[jax.ShapeDtypeStruct((T, D), F32), jax.ShapeDtypeStruct((T, D), BF16)],
        scratch_shapes=[pltpu.VMEM((tm, D), F32)],
        compiler_params=_params(("parallel", "arbitrary"),
                                4 * tm * D + 12 * D * tf + 8 * tm * D + 12 * tm * D + 4 * tm * D
                                + 16 * tm * tf + 8 * tm * D),
        name="ffn_dense_swiglu_ln",
    )(xb, wg, wu, wd, h, g, b)


def _ffn_group_kernel(be_ref, nv_ref, x_ref, wg_ref, wu_ref, wd_ref, o_ref):
    i = pl.program_id(0)
    f = pl.program_id(1)

    @pl.when(i < nv_ref[0])
    def _():
        @pl.when(f == 0)
        def _():
            o_ref[...] = jnp.zeros_like(o_ref)

        o_ref[...] += _swiglu_partial(x_ref, wg_ref, wu_ref, wd_ref)


def _ffn_grouped(blk_e, n_valid, xg, wg, wu, wd, j, tm, tf=256):
    P, D = xg.shape
    F = wg.shape[-1]
    nf = F // tf

    def rows(i, f, be, nv):
        return (jnp.minimum(i, nv[0] - 1), 0)

    def hidden(i, f, nv):
        return jnp.where(i < nv[0], f, nf - 1)

    return pl.pallas_call(
        _ffn_group_kernel,
        grid_spec=pltpu.PrefetchScalarGridSpec(
            num_scalar_prefetch=2,
            grid=(P // tm, nf),
            in_specs=[pl.BlockSpec((tm, D), rows),
                      pl.BlockSpec((None, None, D, tf),
                                   lambda i, f, be, nv: (j, be[i], 0, hidden(i, f, nv))),
                      pl.BlockSpec((None, None, D, tf),
                                   lambda i, f, be, nv: (j, be[i], 0, hidden(i, f, nv))),
                      pl.BlockSpec((None, None, tf, D),
                                   lambda i, f, be, nv: (j, be[i], hidden(i, f, nv), 0))],
            out_specs=pl.BlockSpec((tm, D), rows)),
        out_shape=jax.ShapeDtypeStruct((P, D), F32),
        compiler_params=_params(("arbitrary", "arbitrary"),
                                4 * tm * D + 24 * D * tf + 8 * tm * D
                                + 6 * D * tf + 16 * tm * tf + 8 * tm * D),
        name="moe_grouped_swiglu",
    )(blk_e, n_valid, xg, wg, wu, wd)


def _router_kernel(x_ref, wh_ref, wl_ref, b_ref, e_ref, w_ref):
    x_hi, x_lo = _split_bf16(x_ref[...])
    nt = (((1,), (1,)), ((), ()))

    def dg(a, b):
        return lax.dot_general(a, b, nt, preferred_element_type=F32)

    wh = wh_ref[...]
    wl = wl_ref[...]
    logits = dg(wh, x_hi) + (dg(wh, x_lo) + dg(wl, x_hi)) + b_ref[...]
    n_e = logits.shape[0]
    eid = lax.broadcasted_iota(jnp.int32, logits.shape, 0)
    m1 = jnp.max(logits, axis=0, keepdims=True)
    i1 = jnp.min(jnp.where(logits == m1, eid, n_e), axis=0, keepdims=True)
    rest = jnp.where(eid == i1, -jnp.inf, logits)
    m2 = jnp.max(rest, axis=0, keepdims=True)
    i2 = jnp.min(jnp.where(rest == m2, eid, n_e), axis=0, keepdims=True)
    t = jnp.exp(m2 - m1)
    w1 = 1.0 / (1.0 + t)
    e_ref[...] = jnp.concatenate([i1, i2], axis=0)
    w_ref[...] = jnp.concatenate([w1, t * w1], axis=0)


def _router(h, rw_hi, rw_lo, rb, tm=512):
    T, D = h.shape
    E = rw_hi.shape[0]
    const = lambda i: (0, 0)
    out = pl.BlockSpec((TOP_K, tm), lambda i: (0, i))
    return pl.pallas_call(
        _router_kernel,
        grid=(T // tm,),
        in_specs=[pl.BlockSpec((tm, D), lambda i: (i, 0)), pl.BlockSpec((E, D), const),
                  pl.BlockSpec((E, D), const), pl.BlockSpec((E, 1), const)],
        out_specs=[out, out],
        out_shape=[jax.ShapeDtypeStruct((TOP_K, T), jnp.int32),
                   jax.ShapeDtypeStruct((TOP_K, T), F32)],
        compiler_params=_params(("parallel",), 32 * 1024 * 1024),
        name="moe_router_top2",
    )(h, rw_hi, rw_lo, rb)


def _dft_tables(L):
    n2 = 2 * L
    k = jnp.arange(L, dtype=jnp.int32)[:, None]
    n = jnp.arange(L, dtype=jnp.int32)[None, :]
    ang = (((2 * k + 1) * (2 * n + 1)) % (4 * n2)).astype(F32) * (math.pi / (2 * n2))
    ang0 = (((2 * k + 1) * n) % (2 * n2)).astype(F32) * (math.pi / n2)
    return jnp.cos(ang), jnp.sin(ang), jnp.cos(ang0), jnp.sin(ang0)


def _rope_tables(L, hd):
    rows = L // GRID_W
    row = jnp.repeat(jnp.arange(rows, dtype=F32), GRID_W)
    col = jnp.tile(jnp.arange(GRID_W, dtype=F32), rows)
    n_pairs = hd // 4
    inv = ROPE_THETA ** (-jnp.arange(n_pairs, dtype=F32) / n_pairs)
    ang = jnp.concatenate([row[:, None] * inv, col[:, None] * inv], axis=-1)
    cos, sin = jnp.cos(ang), jnp.sin(ang)
    reps = V7X_LANES // hd
    return (jnp.tile(jnp.concatenate([cos, cos], axis=-1), (1, reps)),
            jnp.tile(jnp.concatenate([-sin, sin], axis=-1), (1, reps)))


def _filter_features(L, n_emb):
    t = jnp.linspace(0.0, 1.0, L, dtype=F32)[:, None]
    bands = (n_emb - 1) // 2
    w = 2.0 * math.pi * jnp.arange(L, dtype=F32)[:, None] / L
    f = jnp.linspace(1e-4, bands - 1, bands, dtype=F32)[None, :]
    z = jnp.concatenate([t, jnp.cos(f * w), -jnp.sin(f * w)], axis=-1)
    return jnp.pad(z, ((0, 0), (0, V7X_LANES - n_emb)))


def _moe_block_rows(T):
    return min(1024, max(V7X_MXU_DIM, T // 4))


def _moe(hf, hb, router_w, router_b, wg, wu, wd, j):
    T, D = hf.shape
    E = router_w.shape[1]
    rw_hi, rw_lo = _split_bf16(router_w.T)
    top_e, top_w = _router(hf, rw_hi, rw_lo, router_b.reshape(E, 1))

    tm = _moe_block_rows(T)
    A = T * TOP_K
    flat_e = top_e.T.reshape(A)
    onehot = (flat_e[:, None] == jnp.arange(E, dtype=jnp.int32)[None, :]).astype(jnp.int32)
    csum = jnp.cumsum(onehot, axis=0)
    counts = csum[-1]
    rank = jnp.take_along_axis(csum - onehot, flat_e[:, None], axis=1)[:, 0]
    padded = (counts + tm - 1) // tm * tm
    pad_end = jnp.cumsum(padded)
    pad_start = pad_end - padded
    dest = pad_start[flat_e] + rank
    nblk = -(-(A + E * (tm - 1)) // tm)
    P = nblk * tm
    src_tok = jnp.zeros((P,), jnp.int32).at[dest].set(jnp.arange(A, dtype=jnp.int32) // TOP_K)
    n_valid = (pad_end[-1] // tm).astype(jnp.int32)
    blk_first = jnp.arange(nblk, dtype=jnp.int32) * tm
    blk_e = jnp.minimum(jnp.searchsorted(pad_end, blk_first, side='right'), E - 1).astype(jnp.int32)
    blk_e = jnp.where(jnp.arange(nblk) < n_valid, blk_e, blk_e[jnp.maximum(n_valid - 1, 0)])

    xg = hb[src_tok]
    yg = _ffn_grouped(blk_e, n_valid.reshape(1), xg, wg, wu, wd, j, tm)
    d2 = dest.reshape(T, TOP_K)
    return yg[d2[:, 0]], yg[d2[:, 1]], top_w.T


def kernel(x, ln_in_g, ln_in_b, w_in, hy_conv_w, hy_conv_b, hy_fw1, hy_fb1, hy_fw2, hy_fb2, hy_fw3, hy_freq, hy_bias, q_norm_g, k_norm_g, w_hy_br, w_att_br, w_o, ln_mix_g, ln_mix_b, ffn_w_gate, ffn_w_up, ffn_w_down, router_w, router_b, exp_w_gate, exp_w_up, exp_w_down, ln_ffn_g, ln_ffn_b):
    B, L, D = x.shape
    T = B * L
    depth = w_in.shape[0]
    dh = w_hy_br.shape[1]
    d_attn = w_att_br.shape[1]
    hd = q_norm_g.shape[-1]
    n_q = d_attn // hd
    d_kv = (w_in.shape[2] - 3 * dh - d_attn - 2 * D) // 2
    n_kv = d_kv // hd
    alpha = (2 * depth) ** 0.25
    qkv0 = 3 * dh
    gate0 = qkv0 + d_attn + 2 * d_kv
    n_rot = (n_q + n_kv) * hd
    assert L % GRID_W == 0 and V7X_LANES % hd == 0 and n_rot % V7X_MXU_DIM == 0

    cs, ss, c0, s0 = _dft_tables(L)
    cs_bf, ss_bf = cs.astype(BF16), ss.astype(BF16)
    c0h, c0l = _split_bf16(c0)
    s0h, s0l = _split_bf16(s0)
    cos_t, sin_t = _rope_tables(L, hd)
    z = _filter_features(L, hy_fw1.shape[1])
    min_decay = math.log(DECAY_TARGET) / SLOW_DECAY_PCT
    max_decay = math.log(DECAY_TARGET) / FAST_DECAY_PCT
    deltas = jnp.abs(jnp.linspace(min_decay, max_decay, dh, dtype=F32)).reshape(1, dh)
    lane = jnp.arange(V7X_MXU_DIM)
    bd = jnp.where((lane[:, None] // hd) == (lane[None, :] // hd), 1.0 / hd, 0.0).astype(BF16)
    qk_scale = jnp.concatenate([jnp.full((n_q * hd,), hd ** -0.5, F32),
                                jnp.ones((n_kv * hd,), F32)]).reshape(1, n_rot)

    w_in_bf = w_in.astype(BF16)
    w_hy_bf = w_hy_br.astype(BF16)
    w_att_bf = w_att_br.astype(BF16)
    w_o_bf = w_o.astype(BF16)
    ffn_g_bf = ffn_w_gate.astype(BF16)
    ffn_u_bf = ffn_w_up.astype(BF16)
    ffn_d_bf = ffn_w_down.astype(BF16)
    conv_b3 = hy_conv_b.reshape(depth, 1, 3 * dh)
    ln_mix_g3, ln_mix_b3 = ln_mix_g.reshape(depth, 1, D), ln_mix_b.reshape(depth, 1, D)
    ln_ffn_g3, ln_ffn_b3 = ln_ffn_g.reshape(depth, 1, D), ln_ffn_b.reshape(depth, 1, D)

    hf, hb = _resid_ln(x.reshape(T, D), [], ln_in_g, ln_in_b, alpha)
    for i in range(depth):
        x0, vv = _hy_proj(hb.reshape(B, L, D), w_in_bf, hy_conv_w, conv_b3, i, dh)
        fw1p = jnp.pad(hy_fw1[i], ((0, V7X_LANES - hy_fw1.shape[1]), (0, 0)))
        hs, hdf = _filters(z, fw1p, hy_fb1[i].reshape(1, -1), hy_fw2[i], hy_fb2[i].reshape(1, -1),
                           hy_freq[i], hy_fw3[i], deltas, dh)
        kre, kim = _spectrum(c0h, c0l, s0h, s0l, hs, hdf)
        y_hy = _longconv(cs_bf, ss_bf, vv, x0, kre, kim, hy_bias[i].reshape(1, dh))
        gain = jnp.concatenate([jnp.tile(q_norm_g[i], n_q), jnp.tile(k_norm_g[i], n_kv)]).reshape(1, n_rot)
        w_qkv = lax.slice_in_dim(w_in_bf[i], qkv0, gate0, axis=1)
        q, k, v = _qkv_proj(hb, w_qkv, bd, gain, qk_scale, cos_t, sin_t, B, L, n_q, n_kv, hd)
        y_att = _attention(q, k, v)
        merged = _merge(hb, y_hy.reshape(T, dh), y_att.reshape(T, d_attn), w_in_bf, w_hy_bf, w_att_bf,
                        i, gate0)
        hf, hb = _oproj_ln(merged, w_o_bf, hf, ln_mix_g3, ln_mix_b3, i, alpha)
        j = i // 2
        if i % 2 == 0:
            hf, hb = _ffn_dense(hb, ffn_g_bf, ffn_u_bf, ffn_d_bf, hf, ln_ffn_g3, ln_ffn_b3, j, i, alpha)
        else:
            ya, yb, wts = _moe(hf, hb, router_w[j], router_b[j], exp_w_gate, exp_w_up, exp_w_down, j)
            hf, hb = _moe_combine_ln(hf, ya, yb, wts, ln_ffn_g[i], ln_ffn_b[i], alpha)
    return hf.reshape(B, L, D)
```

```python
import functools
import math

import jax
import jax.numpy as jnp
from jax import lax
from jax.experimental import pallas as pl
from jax.experimental.pallas import tpu as pltpu

F32 = jnp.float32
BF16 = jnp.bfloat16

GRID_W = 64
ROPE_THETA = 10000.0
TOP_K = 2
DECAY_TARGET = 1e-2
FAST_DECAY_PCT = 0.3
SLOW_DECAY_PCT = 1.5
LN_EPS = 1e-5
RMS_EPS = 1e-6

V7X_VMEM_BYTES = 64 * 1024 * 1024
V7X_LANES = 128
V7X_MXU_DIM = 256
VMEM_BUDGET = V7X_VMEM_BYTES - 8 * 1024 * 1024
ATTN_SUBTILE = 512


def _params(semantics, vmem_bytes):
    return pltpu.CompilerParams(dimension_semantics=semantics,
                                vmem_limit_bytes=min(int(vmem_bytes), VMEM_BUDGET))


def _dot(a, b):
    return jnp.dot(a, b, preferred_element_type=F32)


def _split_bf16(a):
    hi = a.astype(BF16)
    lo = (a - hi.astype(F32)).astype(BF16)
    return hi, lo


def _dot3(a_hi, a_lo, b_hi, b_lo):
    return _dot(a_hi, b_hi) + (_dot(a_hi, b_lo) + _dot(a_lo, b_hi))


def _layer_norm_rows(x, g, b):
    mu = jnp.mean(x, axis=-1, keepdims=True)
    xc = x - mu
    var = jnp.mean(xc * xc, axis=-1, keepdims=True)
    return xc * lax.rsqrt(var + LN_EPS) * g + b


def _resid_ln_kernel(alpha, n_add, *refs):
    h_ref = refs[0]
    add_refs = refs[1:1 + n_add]
    g_ref, b_ref, of_ref, ob_ref = refs[1 + n_add:]
    x = h_ref[...]
    if n_add:
        acc = add_refs[0][...]
        for r in add_refs[1:]:
            acc = acc + r[...]
        x = alpha * x + acc
    y = _layer_norm_rows(x, g_ref[...], b_ref[...])
    of_ref[...] = y
    ob_ref[...] = y.astype(BF16)


def _resid_ln(h, addends, g, b, alpha, tm=256):
    T, D = h.shape
    n_add = len(addends)
    row = pl.BlockSpec((tm, D), lambda i: (i, 0))
    vec = pl.BlockSpec((1, D), lambda i: (0, 0))
    return pl.pallas_call(
        functools.partial(_resid_ln_kernel, alpha, n_add),
        grid=(T // tm,),
        in_specs=[row] * (1 + n_add) + [vec, vec],
        out_specs=[row, row],
        out_shape=[jax.ShapeDtypeStruct((T, D), F32), jax.ShapeDtypeStruct((T, D), BF16)],
        compiler_params=_params(("parallel",), (2 * (1 + n_add) * 4 + 12 + 16) * tm * D),
        name="resid_ln",
    )(h, *addends, g.reshape(1, D), b.reshape(1, D))


def _hy_proj_kernel(h_ref, w0_ref, w1_ref, w2_ref, cw0_ref, cw1_ref, cw2_ref,
                    cb0_ref, cb1_ref, cb2_ref, x0_ref, vv_ref):
    h = h_ref[...]
    L = h.shape[0]
    row = lax.broadcasted_iota(jnp.int32, (L, 1), 0)

    def stream(w_ref, cw_ref, cb_ref):
        p = _dot(h, w_ref[...])
        prev = jnp.where(row == 0, 0.0, pltpu.roll(p, 1, 0))
        nxt = jnp.where(row == L - 1, 0.0, pltpu.roll(p, L - 1, 0))
        cw = cw_ref[...]
        return prev * cw[0:1] + p * cw[1:2] + nxt * cw[2:3] + cb_ref[...]

    x0_ref[...] = stream(w0_ref, cw0_ref, cb0_ref)
    vv_ref[...] = stream(w1_ref, cw1_ref, cb1_ref) * stream(w2_ref, cw2_ref, cb2_ref)


def _hy_proj(hb3, w_in_bf, conv_w, conv_b3, layer, dh, tn=256):
    B, L, D = hb3.shape
    nj = dh // tn

    def wspec(s):
        return pl.BlockSpec((None, D, tn), lambda j, b: (layer, 0, s * nj + j))

    def cwspec(s):
        return pl.BlockSpec((None, conv_w.shape[1], tn), lambda j, b: (layer, 0, s * nj + j))

    def cbspec(s):
        return pl.BlockSpec((None, 1, tn), lambda j, b: (layer, 0, s * nj + j))

    out = pl.BlockSpec((None, L, tn), lambda j, b: (b, 0, j))
    return pl.pallas_call(
        _hy_proj_kernel,
        grid=(nj, B),
        in_specs=[pl.BlockSpec((None, L, D), lambda j, b: (b, 0, 0)),
                  wspec(0), wspec(1), wspec(2), cwspec(0), cwspec(1), cwspec(2),
                  cbspec(0), cbspec(1), cbspec(2)],
        out_specs=[out, out],
        out_shape=[jax.ShapeDtypeStruct((B, L, dh), F32)] * 2,
        compiler_params=_params(("parallel", "parallel"),
                                4 * L * D + 12 * D * tn + 16 * L * tn + 40 * L * tn),
        name="hyena_proj_conv",
    )(hb3, w_in_bf, w_in_bf, w_in_bf, conv_w, conv_w, conv_w, conv_b3, conv_b3, conv_b3)


def _qkv_kernel(n_q, n_kv, hd, x_ref, w_ref, bd_ref, gain_ref, scale_ref, cos_ref, sin_ref,
                q_ref, k_ref, v_ref):
    n_rot = (n_q + n_kv) * hd
    p = _dot(x_ref[...], w_ref[...])
    pr = p[:, :n_rot]
    sq = (pr * pr).astype(BF16)
    bd = bd_ref[...]
    gw = bd.shape[0]
    ms = jnp.concatenate([_dot(sq[:, c:c + gw], bd) for c in range(0, n_rot, gw)], axis=1)
    xn = pr * lax.rsqrt(ms + RMS_EPS) * gain_ref[...]
    reps = n_rot // V7X_LANES
    cos = jnp.concatenate([cos_ref[...]] * reps, axis=1)
    sin = jnp.concatenate([sin_ref[...]] * reps, axis=1)
    lane = lax.broadcasted_iota(jnp.int32, (1, n_rot), 1)
    first_half = (lane % hd) < (hd // 2)
    half = hd // 2
    swapped = jnp.where(first_half, pltpu.roll(xn, n_rot - half, 1), pltpu.roll(xn, half, 1))
    rot = ((xn * cos + swapped * sin) * scale_ref[...]).astype(BF16)
    for hh in range(n_q):
        q_ref[hh] = rot[:, hh * hd:(hh + 1) * hd]
    pad_lane = lax.broadcasted_iota(jnp.int32, (p.shape[0], V7X_LANES - hd), 1)
    ones_col = jnp.where(pad_lane == 0, 1.0, 0.0).astype(BF16)
    for hh in range(n_kv):
        k_ref[hh] = rot[:, (n_q + hh) * hd:(n_q + hh + 1) * hd]
        v_h = p[:, n_rot + hh * hd:n_rot + (hh + 1) * hd].astype(BF16)
        v_ref[hh] = jnp.concatenate([v_h, ones_col], axis=1)


def _qkv_proj(hb, w_qkv, bd, gain, scale, cos_t, sin_t, B, L, n_q, n_kv, hd, tm=512):
    T, D = hb.shape
    nw = w_qkv.shape[1]
    n_rot = (n_q + n_kv) * hd
    tm = min(tm, L)
    tpb = L // tm
    const = lambda i: (0, 0)
    return pl.pallas_call(
        functools.partial(_qkv_kernel, n_q, n_kv, hd),
        grid=(T // tm,),
        in_specs=[pl.BlockSpec((tm, D), lambda i: (i, 0)),
                  pl.BlockSpec((D, nw), const),
                  pl.BlockSpec(bd.shape, const),
                  pl.BlockSpec((1, n_rot), const),
                  pl.BlockSpec((1, n_rot), const),
                  pl.BlockSpec((tm, V7X_LANES), lambda i: (i % tpb, 0)),
                  pl.BlockSpec((tm, V7X_LANES), lambda i: (i % tpb, 0))],
        out_specs=[pl.BlockSpec((None, n_q, tm, hd), lambda i: (i // tpb, 0, i % tpb, 0)),
                   pl.BlockSpec((None, n_kv, tm, hd), lambda i: (i // tpb, 0, i % tpb, 0)),
                   pl.BlockSpec((None, n_kv, tm, V7X_LANES), lambda i: (i // tpb, 0, i % tpb, 0))],
        out_shape=[jax.ShapeDtypeStruct((B, n_q, L, hd), BF16),
                   jax.ShapeDtypeStruct((B, n_kv, L, hd), BF16),
                   jax.ShapeDtypeStruct((B, n_kv, L, V7X_LANES), BF16)],
        compiler_params=_params(("parallel",),
                                4 * tm * D + 4 * D * nw + 8 * tm * (n_q + 2 * n_kv) * V7X_LANES
                                + 48 * tm * nw),
        name="qkv_proj_norm_rope",
    )(hb, w_qkv, bd, gain, scale, cos_t, sin_t)


def _attn_kernel(group, hd, q_ref, k_ref, v_ref, o_ref):
    k = k_ref[...]
    v = v_ref[...]
    tq = q_ref.shape[1]
    ts = min(tq, ATTN_SUBTILE)
    for r0 in range(0, tq, ts):
        for g in range(group):
            s = lax.dot_general(q_ref[g, r0:r0 + ts, :], k, (((1,), (1,)), ((), ())),
                                preferred_element_type=F32)
            m = jnp.max(s, axis=-1, keepdims=True)
            e = jnp.exp(s - m).astype(BF16)
            o = _dot(e, v)
            o_ref[r0:r0 + ts, g * hd:(g + 1) * hd] = (o[:, :hd] / o[:, hd:hd + 1]).astype(o_ref.dtype)


def _attention(q, k, v, tq=2048):
    B, n_q, L, hd = q.shape
    n_kv = k.shape[1]
    group = n_q // n_kv
    tq = min(tq, L)
    return pl.pallas_call(
        functools.partial(_attn_kernel, group, hd),
        grid=(B, n_kv, L // tq),
        in_specs=[pl.BlockSpec((None, group, tq, hd), lambda b, j, t: (b, j, t, 0)),
                  pl.BlockSpec((None, None, L, hd), lambda b, j, t: (b, j, 0, 0)),
                  pl.BlockSpec((None, None, L, v.shape[-1]), lambda b, j, t: (b, j, 0, 0))],
        out_specs=pl.BlockSpec((None, tq, group * hd), lambda b, j, t: (b, t, j)),
        out_shape=jax.ShapeDtypeStruct((B, L, n_q * hd), BF16),
        compiler_params=_params(("parallel", "parallel", "parallel"),
                                8 * L * V7X_LANES + 8 * group * tq * V7X_LANES + 40 * tq * L),
        name="gqa_attention",
    )(q, k, v)


def _filter_kernel(z_ref, fw1_ref, fb1_ref, fw2_ref, fb2_ref, fr_ref, w3f_ref, w3b_ref, dl_ref,
                   hs_ref, hd_ref):
    z = z_ref[...]
    L = z.shape[0]
    fr = fr_ref[...]

    def hp_dot(a, b):
        a_hi, a_lo = _split_bf16(a)
        b_hi, b_lo = _split_bf16(b)
        return _dot3(a_hi, a_lo, b_hi, b_lo)

    a = jnp.sin(fr[0:1] * (hp_dot(z, fw1_ref[...]) + fb1_ref[...]))
    a = jnp.sin(fr[1:2] * (hp_dot(a, fw2_ref[...]) + fb2_ref[...]))
    window = jnp.exp(-z[:, 0:1] * dl_ref[...])
    h_fwd = hp_dot(a, w3f_ref[...]) * window
    h_bwd = hp_dot(a, w3b_ref[...]) * window
    row = lax.broadcasted_iota(jnp.int32, (L, 1), 0)
    h_bwd0 = jnp.where(row == 0, 0.0, h_bwd)
    hs_ref[...] = h_fwd + h_bwd0
    hd_ref[...] = h_bwd0 - h_fwd


def _filters(z, fw1p, fb1, fw2, fb2, fr, fw3, deltas, dh, tn=256):
    L, zw = z.shape
    nh = fw2.shape[0]
    nj = dh // tn
    const = lambda j: (0, 0)
    out = pl.BlockSpec((L, tn), lambda j: (0, j))
    return pl.pallas_call(
        _filter_kernel,
        grid=(nj,),
        in_specs=[pl.BlockSpec((L, zw), const), pl.BlockSpec((zw, nh), const),
                  pl.BlockSpec((1, nh), const), pl.BlockSpec((nh, nh), const),
                  pl.BlockSpec((1, nh), const), pl.BlockSpec((2, nh), const),
                  pl.BlockSpec((nh, tn), lambda j: (0, j)),
                  pl.BlockSpec((nh, tn), lambda j: (0, nj + j)),
                  pl.BlockSpec((1, tn), lambda j: (0, j))],
        out_specs=[out, out],
        out_shape=[jax.ShapeDtypeStruct((L, dh), F32)] * 2,
        compiler_params=_params(("parallel",), 32 * 1024 * 1024),
        name="hyena_filters",
    )(z, fw1p, fb1, fw2, fb2, fr, fw3, fw3, deltas)


def _spectrum_kernel(ch_ref, cl_ref, sh_ref, sl_ref, hs_ref, hd_ref, kre_ref, kim_ref):
    hs_hi, hs_lo = _split_bf16(hs_ref[...])
    hd_hi, hd_lo = _split_bf16(hd_ref[...])
    kre_ref[...] = _dot3(ch_ref[...], cl_ref[...], hs_hi, hs_lo)
    kim_ref[...] = _dot3(sh_ref[...], sl_ref[...], hd_hi, hd_lo)


def _spectrum(c0h, c0l, s0h, s0l, hs, hd, tk=512, tn=256):
    L, dh = hs.shape
    tk = min(tk, L)
    mat = pl.BlockSpec((tk, L), lambda j, i: (i, 0))
    col = pl.BlockSpec((L, tn), lambda j, i: (0, j))
    out = pl.BlockSpec((tk, tn), lambda j, i: (i, j))
    return pl.pallas_call(
        _spectrum_kernel,
        grid=(dh // tn, L // tk),
        in_specs=[mat, mat, mat, mat, col, col],
        out_specs=[out, out],
        out_shape=[jax.ShapeDtypeStruct((L, dh), F32)] * 2,
        compiler_params=_params(("parallel", "parallel"), 40 * 1024 * 1024),
        name="filter_spectrum",
    )(c0h, c0l, s0h, s0l, hs, hd)


def _longconv_kernel(inv_scale, cs_ref, ss_ref, v_ref, x0_ref, kre_ref, kim_ref, bias_ref, y_ref):
    cs = cs_ref[...]
    ss = ss_ref[...]
    v = v_ref[...]
    vb = v.astype(BF16)
    xc = _dot(cs, vb)
    xs = _dot(ss, vb)
    kre = kre_ref[...]
    kim = kim_ref[...]
    yre = (xc * kre + xs * kim).astype(BF16)
    yim = (xc * kim - xs * kre).astype(BF16)
    conv = (_dot(cs, yre) - _dot(ss, yim)) * inv_scale
    y_ref[...] = (x0_ref[...] * (conv + bias_ref[...] * v)).astype(y_ref.dtype)


def _longconv(cs, ss, vv, x0, kre, kim, bias, tc=256):
    B, L, dh = vv.shape
    mat = pl.BlockSpec((L, L), lambda j, b: (0, 0), pipeline_mode=pl.Buffered(1))
    act = pl.BlockSpec((None, L, tc), lambda j, b: (b, 0, j))
    spec = pl.BlockSpec((L, tc), lambda j, b: (0, j))
    return pl.pallas_call(
        functools.partial(_longconv_kernel, 1.0 / L),
        grid=(dh // tc, B),
        in_specs=[mat, mat, act, act, spec, spec, pl.BlockSpec((1, tc), lambda j, b: (0, j))],
        out_specs=act,
        out_shape=jax.ShapeDtypeStruct((B, L, dh), BF16),
        compiler_params=_params(("parallel", "parallel"),
                                4 * L * L + 16 * L * tc + 16 * L * tc + 4 * L * tc + 40 * L * tc),
        name="hyena_longconv",
    )(cs, ss, vv, x0, kre, kim, bias)


def _merge_kernel(h_ref, yh_ref, ya_ref, wgh_ref, wga_ref, wh_ref, wa_ref, o_ref):
    h = h_ref[...]
    gh = jax.nn.sigmoid(_dot(h, wgh_ref[...]))
    ga = jax.nn.sigmoid(_dot(h, wga_ref[...]))
    o_ref[...] = (gh * _dot(yh_ref[...], wh_ref[...])
                  + ga * _dot(ya_ref[...], wa_ref[...])).astype(o_ref.dtype)


def _merge(hb, y_hy, y_att, w_in_bf, w_hy, w_att, layer, gate_col0, tm=1024, tn=256):
    T, D = hb.shape
    dh, da = y_hy.shape[1], y_att.shape[1]
    tm = min(tm, T)
    g0 = gate_col0 // tn
    nj = D // tn
    return pl.pallas_call(
        _merge_kernel,
        grid=(T // tm, nj),
        in_specs=[pl.BlockSpec((tm, D), lambda i, j: (i, 0)),
                  pl.BlockSpec((tm, dh), lambda i, j: (i, 0)),
                  pl.BlockSpec((tm, da), lambda i, j: (i, 0)),
                  pl.BlockSpec((None, D, tn), lambda i, j: (layer, 0, g0 + j)),
                  pl.BlockSpec((None, D, tn), lambda i, j: (layer, 0, g0 + nj + j)),
                  pl.BlockSpec((None, dh, tn), lambda i, j: (layer, 0, j)),
                  pl.BlockSpec((None, da, tn), lambda i, j: (layer, 0, j))],
        out_specs=pl.BlockSpec((tm, tn), lambda i, j: (i, j)),
        out_shape=jax.ShapeDtypeStruct((T, D), BF16),
        compiler_params=_params(("parallel", "parallel"),
                                4 * tm * (D + dh + da) + 4 * tn * (2 * D + dh + da) + 4 * tm * tn
                                + 32 * tm * tn),
        name="gated_merge",
    )(hb, y_hy, y_att, w_in_bf, w_in_bf, w_hy, w_att)


def _oproj_ln_kernel(alpha, x_ref, w_ref, h_ref, g_ref, b_ref, of_ref, ob_ref):
    y = alpha * h_ref[...] + _dot(x_ref[...], w_ref[...])
    y = _layer_norm_rows(y, g_ref[...], b_ref[...])
    of_ref[...] = y
    ob_ref[...] = y.astype(BF16)


def _oproj_ln(xb, w_o, h, g, b, layer, alpha, tm=256):
    T, D = h.shape
    row = lambda i: (i, 0)
    vec = pl.BlockSpec((None, 1, D), lambda i: (layer, 0, 0))
    return pl.pallas_call(
        functools.partial(_oproj_ln_kernel, alpha),
        grid=(T // tm,),
        in_specs=[pl.BlockSpec((tm, D), row),
                  pl.BlockSpec((None, D, D), lambda i: (layer, 0, 0)),
                  pl.BlockSpec((tm, D), row), vec, vec],
        out_specs=[pl.BlockSpec((tm, D), row), pl.BlockSpec((tm, D), row)],
        out_shape=[jax.ShapeDtypeStruct((T, D), F32), jax.ShapeDtypeStruct((T, D), BF16)],
        compiler_params=_params(("parallel",), 4 * D * D + 24 * tm * D + 24 * tm * D),
        name="oproj_resid_ln",
    )(xb, w_o, h, g, b)


def _swiglu_partial(x_ref, wg_ref, wu_ref, wd_ref):
    x = x_ref[...]
    g = _dot(x, wg_ref[...].astype(BF16))
    u = _dot(x, wu_ref[...].astype(BF16))
    a = (g * jax.nn.sigmoid(g) * u).astype(BF16)
    return _dot(a, wd_ref[...].astype(BF16))


def _ffn_dense_kernel(alpha, x_ref, wg_ref, wu_ref, wd_ref, h_ref, g_ref, b_ref,
                      of_ref, ob_ref, acc_ref):
    f = pl.program_id(1)

    @pl.when(f == 0)
    def _():
        acc_ref[...] = jnp.zeros_like(acc_ref)

    acc_ref[...] += _swiglu_partial(x_ref, wg_ref, wu_ref, wd_ref)

    @pl.when(f == pl.num_programs(1) - 1)
    def _():
        y = _layer_norm_rows(alpha * h_ref[...] + acc_ref[...], g_ref[...], b_ref[...])
        of_ref[...] = y
        ob_ref[...] = y.astype(BF16)


def _ffn_dense(xb, wg, wu, wd, h, g, b, j, layer, alpha, tm=512, tf=512):
    T, D = xb.shape
    F = wg.shape[-1]
    row = lambda i, f: (i, 0)
    vec = pl.BlockSpec((None, 1, D), lambda i, f: (layer, 0, 0))
    return pl.pallas_call(
        functools.partial(_ffn_dense_kernel, alpha),
        grid=(T // tm, F // tf),
        in_specs=[pl.BlockSpec((tm, D), row),
                  pl.BlockSpec((None, D, tf), lambda i, f: (j, 0, f)),
                  pl.BlockSpec((None, D, tf), lambda i, f: (j, 0, f)),
                  pl.BlockSpec((None, tf, D), lambda i, f: (j, f, 0)),
                  pl.BlockSpec((tm, D), row), vec, vec],
        out_specs=[pl.BlockSpec((tm, D), row), pl.BlockSpec((tm, D), row)],
        out_shape=[jax.ShapeDtypeStruct((T, D), F32), jax.ShapeDtypeStruct((T, D), BF16)],
        scratch_shapes=[pltpu.VMEM((tm, D), F32)],
        compiler_params=_params(("parallel", "arbitrary"),
                                4 * tm * D + 12 * D * tf + 8 * tm * D + 12 * tm * D + 4 * tm * D
                                + 16 * tm * tf + 8 * tm * D),
        name="ffn_dense_swiglu_ln",
    )(xb, wg, wu, wd, h, g, b)


def _ffn_group_kernel(be_ref, nv_ref, x_ref, wg_ref, wu_ref, wd_ref, o_ref):
    i = pl.program_id(0)
    f = pl.program_id(1)

    @pl.when(i < nv_ref[0])
    def _():
        @pl.when(f == 0)
        def _():
            o_ref[...] = jnp.zeros_like(o_ref)

        o_ref[...] += _swiglu_partial(x_ref, wg_ref, wu_ref, wd_ref)


def _ffn_grouped(blk_e, n_valid, xg, wg, wu, wd, j, tm, tf=256):
    P, D = xg.shape
    F = wg.shape[-1]
    nf = F // tf

    def rows(i, f, be, nv):
        return (jnp.minimum(i, nv[0] - 1), 0)

    def hidden(i, f, nv):
        return jnp.where(i < nv[0], f, nf - 1)

    return pl.pallas_call(
        _ffn_group_kernel,
        grid_spec=pltpu.PrefetchScalarGridSpec(
            num_scalar_prefetch=2,
            grid=(P // tm, nf),
            in_specs=[pl.BlockSpec((tm, D), rows),
                      pl.BlockSpec((None, None, D, tf),
                                   lambda i, f, be, nv: (j, be[i], 0, hidden(i, f, nv))),
                      pl.BlockSpec((None, None, D, tf),
                                   lambda i, f, be, nv: (j, be[i], 0, hidden(i, f, nv))),
                      pl.BlockSpec((None, None, tf, D),
                                   lambda i, f, be, nv: (j, be[i], hidden(i, f, nv), 0))],
            out_specs=pl.BlockSpec((tm, D), rows)),
        out_shape=jax.ShapeDtypeStruct((P, D), F32),
        compiler_params=_params(("arbitrary", "arbitrary"),
                                4 * tm * D + 24 * D * tf + 8 * tm * D
                                + 6 * D * tf + 16 * tm * tf + 8 * tm * D),
        name="moe_grouped_swiglu",
    )(blk_e, n_valid, xg, wg, wu, wd)


def _router_kernel(x_ref, wh_ref, wl_ref, b_ref, tri_ref, e_ref, w_ref, r_ref, cnt_ref):
    @pl.when(pl.program_id(0) == 0)
    def _():
        cnt_ref[...] = jnp.zeros_like(cnt_ref)

    x_hi, x_lo = _split_bf16(x_ref[...])
    nt = (((1,), (1,)), ((), ()))

    def dg(a, b):
        return lax.dot_general(a, b, nt, preferred_element_type=F32)

    wh = wh_ref[...]
    wl = wl_ref[...]
    logits = dg(wh, x_hi) + (dg(wh, x_lo) + dg(wl, x_hi)) + b_ref[...]
    n_e = logits.shape[0]
    eid = lax.broadcasted_iota(jnp.int32, logits.shape, 0)
    m1 = jnp.max(logits, axis=0, keepdims=True)
    i1 = jnp.min(jnp.where(logits == m1, eid, n_e), axis=0, keepdims=True)
    rest = jnp.where(eid == i1, -jnp.inf, logits)
    m2 = jnp.max(rest, axis=0, keepdims=True)
    i2 = jnp.min(jnp.where(rest == m2, eid, n_e), axis=0, keepdims=True)
    t = jnp.exp(m2 - m1)
    w1 = 1.0 / (1.0 + t)
    e_ref[...] = jnp.concatenate([i1, i2], axis=0)
    w_ref[...] = jnp.concatenate([w1, t * w1], axis=0)

    tri = tri_ref[...]
    oh1 = (eid == i1).astype(F32)
    oh2 = (eid == i2).astype(F32)
    tot1 = jnp.sum(oh1, axis=1, keepdims=True)
    tot2 = jnp.sum(oh2, axis=1, keepdims=True)
    base = cnt_ref[...][:, 0:1]
    c1 = base + _dot(oh1.astype(BF16), tri)
    c2 = base + tot1 + _dot(oh2.astype(BF16), tri)
    r1 = jnp.sum(oh1 * c1, axis=0, keepdims=True)
    r2 = jnp.sum(oh2 * c2, axis=0, keepdims=True)
    r_ref[...] = jnp.concatenate([r1, r2], axis=0).astype(jnp.int32)
    cnt_ref[...] = cnt_ref[...] + (tot1 + tot2)


def _router(h, rw_hi, rw_lo, rb, tm=512):
    T, D = h.shape
    E = rw_hi.shape[0]
    const = lambda i: (0, 0)
    out = pl.BlockSpec((TOP_K, tm), lambda i: (0, i))
    idx = jnp.arange(tm)
    tri = (idx[:, None] < idx[None, :]).astype(BF16)
    return pl.pallas_call(
        _router_kernel,
        grid=(T // tm,),
        in_specs=[pl.BlockSpec((tm, D), lambda i: (i, 0)), pl.BlockSpec((E, D), const),
                  pl.BlockSpec((E, D), const), pl.BlockSpec((E, 1), const),
                  pl.BlockSpec((tm, tm), const)],
        out_specs=[out, out, out, pl.BlockSpec((E, V7X_LANES), const)],
        out_shape=[jax.ShapeDtypeStruct((TOP_K, T), jnp.int32),
                   jax.ShapeDtypeStruct((TOP_K, T), F32),
                   jax.ShapeDtypeStruct((TOP_K, T), jnp.int32),
                   jax.ShapeDtypeStruct((E, V7X_LANES), F32)],
        compiler_params=_params(("arbitrary",), 32 * 1024 * 1024),
        name="moe_router_top2",
    )(h, rw_hi, rw_lo, rb, tri)


def _moe_combine_ln_kernel(alpha, h_ref, ya_ref, yb_ref, w_ref, g_ref, b_ref, of_ref, ob_ref):
    w = w_ref[...]
    f = ya_ref[...] * w[:, 0:1] + yb_ref[...] * w[:, 1:2]
    y = _layer_norm_rows(alpha * h_ref[...] + f, g_ref[...], b_ref[...])
    of_ref[...] = y
    ob_ref[...] = y.astype(BF16)


def _moe_combine_ln(h, ya, yb, w, g, b, alpha, tm=256):
    T, D = h.shape
    row = pl.BlockSpec((tm, D), lambda i: (i, 0))
    vec = pl.BlockSpec((1, D), lambda i: (0, 0))
    return pl.pallas_call(
        functools.partial(_moe_combine_ln_kernel, alpha),
        grid=(T // tm,),
        in_specs=[row, row, row, pl.BlockSpec((tm, w.shape[1]), lambda i: (i, 0)), vec, vec],
        out_specs=[row, row],
        out_shape=[jax.ShapeDtypeStruct((T, D), F32), jax.ShapeDtypeStruct((T, D), BF16)],
        compiler_params=_params(("parallel",), 64 * tm * D),
        name="moe_combine_ln",
    )(h, ya, yb, w, g.reshape(1, D), b.reshape(1, D))


def _dft_tables(L):
    n2 = 2 * L
    k = jnp.arange(L, dtype=jnp.int32)[:, None]
    n = jnp.arange(L, dtype=jnp.int32)[None, :]
    ang = (((2 * k + 1) * (2 * n + 1)) % (4 * n2)).astype(F32) * (math.pi / (2 * n2))
    ang0 = (((2 * k + 1) * n) % (2 * n2)).astype(F32) * (math.pi / n2)
    return jnp.cos(ang), jnp.sin(ang), jnp.cos(ang0), jnp.sin(ang0)


def _rope_tables(L, hd):
    rows = L // GRID_W
    row = jnp.repeat(jnp.arange(rows, dtype=F32), GRID_W)
    col = jnp.tile(jnp.arange(GRID_W, dtype=F32), rows)
    n_pairs = hd // 4
    inv = ROPE_THETA ** (-jnp.arange(n_pairs, dtype=F32) / n_pairs)
    ang = jnp.concatenate([row[:, None] * inv, col[:, None] * inv], axis=-1)
    cos, sin = jnp.cos(ang), jnp.sin(ang)
    reps = V7X_LANES // hd
    return (jnp.tile(jnp.concatenate([cos, cos], axis=-1), (1, reps)),
            jnp.tile(jnp.concatenate([-sin, sin], axis=-1), (1, reps)))


def _filter_features(L, n_emb):
    t = jnp.linspace(0.0, 1.0, L, dtype=F32)[:, None]
    bands = (n_emb - 1) // 2
    w = 2.0 * math.pi * jnp.arange(L, dtype=F32)[:, None] / L
    f = jnp.linspace(1e-4, bands - 1, bands, dtype=F32)[None, :]
    z = jnp.concatenate([t, jnp.cos(f * w), -jnp.sin(f * w)], axis=-1)
    return jnp.pad(z, ((0, 0), (0, V7X_LANES - n_emb)))


def _moe_block_rows(T):
    return min(1024, max(V7X_MXU_DIM, T // 4))


def _moe(hf, hb, router_w, router_b, wg, wu, wd, j):
    T, D = hf.shape
    E = router_w.shape[1]
    rw_hi, rw_lo = _split_bf16(router_w.T)
    top_e, top_w, rank, cnt = _router(hf, rw_hi, rw_lo, router_b.reshape(E, 1))

    tm = _moe_block_rows(T)
    counts = cnt[:, 0].astype(jnp.int32)
    padded = (counts + tm - 1) // tm * tm
    pad_end = jnp.cumsum(padded)
    pad_start = pad_end - padded
    nblk = -(-(T * TOP_K + E * (tm - 1)) // tm)
    P = nblk * tm
    n_valid = (pad_end[-1] // tm).astype(jnp.int32)
    blk_first = jnp.arange(nblk, dtype=jnp.int32) * tm
    blk_e = jnp.sum((blk_first[:, None] >= pad_end[None, :]).astype(jnp.int32), axis=1)
    blk_e = jnp.minimum(blk_e, E - 1)
    blk_e = jnp.where(jnp.arange(nblk) < n_valid, blk_e, blk_e[jnp.maximum(n_valid - 1, 0)])
    start_of = jnp.sum(jnp.where(top_e[:, :, None] == jnp.arange(E, dtype=jnp.int32),
                                 pad_start[None, None, :], 0), axis=-1)
    dest = start_of + rank

    tok = jnp.broadcast_to(jnp.arange(T, dtype=jnp.int32), (TOP_K, T))
    src_tok = jnp.zeros((P,), jnp.int32).at[dest.reshape(-1)].set(tok.reshape(-1))
    yg = _ffn_grouped(blk_e, n_valid.reshape(1), hb[src_tok], wg, wu, wd, j, tm)
    return yg[dest[0]], yg[dest[1]], top_w.T


def kernel(x, ln_in_g, ln_in_b, w_in, hy_conv_w, hy_conv_b, hy_fw1, hy_fb1, hy_fw2, hy_fb2, hy_fw3, hy_freq, hy_bias, q_norm_g, k_norm_g, w_hy_br, w_att_br, w_o, ln_mix_g, ln_mix_b, ffn_w_gate, ffn_w_up, ffn_w_down, router_w, router_b, exp_w_gate, exp_w_up, exp_w_down, ln_ffn_g, ln_ffn_b):
    B, L, D = x.shape
    T = B * L
    depth = w_in.shape[0]
    dh = w_hy_br.shape[1]
    d_attn = w_att_br.shape[1]
    hd = q_norm_g.shape[-1]
    n_q = d_attn // hd
    d_kv = (w_in.shape[2] - 3 * dh - d_attn - 2 * D) // 2
    n_kv = d_kv // hd
    alpha = (2 * depth) ** 0.25
    qkv0 = 3 * dh
    gate0 = qkv0 + d_attn + 2 * d_kv
    n_rot = (n_q + n_kv) * hd
    assert L % GRID_W == 0 and V7X_LANES % hd == 0 and n_rot % V7X_MXU_DIM == 0

    cs, ss, c0, s0 = _dft_tables(L)
    cs_bf, ss_bf = cs.astype(BF16), ss.astype(BF16)
    c0h, c0l = _split_bf16(c0)
    s0h, s0l = _split_bf16(s0)
    cos_t, sin_t = _rope_tables(L, hd)
    z = _filter_features(L, hy_fw1.shape[1])
    min_decay = math.log(DECAY_TARGET) / SLOW_DECAY_PCT
    max_decay = math.log(DECAY_TARGET) / FAST_DECAY_PCT
    deltas = jnp.abs(jnp.linspace(min_decay, max_decay, dh, dtype=F32)).reshape(1, dh)
    lane = jnp.arange(V7X_MXU_DIM)
    bd = jnp.where((lane[:, None] // hd) == (lane[None, :] // hd), 1.0 / hd, 0.0).astype(BF16)
    qk_scale = jnp.concatenate([jnp.full((n_q * hd,), hd ** -0.5, F32),
                                jnp.ones((n_kv * hd,), F32)]).reshape(1, n_rot)

    w_in_bf = w_in.astype(BF16)
    w_hy_bf = w_hy_br.astype(BF16)
    w_att_bf = w_att_br.astype(BF16)
    w_o_bf = w_o.astype(BF16)
    ffn_g_bf = ffn_w_gate.astype(BF16)
    ffn_u_bf = ffn_w_up.astype(BF16)
    ffn_d_bf = ffn_w_down.astype(BF16)
    conv_b3 = hy_conv_b.reshape(depth, 1, 3 * dh)
    ln_mix_g3, ln_mix_b3 = ln_mix_g.reshape(depth, 1, D), ln_mix_b.reshape(depth, 1, D)
    ln_ffn_g3, ln_ffn_b3 = ln_ffn_g.reshape(depth, 1, D), ln_ffn_b.reshape(depth, 1, D)

    hf, hb = _resid_ln(x.reshape(T, D), [], ln_in_g, ln_in_b, alpha)
    for i in range(depth):
        x0, vv = _hy_proj(hb.reshape(B, L, D), w_in_bf, hy_conv_w, conv_b3, i, dh)
        fw1p = jnp.pad(hy_fw1[i], ((0, V7X_LANES - hy_fw1.shape[1]), (0, 0)))
        hs, hdf = _filters(z, fw1p, hy_fb1[i].reshape(1, -1), hy_fw2[i], hy_fb2[i].reshape(1, -1),
                           hy_freq[i], hy_fw3[i], deltas, dh)
        kre, kim = _spectrum(c0h, c0l, s0h, s0l, hs, hdf)
        y_hy = _longconv(cs_bf, ss_bf, vv, x0, kre, kim, hy_bias[i].reshape(1, dh))
        gain = jnp.concatenate([jnp.tile(q_norm_g[i], n_q), jnp.tile(k_norm_g[i], n_kv)]).reshape(1, n_rot)
        w_qkv = lax.slice_in_dim(w_in_bf[i], qkv0, gate0, axis=1)
        q, k, v = _qkv_proj(hb, w_qkv, bd, gain, qk_scale, cos_t, sin_t, B, L, n_q, n_kv, hd)
        y_att = _attention(q, k, v)
        merged = _merge(hb, y_hy.reshape(T, dh), y_att.reshape(T, d_attn), w_in_bf, w_hy_bf, w_att_bf,
                        i, gate0)
        hf, hb = _oproj_ln(merged, w_o_bf, hf, ln_mix_g3, ln_mix_b3, i, alpha)
        j = i // 2
        if i % 2 == 0:
            hf, hb = _ffn_dense(hb, ffn_g_bf, ffn_u_bf, ffn_d_bf, hf, ln_ffn_g3, ln_ffn_b3, j, i, alpha)
        else:
            ya, yb, wts = _moe(hf, hb, router_w[j], router_b[j], exp_w_gate, exp_w_up, exp_w_down, j)
            hf, hb = _moe_combine_ln(hf, ya, yb, wts, ln_ffn_g[i], ln_ffn_b[i], alpha)
    return hf.reshape(B, L, D)
```

```python
import functools
import math

import jax
import jax.numpy as jnp
from jax import lax
from jax.experimental import pallas as pl
from jax.experimental.pallas import tpu as pltpu

F32 = jnp.float32
BF16 = jnp.bfloat16

GRID_W = 64
ROPE_THETA = 10000.0
TOP_K = 2
DECAY_TARGET = 1e-2
FAST_DECAY_PCT = 0.3
SLOW_DECAY_PCT = 1.5
LN_EPS = 1e-5
RMS_EPS = 1e-6

V7X_VMEM_BYTES = 64 * 1024 * 1024
V7X_LANES = 128
V7X_MXU_DIM = 256
VMEM_BUDGET = V7X_VMEM_BYTES - 8 * 1024 * 1024
ATTN_SUBTILE = 512
EPILOGUE_SUBTILE = V7X_MXU_DIM
MOE_CALLS = 4


def _params(semantics, vmem_bytes):
    return pltpu.CompilerParams(dimension_semantics=semantics,
                                vmem_limit_bytes=min(int(vmem_bytes), VMEM_BUDGET))


def _dot(a, b):
    return jnp.dot(a, b, preferred_element_type=F32)


def _split_bf16(a):
    hi = a.astype(BF16)
    lo = (a - hi.astype(F32)).astype(BF16)
    return hi, lo


def _dot3(a_hi, a_lo, b_hi, b_lo):
    return _dot(a_hi, b_hi) + (_dot(a_hi, b_lo) + _dot(a_lo, b_hi))


def _layer_norm_rows(x, g, b):
    mu = jnp.mean(x, axis=-1, keepdims=True)
    xc = x - mu
    var = jnp.mean(xc * xc, axis=-1, keepdims=True)
    return xc * lax.rsqrt(var + LN_EPS) * g + b


def _resid_ln_kernel(alpha, n_add, *refs):
    h_ref = refs[0]
    add_refs = refs[1:1 + n_add]
    g_ref, b_ref, of_ref, ob_ref = refs[1 + n_add:]
    x = h_ref[...]
    if n_add:
        acc = add_refs[0][...]
        for r in add_refs[1:]:
            acc = acc + r[...]
        x = alpha * x + acc
    y = _layer_norm_rows(x, g_ref[...], b_ref[...])
    of_ref[...] = y
    ob_ref[...] = y.astype(BF16)


def _resid_ln(h, addends, g, b, alpha, tm=256):
    T, D = h.shape
    n_add = len(addends)
    row = pl.BlockSpec((tm, D), lambda i: (i, 0))
    vec = pl.BlockSpec((1, D), lambda i: (0, 0))
    return pl.pallas_call(
        functools.partial(_resid_ln_kernel, alpha, n_add),
        grid=(T // tm,),
        in_specs=[row] * (1 + n_add) + [vec, vec],
        out_specs=[row, row],
        out_shape=[jax.ShapeDtypeStruct((T, D), F32), jax.ShapeDtypeStruct((T, D), BF16)],
        compiler_params=_params(("parallel",), (2 * (1 + n_add) * 4 + 12 + 16) * tm * D),
        name="resid_ln",
    )(h, *addends, g.reshape(1, D), b.reshape(1, D))


def _hy_proj_kernel(h_ref, w0_ref, w1_ref, w2_ref, cw0_ref, cw1_ref, cw2_ref,
                    cb0_ref, cb1_ref, cb2_ref, x0_ref, vv_ref):
    h = h_ref[...]
    L = h.shape[0]
    row = lax.broadcasted_iota(jnp.int32, (L, 1), 0)

    def stream(w_ref, cw_ref, cb_ref):
        p = _dot(h, w_ref[...])
        prev = jnp.where(row == 0, 0.0, pltpu.roll(p, 1, 0))
        nxt = jnp.where(row == L - 1, 0.0, pltpu.roll(p, L - 1, 0))
        cw = cw_ref[...]
        return prev * cw[0:1] + p * cw[1:2] + nxt * cw[2:3] + cb_ref[...]

    x0_ref[...] = stream(w0_ref, cw0_ref, cb0_ref)
    vv_ref[...] = stream(w1_ref, cw1_ref, cb1_ref) * stream(w2_ref, cw2_ref, cb2_ref)


def _hy_proj(hb3, w_in_bf, conv_w, conv_b3, layer, dh, tn=256):
    B, L, D = hb3.shape
    nj = dh // tn

    def wspec(s):
        return pl.BlockSpec((None, D, tn), lambda j, b: (layer, 0, s * nj + j))

    def cwspec(s):
        return pl.BlockSpec((None, conv_w.shape[1], tn), lambda j, b: (layer, 0, s * nj + j))

    def cbspec(s):
        return pl.BlockSpec((None, 1, tn), lambda j, b: (layer, 0, s * nj + j))

    out = pl.BlockSpec((None, L, tn), lambda j, b: (b, 0, j))
    return pl.pallas_call(
        _hy_proj_kernel,
        grid=(nj, B),
        in_specs=[pl.BlockSpec((None, L, D), lambda j, b: (b, 0, 0)),
                  wspec(0), wspec(1), wspec(2), cwspec(0), cwspec(1), cwspec(2),
                  cbspec(0), cbspec(1), cbspec(2)],
        out_specs=[out, out],
        out_shape=[jax.ShapeDtypeStruct((B, L, dh), F32)] * 2,
        compiler_params=_params(("parallel", "parallel"),
                                4 * L * D + 12 * D * tn + 16 * L * tn + 40 * L * tn),
        name="hyena_proj_conv",
    )(hb3, w_in_bf, w_in_bf, w_in_bf, conv_w, conv_w, conv_w, conv_b3, conv_b3, conv_b3)


def _qkv_kernel(n_q, n_kv, hd, x_ref, w_ref, bd_ref, gain_ref, scale_ref, cos_ref, sin_ref,
                q_ref, k_ref, v_ref):
    n_rot = (n_q + n_kv) * hd
    tm = x_ref.shape[0]
    ts = min(tm, EPILOGUE_SUBTILE)
    w = w_ref[...]
    bd = bd_ref[...]
    gw = bd.shape[0]
    reps = n_rot // V7X_LANES
    lane = lax.broadcasted_iota(jnp.int32, (1, n_rot), 1)
    first_half = (lane % hd) < (hd // 2)
    half = hd // 2
    pad_lane = lax.broadcasted_iota(jnp.int32, (ts, V7X_LANES - hd), 1)
    ones_col = jnp.where(pad_lane == 0, 1.0, 0.0).astype(BF16)
    for r0 in range(0, tm, ts):
        rows = slice(r0, r0 + ts)
        p = _dot(x_ref[rows, :], w)
        pr = p[:, :n_rot]
        sq = (pr * pr).astype(BF16)
        ms = jnp.concatenate([_dot(sq[:, c:c + gw], bd) for c in range(0, n_rot, gw)], axis=1)
        xn = pr * lax.rsqrt(ms + RMS_EPS) * gain_ref[...]
        cos = jnp.concatenate([cos_ref[rows, :]] * reps, axis=1)
        sin = jnp.concatenate([sin_ref[rows, :]] * reps, axis=1)
        swapped = jnp.where(first_half, pltpu.roll(xn, n_rot - half, 1), pltpu.roll(xn, half, 1))
        rot = ((xn * cos + swapped * sin) * scale_ref[...]).astype(BF16)
        for hh in range(n_q):
            q_ref[hh, rows, :] = rot[:, hh * hd:(hh + 1) * hd]
        for hh in range(n_kv):
            k_ref[hh, rows, :] = rot[:, (n_q + hh) * hd:(n_q + hh + 1) * hd]
            v_h = p[:, n_rot + hh * hd:n_rot + (hh + 1) * hd].astype(BF16)
            v_ref[hh, rows, :] = jnp.concatenate([v_h, ones_col], axis=1)


def _qkv_proj(hb, w_qkv, bd, gain, scale, cos_t, sin_t, B, L, n_q, n_kv, hd, tm=512):
    T, D = hb.shape
    nw = w_qkv.shape[1]
    n_rot = (n_q + n_kv) * hd
    tm = min(tm, L)
    tpb = L // tm
    const = lambda i: (0, 0)
    return pl.pallas_call(
        functools.partial(_qkv_kernel, n_q, n_kv, hd),
        grid=(T // tm,),
        in_specs=[pl.BlockSpec((tm, D), lambda i: (i, 0)),
                  pl.BlockSpec((D, nw), const),
                  pl.BlockSpec(bd.shape, const),
                  pl.BlockSpec((1, n_rot), const),
                  pl.BlockSpec((1, n_rot), const),
                  pl.BlockSpec((tm, V7X_LANES), lambda i: (i % tpb, 0)),
                  pl.BlockSpec((tm, V7X_LANES), lambda i: (i % tpb, 0))],
        out_specs=[pl.BlockSpec((None, n_q, tm, hd), lambda i: (i // tpb, 0, i % tpb, 0)),
                   pl.BlockSpec((None, n_kv, tm, hd), lambda i: (i // tpb, 0, i % tpb, 0)),
                   pl.BlockSpec((None, n_kv, tm, V7X_LANES), lambda i: (i // tpb, 0, i % tpb, 0))],
        out_shape=[jax.ShapeDtypeStruct((B, n_q, L, hd), BF16),
                   jax.ShapeDtypeStruct((B, n_kv, L, hd), BF16),
                   jax.ShapeDtypeStruct((B, n_kv, L, V7X_LANES), BF16)],
        compiler_params=_params(("parallel",),
                                4 * tm * D + 4 * D * nw + 8 * tm * (n_q + 2 * n_kv) * V7X_LANES
                                + 48 * tm * nw),
        name="qkv_proj_norm_rope",
    )(hb, w_qkv, bd, gain, scale, cos_t, sin_t)


def _attn_kernel(group, hd, q_ref, k_ref, v_ref, o_ref):
    k = k_ref[...]
    v = v_ref[...]
    tq = q_ref.shape[1]
    ts = min(tq, ATTN_SUBTILE)
    for r0 in range(0, tq, ts):
        for g in range(group):
            s = lax.dot_general(q_ref[g, r0:r0 + ts, :], k, (((1,), (1,)), ((), ())),
                                preferred_element_type=F32)
            m = jnp.max(s, axis=-1, keepdims=True)
            e = jnp.exp(s - m).astype(BF16)
            o = _dot(e, v)
            o_ref[r0:r0 + ts, g * hd:(g + 1) * hd] = (o[:, :hd] / o[:, hd:hd + 1]).astype(o_ref.dtype)


def _attention(q, k, v, tq=2048):
    B, n_q, L, hd = q.shape
    n_kv = k.shape[1]
    group = n_q // n_kv
    tq = min(tq, L)
    return pl.pallas_call(
        functools.partial(_attn_kernel, group, hd),
        grid=(B, n_kv, L // tq),
        in_specs=[pl.BlockSpec((None, group, tq, hd), lambda b, j, t: (b, j, t, 0)),
                  pl.BlockSpec((None, None, L, hd), lambda b, j, t: (b, j, 0, 0)),
                  pl.BlockSpec((None, None, L, v.shape[-1]), lambda b, j, t: (b, j, 0, 0))],
        out_specs=pl.BlockSpec((None, tq, group * hd), lambda b, j, t: (b, t, j)),
        out_shape=jax.ShapeDtypeStruct((B, L, n_q * hd), BF16),
        compiler_params=_params(("parallel", "parallel", "parallel"),
                                8 * L * V7X_LANES + 8 * group * tq * V7X_LANES + 40 * tq * L),
        name="gqa_attention",
    )(q, k, v)


def _filter_kernel(z_ref, fw1_ref, fb1_ref, fw2_ref, fb2_ref, fr_ref, w3f_ref, w3b_ref, dl_ref,
                   hs_ref, hd_ref):
    z = z_ref[...]
    L = z.shape[0]
    fr = fr_ref[...]

    def hp_dot(a, b):
        a_hi, a_lo = _split_bf16(a)
        b_hi, b_lo = _split_bf16(b)
        return _dot3(a_hi, a_lo, b_hi, b_lo)

    a = jnp.sin(fr[0:1] * (hp_dot(z, fw1_ref[...]) + fb1_ref[...]))
    a = jnp.sin(fr[1:2] * (hp_dot(a, fw2_ref[...]) + fb2_ref[...]))
    window = jnp.exp(-z[:, 0:1] * dl_ref[...])
    h_fwd = hp_dot(a, w3f_ref[...]) * window
    h_bwd = hp_dot(a, w3b_ref[...]) * window
    row = lax.broadcasted_iota(jnp.int32, (L, 1), 0)
    h_bwd0 = jnp.where(row == 0, 0.0, h_bwd)
    hs_ref[...] = h_fwd + h_bwd0
    hd_ref[...] = h_bwd0 - h_fwd


def _filters(z, fw1p, fb1, fw2, fb2, fr, fw3, deltas, dh, tn=256):
    L, zw = z.shape
    nh = fw2.shape[0]
    nj = dh // tn
    const = lambda j: (0, 0)
    out = pl.BlockSpec((L, tn), lambda j: (0, j))
    return pl.pallas_call(
        _filter_kernel,
        grid=(nj,),
        in_specs=[pl.BlockSpec((L, zw), const), pl.BlockSpec((zw, nh), const),
                  pl.BlockSpec((1, nh), const), pl.BlockSpec((nh, nh), const),
                  pl.BlockSpec((1, nh), const), pl.BlockSpec((2, nh), const),
                  pl.BlockSpec((nh, tn), lambda j: (0, j)),
                  pl.BlockSpec((nh, tn), lambda j: (0, nj + j)),
                  pl.BlockSpec((1, tn), lambda j: (0, j))],
        out_specs=[out, out],
        out_shape=[jax.ShapeDtypeStruct((L, dh), F32)] * 2,
        compiler_params=_params(("parallel",), 32 * 1024 * 1024),
        name="hyena_filters",
    )(z, fw1p, fb1, fw2, fb2, fr, fw3, fw3, deltas)


def _spectrum_kernel(cs_ref, ss_ref, cp_ref, sp_ref, hs_ref, hd_ref, kre_ref, kim_ref):
    cs = cs_ref[...]
    ss = ss_ref[...]
    hs = hs_ref[...].astype(BF16)
    hd = hd_ref[...].astype(BF16)
    cp = cp_ref[...]
    sp = sp_ref[...]
    kre_ref[...] = cp * _dot(cs, hs) + sp * _dot(ss, hs)
    kim_ref[...] = cp * _dot(ss, hd) - sp * _dot(cs, hd)


def _spectrum(cs, ss, cos_phi, sin_phi, hs, hd, tk=512, tn=256):
    L, dh = hs.shape
    tk = min(tk, L)
    mat = pl.BlockSpec((tk, L), lambda j, i: (i, 0))
    vec = pl.BlockSpec((tk, 1), lambda j, i: (i, 0))
    col = pl.BlockSpec((L, tn), lambda j, i: (0, j))
    out = pl.BlockSpec((tk, tn), lambda j, i: (i, j))
    return pl.pallas_call(
        _spectrum_kernel,
        grid=(dh // tn, L // tk),
        in_specs=[mat, mat, vec, vec, col, col],
        out_specs=[out, out],
        out_shape=[jax.ShapeDtypeStruct((L, dh), F32)] * 2,
        compiler_params=_params(("parallel", "parallel"), 40 * 1024 * 1024),
        name="filter_spectrum",
    )(cs, ss, cos_phi, sin_phi, hs, hd)


def _longconv_kernel(inv_scale, cs_ref, ss_ref, v_ref, x0_ref, kre_ref, kim_ref, bias_ref, y_ref):
    cs = cs_ref[...]
    ss = ss_ref[...]
    v = v_ref[...]
    vb = v.astype(BF16)
    xc = _dot(cs, vb)
    xs = _dot(ss, vb)
    kre = kre_ref[...]
    kim = kim_ref[...]
    yre = (xc * kre + xs * kim).astype(BF16)
    yim = (xc * kim - xs * kre).astype(BF16)
    conv = (_dot(cs, yre) - _dot(ss, yim)) * inv_scale
    y_ref[...] = (x0_ref[...] * (conv + bias_ref[...] * v)).astype(y_ref.dtype)


def _longconv(cs, ss, vv, x0, kre, kim, bias, tc=256):
    B, L, dh = vv.shape
    mat = pl.BlockSpec((L, L), lambda j, b: (0, 0), pipeline_mode=pl.Buffered(1))
    act = pl.BlockSpec((None, L, tc), lambda j, b: (b, 0, j))
    spec = pl.BlockSpec((L, tc), lambda j, b: (0, j))
    return pl.pallas_call(
        functools.partial(_longconv_kernel, 1.0 / L),
        grid=(dh // tc, B),
        in_specs=[mat, mat, act, act, spec, spec, pl.BlockSpec((1, tc), lambda j, b: (0, j))],
        out_specs=act,
        out_shape=jax.ShapeDtypeStruct((B, L, dh), BF16),
        compiler_params=_params(("parallel", "parallel"),
                                4 * L * L + 16 * L * tc + 16 * L * tc + 4 * L * tc + 40 * L * tc),
        name="hyena_longconv",
    )(cs, ss, vv, x0, kre, kim, bias)


def _merge_kernel(h_ref, yh_ref, ya_ref, wgh_ref, wga_ref, wh_ref, wa_ref, o_ref):
    h = h_ref[...]
    gh = jax.nn.sigmoid(_dot(h, wgh_ref[...]))
    ga = jax.nn.sigmoid(_dot(h, wga_ref[...]))
    o_ref[...] = (gh * _dot(yh_ref[...], wh_ref[...])
                  + ga * _dot(ya_ref[...], wa_ref[...])).astype(o_ref.dtype)


def _merge(hb, y_hy, y_att, w_in_bf, w_hy, w_att, layer, gate_col0, tm=1024, tn=256):
    T, D = hb.shape
    dh, da = y_hy.shape[1], y_att.shape[1]
    tm = min(tm, T)
    g0 = gate_col0 // tn
    nj = D // tn
    return pl.pallas_call(
        _merge_kernel,
        grid=(T // tm, nj),
        in_specs=[pl.BlockSpec((tm, D), lambda i, j: (i, 0)),
                  pl.BlockSpec((tm, dh), lambda i, j: (i, 0)),
                  pl.BlockSpec((tm, da), lambda i, j: (i, 0)),
                  pl.BlockSpec((None, D, tn), lambda i, j: (layer, 0, g0 + j)),
                  pl.BlockSpec((None, D, tn), lambda i, j: (layer, 0, g0 + nj + j)),
                  pl.BlockSpec((None, dh, tn), lambda i, j: (layer, 0, j)),
                  pl.BlockSpec((None, da, tn), lambda i, j: (layer, 0, j))],
        out_specs=pl.BlockSpec((tm, tn), lambda i, j: (i, j)),
        out_shape=jax.ShapeDtypeStruct((T, D), BF16),
        compiler_params=_params(("parallel", "parallel"),
                                4 * tm * (D + dh + da) + 4 * tn * (2 * D + dh + da) + 4 * tm * tn
                                + 32 * tm * tn),
        name="gated_merge",
    )(hb, y_hy, y_att, w_in_bf, w_in_bf, w_hy, w_att)


def _oproj_ln_kernel(alpha, x_ref, w_ref, h_ref, g_ref, b_ref, of_ref, ob_ref):
    tm = x_ref.shape[0]
    ts = min(tm, EPILOGUE_SUBTILE)
    w = w_ref[...]
    for r0 in range(0, tm, ts):
        y = alpha * h_ref[r0:r0 + ts, :] + _dot(x_ref[r0:r0 + ts, :], w)
        y = _layer_norm_rows(y, g_ref[...], b_ref[...])
        of_ref[r0:r0 + ts, :] = y
        ob_ref[r0:r0 + ts, :] = y.astype(BF16)


def _oproj_ln(xb, w_o, h, g, b, layer, alpha, tm=512):
    T, D = h.shape
    row = lambda i: (i, 0)
    vec = pl.BlockSpec((None, 1, D), lambda i: (layer, 0, 0))
    return pl.pallas_call(
        functools.partial(_oproj_ln_kernel, alpha),
        grid=(T // tm,),
        in_specs=[pl.BlockSpec((tm, D), row),
                  pl.BlockSpec((None, D, D), lambda i: (layer, 0, 0)),
                  pl.BlockSpec((tm, D), row), vec, vec],
        out_specs=[pl.BlockSpec((tm, D), row), pl.BlockSpec((tm, D), row)],
        out_shape=[jax.ShapeDtypeStruct((T, D), F32), jax.ShapeDtypeStruct((T, D), BF16)],
        compiler_params=_params(("parallel",), 4 * D * D + 24 * tm * D + 24 * tm * D),
        name="oproj_resid_ln",
    )(xb, w_o, h, g, b)


def _swiglu_partial(x_ref, wg_ref, wu_ref, wd_ref):
    x = x_ref[...]
    g = _dot(x, wg_ref[...].astype(BF16))
    u = _dot(x, wu_ref[...].astype(BF16))
    a = (g * jax.nn.sigmoid(g) * u).astype(BF16)
    return _dot(a, wd_ref[...].astype(BF16))


def _ffn_dense_kernel(alpha, x_ref, wg_ref, wu_ref, wd_ref, h_ref, g_ref, b_ref,
                      of_ref, ob_ref, acc_ref):
    f = pl.program_id(1)

    @pl.when(f == 0)
    def _():
        acc_ref[...] = jnp.zeros_like(acc_ref)

    acc_ref[...] += _swiglu_partial(x_ref, wg_ref, wu_ref, wd_ref)

    @pl.when(f == pl.num_programs(1) - 1)
    def _():
        y = _layer_norm_rows(alpha * h_ref[...] + acc_ref[...], g_ref[...], b_ref[...])
        of_ref[...] = y
        ob_ref[...] = y.astype(BF16)


def _ffn_dense(xb, wg, wu, wd, h, g, b, j, layer, alpha, tm=512, tf=512):
    T, D = xb.shape
    F = wg.shape[-1]
    row = lambda i, f: (i, 0)
    vec = pl.BlockSpec((None, 1, D), lambda i, f: (layer, 0, 0))
    return pl.pallas_call(
        functools.partial(_ffn_dense_kernel, alpha),
        grid=(T // tm, F // tf),
        in_specs=[pl.BlockSpec((tm, D), row),
                  pl.BlockSpec((None, D, tf), lambda i, f: (j, 0, f)),
                  pl.BlockSpec((None, D, tf), lambda i, f: (j, 0, f)),
                  pl.BlockSpec((None, tf, D), lambda i, f: (j, f, 0)),
                  pl.BlockSpec((tm, D), row), vec, vec],
        out_specs=[pl.BlockSpec((tm, D), row), pl.BlockSpec((tm, D), row)],
        out_shape=[jax.ShapeDtypeStruct((T, D), F32), jax.ShapeDtypeStruct((T, D), BF16)],
        scratch_shapes=[pltpu.VMEM((tm, D), F32)],
        compiler_params=_params(("parallel", "arbitrary"),
                                4 * tm * D + 12 * D * tf + 8 * tm * D + 12 * tm * D + 4 * tm * D
                                + 16 * tm * tf + 8 * tm * D),
        name="ffn_dense_swiglu_ln",
    )(xb, wg, wu, wd, h, g, b)


def _ffn_group_kernel(blk0, be_ref, nv_ref, x_ref, wg_ref, wu_ref, wd_ref, *rest):
    o_ref = rest[-1]
    f = pl.program_id(1)

    @pl.when(blk0 + pl.program_id(0) < nv_ref[0])
    def _():
        @pl.when(f == 0)
        def _():
            o_ref[...] = jnp.zeros_like(o_ref)

        o_ref[...] += _swiglu_partial(x_ref, wg_ref, wu_ref, wd_ref)


def _ffn_grouped(blk_e, n_valid, xg, wg, wu, wd, j, tm, blk0, nblk, prev=None, tf=256):
    D = xg.shape[1]
    F = wg.shape[-1]
    nf = F // tf
    nb = xg.shape[0] // tm

    def live(i, nv):
        return blk0 + i < nv[0]

    def x_rows(i, f, be, nv):
        return (jnp.where(live(i, nv), i, 0), 0)

    def out_rows(i, f, be, nv):
        return (jnp.where(live(i, nv), blk0 + i, nblk), 0)

    def hidden(i, f, nv):
        return jnp.where(live(i, nv), f, nf - 1)

    in_specs = [pl.BlockSpec((tm, D), x_rows),
                pl.BlockSpec((None, None, D, tf),
                             lambda i, f, be, nv: (j, be[blk0 + i], 0, hidden(i, f, nv))),
                pl.BlockSpec((None, None, D, tf),
                             lambda i, f, be, nv: (j, be[blk0 + i], 0, hidden(i, f, nv))),
                pl.BlockSpec((None, None, tf, D),
                             lambda i, f, be, nv: (j, be[blk0 + i], hidden(i, f, nv), 0))]
    args = [blk_e, n_valid, xg, wg, wu, wd]
    aliases = {}
    if prev is not None:
        in_specs.append(pl.BlockSpec(memory_space=pl.ANY))
        aliases = {len(args): 0}
        args.append(prev)
    return pl.pallas_call(
        functools.partial(_ffn_group_kernel, blk0),
        grid_spec=pltpu.PrefetchScalarGridSpec(
            num_scalar_prefetch=2,
            grid=(nb, nf),
            in_specs=in_specs,
            out_specs=pl.BlockSpec((tm, D), out_rows)),
        out_shape=jax.ShapeDtypeStruct(((nblk + 1) * tm, D), F32),
        input_output_aliases=aliases,
        compiler_params=_params(("arbitrary", "arbitrary"),
                                4 * tm * D + 24 * D * tf + 8 * tm * D
                                + 6 * D * tf + 16 * tm * tf + 8 * tm * D),
        name="moe_grouped_swiglu",
    )(*args)


def _router_kernel(x_ref, wh_ref, wl_ref, b_ref, tri_ref, e_ref, w_ref, r_ref, cnt_ref):
    @pl.when(pl.program_id(0) == 0)
    def _():
        cnt_ref[...] = jnp.zeros_like(cnt_ref)

    x_hi, x_lo = _split_bf16(x_ref[...])
    nt = (((1,), (1,)), ((), ()))

    def dg(a, b):
        return lax.dot_general(a, b, nt, preferred_element_type=F32)

    wh = wh_ref[...]
    wl = wl_ref[...]
    logits = dg(wh, x_hi) + (dg(wh, x_lo) + dg(wl, x_hi)) + b_ref[...]
    n_e = logits.shape[0]
    eid = lax.broadcasted_iota(jnp.int32, logits.shape, 0)
    m1 = jnp.max(logits, axis=0, keepdims=True)
    i1 = jnp.min(jnp.where(logits == m1, eid, n_e), axis=0, keepdims=True)
    rest = jnp.where(eid == i1, -jnp.inf, logits)
    m2 = jnp.max(rest, axis=0, keepdims=True)
    i2 = jnp.min(jnp.where(rest == m2, eid, n_e), axis=0, keepdims=True)
    t = jnp.exp(m2 - m1)
    w1 = 1.0 / (1.0 + t)
    e_ref[...] = jnp.concatenate([i1, i2], axis=0)
    w_ref[...] = jnp.concatenate([w1, t * w1], axis=0)

    tri = tri_ref[...]
    oh1 = (eid == i1).astype(F32)
    oh2 = (eid == i2).astype(F32)
    tot1 = jnp.sum(oh1, axis=1, keepdims=True)
    tot2 = jnp.sum(oh2, axis=1, keepdims=True)
    base = cnt_ref[...][:, 0:1]
    c1 = base + _dot(oh1.astype(BF16), tri)
    c2 = base + tot1 + _dot(oh2.astype(BF16), tri)
    r1 = jnp.sum(oh1 * c1, axis=0, keepdims=True)
    r2 = jnp.sum(oh2 * c2, axis=0, keepdims=True)
    r_ref[...] = jnp.concatenate([r1, r2], axis=0).astype(jnp.int32)
    cnt_ref[...] = cnt_ref[...] + (tot1 + tot2)


def _router(h, rw_hi, rw_lo, rb, tm=512):
    T, D = h.shape
    E = rw_hi.shape[0]
    const = lambda i: (0, 0)
    out = pl.BlockSpec((TOP_K, tm), lambda i: (0, i))
    idx = jnp.arange(tm)
    tri = (idx[:, None] < idx[None, :]).astype(BF16)
    return pl.pallas_call(
        _router_kernel,
        grid=(T // tm,),
        in_specs=[pl.BlockSpec((tm, D), lambda i: (i, 0)), pl.BlockSpec((E, D), const),
                  pl.BlockSpec((E, D), const), pl.BlockSpec((E, 1), const),
                  pl.BlockSpec((tm, tm), const)],
        out_specs=[out, out, out, pl.BlockSpec((E, V7X_LANES), const)],
        out_shape=[jax.ShapeDtypeStruct((TOP_K, T), jnp.int32),
                   jax.ShapeDtypeStruct((TOP_K, T), F32),
                   jax.ShapeDtypeStruct((TOP_K, T), jnp.int32),
                   jax.ShapeDtypeStruct((E, V7X_LANES), F32)],
        compiler_params=_params(("arbitrary",), 32 * 1024 * 1024),
        name="moe_router_top2",
    )(h, rw_hi, rw_lo, rb, tri)


def _moe_combine_ln_kernel(alpha, h_ref, ya_ref, yb_ref, w_ref, g_ref, b_ref, of_ref, ob_ref):
    w = w_ref[...]
    f = ya_ref[...] * w[:, 0:1] + yb_ref[...] * w[:, 1:2]
    y = _layer_norm_rows(alpha * h_ref[...] + f, g_ref[...], b_ref[...])
    of_ref[...] = y
    ob_ref[...] = y.astype(BF16)


def _moe_combine_ln(h, ya, yb, w, g, b, alpha, tm=256):
    T, D = h.shape
    row = pl.BlockSpec((tm, D), lambda i: (i, 0))
    vec = pl.BlockSpec((1, D), lambda i: (0, 0))
    return pl.pallas_call(
        functools.partial(_moe_combine_ln_kernel, alpha),
        grid=(T // tm,),
        in_specs=[row, row, row, pl.BlockSpec((tm, w.shape[1]), lambda i: (i, 0)), vec, vec],
        out_specs=[row, row],
        out_shape=[jax.ShapeDtypeStruct((T, D), F32), jax.ShapeDtypeStruct((T, D), BF16)],
        compiler_params=_params(("parallel",), 64 * tm * D),
        name="moe_combine_ln",
    )(h, ya, yb, w, g.reshape(1, D), b.reshape(1, D))


def _dft_tables(L):
    n2 = 2 * L
    k = jnp.arange(L, dtype=jnp.int32)[:, None]
    n = jnp.arange(L, dtype=jnp.int32)[None, :]
    ang = (((2 * k + 1) * (2 * n + 1)) % (4 * n2)).astype(F32) * (math.pi / (2 * n2))
    phi = (2 * k + 1).astype(F32) * (math.pi / (2 * n2))
    return jnp.cos(ang).astype(BF16), jnp.sin(ang).astype(BF16), jnp.cos(phi), jnp.sin(phi)


def _rope_tables(L, hd):
    rows = L // GRID_W
    row = jnp.repeat(jnp.arange(rows, dtype=F32), GRID_W)
    col = jnp.tile(jnp.arange(GRID_W, dtype=F32), rows)
    n_pairs = hd // 4
    inv = ROPE_THETA ** (-jnp.arange(n_pairs, dtype=F32) / n_pairs)
    ang = jnp.concatenate([row[:, None] * inv, col[:, None] * inv], axis=-1)
    cos, sin = jnp.cos(ang), jnp.sin(ang)
    reps = V7X_LANES // hd
    return (jnp.tile(jnp.concatenate([cos, cos], axis=-1), (1, reps)),
            jnp.tile(jnp.concatenate([-sin, sin], axis=-1), (1, reps)))


def _filter_features(L, n_emb):
    t = jnp.linspace(0.0, 1.0, L, dtype=F32)[:, None]
    bands = (n_emb - 1) // 2
    w = 2.0 * math.pi * jnp.arange(L, dtype=F32)[:, None] / L
    f = jnp.linspace(1e-4, bands - 1, bands, dtype=F32)[None, :]
    z = jnp.concatenate([t, jnp.cos(f * w), -jnp.sin(f * w)], axis=-1)
    return jnp.pad(z, ((0, 0), (0, V7X_LANES - n_emb)))


def _moe_block_rows(T):
    return min(1024, max(V7X_MXU_DIM, T // 4))


def _moe(hf, hb, router_w, router_b, wg, wu, wd, j):
    T, D = hf.shape
    E = router_w.shape[1]
    rw_hi, rw_lo = _split_bf16(router_w.T)
    top_e, top_w, rank, cnt = _router(hf, rw_hi, rw_lo, router_b.reshape(E, 1))

    tm = _moe_block_rows(T)
    counts = cnt[:, 0].astype(jnp.int32)
    padded = (counts + tm - 1) // tm * tm
    pad_end = jnp.cumsum(padded)
    pad_start = pad_end - padded
    nblk = -(-(T * TOP_K + E * (tm - 1)) // tm)
    P = nblk * tm
    n_valid = (pad_end[-1] // tm).astype(jnp.int32)
    blk_first = jnp.arange(nblk, dtype=jnp.int32) * tm
    blk_e = jnp.sum((blk_first[:, None] >= pad_end[None, :]).astype(jnp.int32), axis=1)
    blk_e = jnp.minimum(blk_e, E - 1)
    blk_e = jnp.where(jnp.arange(nblk) < n_valid, blk_e, blk_e[jnp.maximum(n_valid - 1, 0)])
    start_of = jnp.sum(jnp.where(top_e[:, :, None] == jnp.arange(E, dtype=jnp.int32),
                                 pad_start[None, None, :], 0), axis=-1)
    dest = start_of + rank

    tok = jnp.broadcast_to(jnp.arange(T, dtype=jnp.int32), (TOP_K, T))
    src_tok = jnp.zeros((P,), jnp.int32).at[dest.reshape(-1)].set(tok.reshape(-1))
    yg = None
    nv = n_valid.reshape(1)
    for blk0 in range(0, nblk, -(-nblk // MOE_CALLS)):
        nb = min(-(-nblk // MOE_CALLS), nblk - blk0)
        xg = hb[lax.slice_in_dim(src_tok, blk0 * tm, (blk0 + nb) * tm)]
        yg = _ffn_grouped(blk_e, nv, xg, wg, wu, wd, j, tm, blk0, nblk, prev=yg)
    return yg[dest[0]], yg[dest[1]], top_w.T


def kernel(x, ln_in_g, ln_in_b, w_in, hy_conv_w, hy_conv_b, hy_fw1, hy_fb1, hy_fw2, hy_fb2, hy_fw3, hy_freq, hy_bias, q_norm_g, k_norm_g, w_hy_br, w_att_br, w_o, ln_mix_g, ln_mix_b, ffn_w_gate, ffn_w_up, ffn_w_down, router_w, router_b, exp_w_gate, exp_w_up, exp_w_down, ln_ffn_g, ln_ffn_b):
    B, L, D = x.shape
    T = B * L
    depth = w_in.shape[0]
    dh = w_hy_br.shape[1]
    d_attn = w_att_br.shape[1]
    hd = q_norm_g.shape[-1]
    n_q = d_attn // hd
    d_kv = (w_in.shape[2] - 3 * dh - d_attn - 2 * D) // 2
    n_kv = d_kv // hd
    alpha = (2 * depth) ** 0.25
    qkv0 = 3 * dh
    gate0 = qkv0 + d_attn + 2 * d_kv
    n_rot = (n_q + n_kv) * hd
    assert L % GRID_W == 0 and V7X_LANES % hd == 0 and n_rot % V7X_MXU_DIM == 0

    cs_bf, ss_bf, cos_phi, sin_phi = _dft_tables(L)
    cos_t, sin_t = _rope_tables(L, hd)
    z = _filter_features(L, hy_fw1.shape[1])
    min_decay = math.log(DECAY_TARGET) / SLOW_DECAY_PCT
    max_decay = math.log(DECAY_TARGET) / FAST_DECAY_PCT
    deltas = jnp.abs(jnp.linspace(min_decay, max_decay, dh, dtype=F32)).reshape(1, dh)
    lane = jnp.arange(V7X_MXU_DIM)
    bd = jnp.where((lane[:, None] // hd) == (lane[None, :] // hd), 1.0 / hd, 0.0).astype(BF16)
    qk_scale = jnp.concatenate([jnp.full((n_q * hd,), hd ** -0.5, F32),
                                jnp.ones((n_kv * hd,), F32)]).reshape(1, n_rot)

    w_in_bf = w_in.astype(BF16)
    w_hy_bf = w_hy_br.astype(BF16)
    w_att_bf = w_att_br.astype(BF16)
    w_o_bf = w_o.astype(BF16)
    ffn_g_bf = ffn_w_gate.astype(BF16)
    ffn_u_bf = ffn_w_up.astype(BF16)
    ffn_d_bf = ffn_w_down.astype(BF16)
    conv_b3 = hy_conv_b.reshape(depth, 1, 3 * dh)
    ln_mix_g3, ln_mix_b3 = ln_mix_g.reshape(depth, 1, D), ln_mix_b.reshape(depth, 1, D)
    ln_ffn_g3, ln_ffn_b3 = ln_ffn_g.reshape(depth, 1, D), ln_ffn_b.reshape(depth, 1, D)

    hf, hb = _resid_ln(x.reshape(T, D), [], ln_in_g, ln_in_b, alpha)
    for i in range(depth):
        x0, vv = _hy_proj(hb.reshape(B, L, D), w_in_bf, hy_conv_w, conv_b3, i, dh)
        fw1p = jnp.pad(hy_fw1[i], ((0, V7X_LANES - hy_fw1.shape[1]), (0, 0)))
        hs, hdf = _filters(z, fw1p, hy_fb1[i].reshape(1, -1), hy_fw2[i], hy_fb2[i].reshape(1, -1),
                           hy_freq[i], hy_fw3[i], deltas, dh)
        kre, kim = _spectrum(cs_bf, ss_bf, cos_phi, sin_phi, hs, hdf)
        y_hy = _longconv(cs_bf, ss_bf, vv, x0, kre, kim, hy_bias[i].reshape(1, dh))
        gain = jnp.concatenate([jnp.tile(q_norm_g[i], n_q), jnp.tile(k_norm_g[i], n_kv)]).reshape(1, n_rot)
        w_qkv = lax.slice_in_dim(w_in_bf[i], qkv0, gate0, axis=1)
        q, k, v = _qkv_proj(hb, w_qkv, bd, gain, qk_scale, cos_t, sin_t, B, L, n_q, n_kv, hd)
        y_att = _attention(q, k, v)
        merged = _merge(hb, y_hy.reshape(T, dh), y_att.reshape(T, d_attn), w_in_bf, w_hy_bf, w_att_bf,
                        i, gate0)
        hf, hb = _oproj_ln(merged, w_o_bf, hf, ln_mix_g3, ln_mix_b3, i, alpha)
        j = i // 2
        if i % 2 == 0:
            hf, hb = _ffn_dense(hb, ffn_g_bf, ffn_u_bf, ffn_d_bf, hf, ln_ffn_g3, ln_ffn_b3, j, i, alpha)
        else:
            ya, yb, wts = _moe(hf, hb, router_w[j], router_b[j], exp_w_gate, exp_w_up, exp_w_down, j)
            hf, hb = _moe_combine_ln(hf, ya, yb, wts, ln_ffn_g[i], ln_ffn_b[i], alpha)
    return hf.reshape(B, L, D)
```

```python
import functools
import math

import jax
import jax.numpy as jnp
from jax import lax
from jax.experimental import pallas as pl
from jax.experimental.pallas import tpu as pltpu

F32 = jnp.float32
BF16 = jnp.bfloat16

GRID_W = 64
ROPE_THETA = 10000.0
TOP_K = 2
DECAY_TARGET = 1e-2
FAST_DECAY_PCT = 0.3
SLOW_DECAY_PCT = 1.5
LN_EPS = 1e-5
RMS_EPS = 1e-6

V7X_VMEM_BYTES = 64 * 1024 * 1024
V7X_LANES = 128
V7X_MXU_DIM = 256
VMEM_BUDGET = V7X_VMEM_BYTES - 8 * 1024 * 1024
ATTN_SUBTILE = 512
EPILOGUE_SUBTILE = V7X_MXU_DIM
MOE_CALLS = 4
MOE_ROW_STEP = V7X_MXU_DIM


def _params(semantics, vmem_bytes):
    return pltpu.CompilerParams(dimension_semantics=semantics,
                                vmem_limit_bytes=min(int(vmem_bytes), VMEM_BUDGET))


def _dot(a, b):
    return jnp.dot(a, b, preferred_element_type=F32)


def _split_bf16(a):
    hi = a.astype(BF16)
    lo = (a - hi.astype(F32)).astype(BF16)
    return hi, lo


def _dot3(a_hi, a_lo, b_hi, b_lo):
    return _dot(a_hi, b_hi) + (_dot(a_hi, b_lo) + _dot(a_lo, b_hi))


def _layer_norm_rows(x, g, b):
    mu = jnp.mean(x, axis=-1, keepdims=True)
    xc = x - mu
    var = jnp.mean(xc * xc, axis=-1, keepdims=True)
    return xc * lax.rsqrt(var + LN_EPS) * g + b


def _resid_ln_kernel(alpha, n_add, *refs):
    h_ref = refs[0]
    add_refs = refs[1:1 + n_add]
    g_ref, b_ref, of_ref, ob_ref = refs[1 + n_add:]
    x = h_ref[...]
    if n_add:
        acc = add_refs[0][...]
        for r in add_refs[1:]:
            acc = acc + r[...]
        x = alpha * x + acc
    y = _layer_norm_rows(x, g_ref[...], b_ref[...])
    of_ref[...] = y
    ob_ref[...] = y.astype(BF16)


def _resid_ln(h, addends, g, b, alpha, tm=256):
    T, D = h.shape
    n_add = len(addends)
    row = pl.BlockSpec((tm, D), lambda i: (i, 0))
    vec = pl.BlockSpec((1, D), lambda i: (0, 0))
    return pl.pallas_call(
        functools.partial(_resid_ln_kernel, alpha, n_add),
        grid=(T // tm,),
        in_specs=[row] * (1 + n_add) + [vec, vec],
        out_specs=[row, row],
        out_shape=[jax.ShapeDtypeStruct((T, D), F32), jax.ShapeDtypeStruct((T, D), BF16)],
        compiler_params=_params(("parallel",), (2 * (1 + n_add) * 4 + 12 + 16) * tm * D),
        name="resid_ln",
    )(h, *addends, g.reshape(1, D), b.reshape(1, D))


def _hy_proj_kernel(h_ref, w0_ref, w1_ref, w2_ref, cw0_ref, cw1_ref, cw2_ref,
                    cb0_ref, cb1_ref, cb2_ref, x0_ref, vv_ref):
    h = h_ref[...]
    L = h.shape[0]
    row = lax.broadcasted_iota(jnp.int32, (L, 1), 0)

    def stream(w_ref, cw_ref, cb_ref):
        p = _dot(h, w_ref[...])
        prev = jnp.where(row == 0, 0.0, pltpu.roll(p, 1, 0))
        nxt = jnp.where(row == L - 1, 0.0, pltpu.roll(p, L - 1, 0))
        cw = cw_ref[...]
        return prev * cw[0:1] + p * cw[1:2] + nxt * cw[2:3] + cb_ref[...]

    x0_ref[...] = stream(w0_ref, cw0_ref, cb0_ref)
    vv_ref[...] = stream(w1_ref, cw1_ref, cb1_ref) * stream(w2_ref, cw2_ref, cb2_ref)


def _hy_proj(hb3, w_in_bf, conv_w, conv_b3, layer, dh, tn=256):
    B, L, D = hb3.shape
    nj = dh // tn

    def wspec(s):
        return pl.BlockSpec((None, D, tn), lambda j, b: (layer, 0, s * nj + j))

    def cwspec(s):
        return pl.BlockSpec((None, conv_w.shape[1], tn), lambda j, b: (layer, 0, s * nj + j))

    def cbspec(s):
        return pl.BlockSpec((None, 1, tn), lambda j, b: (layer, 0, s * nj + j))

    out = pl.BlockSpec((None, L, tn), lambda j, b: (b, 0, j))
    return pl.pallas_call(
        _hy_proj_kernel,
        grid=(nj, B),
        in_specs=[pl.BlockSpec((None, L, D), lambda j, b: (b, 0, 0)),
                  wspec(0), wspec(1), wspec(2), cwspec(0), cwspec(1), cwspec(2),
                  cbspec(0), cbspec(1), cbspec(2)],
        out_specs=[out, out],
        out_shape=[jax.ShapeDtypeStruct((B, L, dh), F32)] * 2,
        compiler_params=_params(("parallel", "parallel"),
                                4 * L * D + 12 * D * tn + 16 * L * tn + 40 * L * tn),
        name="hyena_proj_conv",
    )(hb3, w_in_bf, w_in_bf, w_in_bf, conv_w, conv_w, conv_w, conv_b3, conv_b3, conv_b3)


def _qkv_kernel(n_q, n_kv, hd, x_ref, w_ref, bd_ref, gain_ref, scale_ref, cos_ref, sin_ref,
                q_ref, k_ref, v_ref):
    n_rot = (n_q + n_kv) * hd
    tm = x_ref.shape[0]
    ts = min(tm, EPILOGUE_SUBTILE)
    w = w_ref[...]
    bd = bd_ref[...]
    gw = bd.shape[0]
    reps = n_rot // V7X_LANES
    lane = lax.broadcasted_iota(jnp.int32, (1, n_rot), 1)
    first_half = (lane % hd) < (hd // 2)
    half = hd // 2
    pad_lane = lax.broadcasted_iota(jnp.int32, (ts, V7X_LANES - hd), 1)
    ones_col = jnp.where(pad_lane == 0, 1.0, 0.0).astype(BF16)
    for r0 in range(0, tm, ts):
        rows = slice(r0, r0 + ts)
        p = _dot(x_ref[rows, :], w)
        pr = p[:, :n_rot]
        sq = (pr * pr).astype(BF16)
        ms = jnp.concatenate([_dot(sq[:, c:c + gw], bd) for c in range(0, n_rot, gw)], axis=1)
        xn = pr * lax.rsqrt(ms + RMS_EPS) * gain_ref[...]
        cos = jnp.concatenate([cos_ref[rows, :]] * reps, axis=1)
        sin = jnp.concatenate([sin_ref[rows, :]] * reps, axis=1)
        swapped = jnp.where(first_half, pltpu.roll(xn, n_rot - half, 1), pltpu.roll(xn, half, 1))
        rot = ((xn * cos + swapped * sin) * scale_ref[...]).astype(BF16)
        for hh in range(n_q):
            q_ref[hh, rows, :] = rot[:, hh * hd:(hh + 1) * hd]
        for hh in range(n_kv):
            k_ref[hh, rows, :] = rot[:, (n_q + hh) * hd:(n_q + hh + 1) * hd]
            v_h = p[:, n_rot + hh * hd:n_rot + (hh + 1) * hd].astype(BF16)
            v_ref[hh, rows, :] = jnp.concatenate([v_h, ones_col], axis=1)


def _qkv_proj(hb, w_qkv, bd, gain, scale, cos_t, sin_t, B, L, n_q, n_kv, hd, tm=512):
    T, D = hb.shape
    nw = w_qkv.shape[1]
    n_rot = (n_q + n_kv) * hd
    tm = min(tm, L)
    tpb = L // tm
    const = lambda i: (0, 0)
    return pl.pallas_call(
        functools.partial(_qkv_kernel, n_q, n_kv, hd),
        grid=(T // tm,),
        in_specs=[pl.BlockSpec((tm, D), lambda i: (i, 0)),
                  pl.BlockSpec((D, nw), const),
                  pl.BlockSpec(bd.shape, const),
                  pl.BlockSpec((1, n_rot), const),
                  pl.BlockSpec((1, n_rot), const),
                  pl.BlockSpec((tm, V7X_LANES), lambda i: (i % tpb, 0)),
                  pl.BlockSpec((tm, V7X_LANES), lambda i: (i % tpb, 0))],
        out_specs=[pl.BlockSpec((None, n_q, tm, hd), lambda i: (i // tpb, 0, i % tpb, 0)),
                   pl.BlockSpec((None, n_kv, tm, hd), lambda i: (i // tpb, 0, i % tpb, 0)),
                   pl.BlockSpec((None, n_kv, tm, V7X_LANES), lambda i: (i // tpb, 0, i % tpb, 0))],
        out_shape=[jax.ShapeDtypeStruct((B, n_q, L, hd), BF16),
                   jax.ShapeDtypeStruct((B, n_kv, L, hd), BF16),
                   jax.ShapeDtypeStruct((B, n_kv, L, V7X_LANES), BF16)],
        compiler_params=_params(("parallel",),
                                4 * tm * D + 4 * D * nw + 8 * tm * (n_q + 2 * n_kv) * V7X_LANES
                                + 48 * tm * nw),
        name="qkv_proj_norm_rope",
    )(hb, w_qkv, bd, gain, scale, cos_t, sin_t)


def _attn_kernel(group, hd, q_ref, k_ref, v_ref, o_ref):
    k = k_ref[...]
    v = v_ref[...]
    tq = q_ref.shape[1]
    ts = min(tq, ATTN_SUBTILE)
    for r0 in range(0, tq, ts):
        for g in range(group):
            s = lax.dot_general(q_ref[g, r0:r0 + ts, :], k, (((1,), (1,)), ((), ())),
                                preferred_element_type=F32)
            m = jnp.max(s, axis=-1, keepdims=True)
            e = jnp.exp(s - m).astype(BF16)
            o = _dot(e, v)
            o_ref[r0:r0 + ts, g * hd:(g + 1) * hd] = (o[:, :hd] / o[:, hd:hd + 1]).astype(o_ref.dtype)


def _attention(q, k, v, tq=2048):
    B, n_q, L, hd = q.shape
    n_kv = k.shape[1]
    group = n_q // n_kv
    tq = min(tq, L)
    return pl.pallas_call(
        functools.partial(_attn_kernel, group, hd),
        grid=(B, n_kv, L // tq),
        in_specs=[pl.BlockSpec((None, group, tq, hd), lambda b, j, t: (b, j, t, 0)),
                  pl.BlockSpec((None, None, L, hd), lambda b, j, t: (b, j, 0, 0)),
                  pl.BlockSpec((None, None, L, v.shape[-1]), lambda b, j, t: (b, j, 0, 0))],
        out_specs=pl.BlockSpec((None, tq, group * hd), lambda b, j, t: (b, t, j)),
        out_shape=jax.ShapeDtypeStruct((B, L, n_q * hd), BF16),
        compiler_params=_params(("parallel", "parallel", "parallel"),
                                8 * L * V7X_LANES + 8 * group * tq * V7X_LANES + 40 * tq * L),
        name="gqa_attention",
    )(q, k, v)


def _filter_kernel(z_ref, fw1_ref, fb1_ref, fw2_ref, fb2_ref, fr_ref, w3f_ref, w3b_ref, dl_ref,
                   hs_ref, hd_ref):
    z = z_ref[...]
    L = z.shape[0]
    fr = fr_ref[...]

    def hp_dot(a, b):
        a_hi, a_lo = _split_bf16(a)
        b_hi, b_lo = _split_bf16(b)
        return _dot3(a_hi, a_lo, b_hi, b_lo)

    a = jnp.sin(fr[0:1] * (hp_dot(z, fw1_ref[...]) + fb1_ref[...]))
    a = jnp.sin(fr[1:2] * (hp_dot(a, fw2_ref[...]) + fb2_ref[...]))
    window = jnp.exp(-z[:, 0:1] * dl_ref[...])
    h_fwd = hp_dot(a, w3f_ref[...]) * window
    h_bwd = hp_dot(a, w3b_ref[...]) * window
    row = lax.broadcasted_iota(jnp.int32, (L, 1), 0)
    h_bwd0 = jnp.where(row == 0, 0.0, h_bwd)
    hs_ref[...] = h_fwd + h_bwd0
    hd_ref[...] = h_bwd0 - h_fwd


def _filters(z, fw1p, fb1, fw2, fb2, fr, fw3, deltas, dh, tn=256):
    L, zw = z.shape
    nh = fw2.shape[0]
    nj = dh // tn
    const = lambda j: (0, 0)
    out = pl.BlockSpec((L, tn), lambda j: (0, j))
    return pl.pallas_call(
        _filter_kernel,
        grid=(nj,),
        in_specs=[pl.BlockSpec((L, zw), const), pl.BlockSpec((zw, nh), const),
                  pl.BlockSpec((1, nh), const), pl.BlockSpec((nh, nh), const),
                  pl.BlockSpec((1, nh), const), pl.BlockSpec((2, nh), const),
                  pl.BlockSpec((nh, tn), lambda j: (0, j)),
                  pl.BlockSpec((nh, tn), lambda j: (0, nj + j)),
                  pl.BlockSpec((1, tn), lambda j: (0, j))],
        out_specs=[out, out],
        out_shape=[jax.ShapeDtypeStruct((L, dh), F32)] * 2,
        compiler_params=_params(("parallel",), 32 * 1024 * 1024),
        name="hyena_filters",
    )(z, fw1p, fb1, fw2, fb2, fr, fw3, fw3, deltas)


def _spectrum_kernel(cs_ref, ss_ref, cp_ref, sp_ref, hs_ref, hd_ref, kre_ref, kim_ref):
    cs = cs_ref[...]
    ss = ss_ref[...]
    hs = hs_ref[...].astype(BF16)
    hd = hd_ref[...].astype(BF16)
    cp = cp_ref[...]
    sp = sp_ref[...]
    kre_ref[...] = cp * _dot(cs, hs) + sp * _dot(ss, hs)
    kim_ref[...] = cp * _dot(ss, hd) - sp * _dot(cs, hd)


def _spectrum(cs, ss, cos_phi, sin_phi, hs, hd, tk=512, tn=256):
    L, dh = hs.shape
    tk = min(tk, L)
    mat = pl.BlockSpec((tk, L), lambda j, i: (i, 0))
    vec = pl.BlockSpec((tk, 1), lambda j, i: (i, 0))
    col = pl.BlockSpec((L, tn), lambda j, i: (0, j))
    out = pl.BlockSpec((tk, tn), lambda j, i: (i, j))
    return pl.pallas_call(
        _spectrum_kernel,
        grid=(dh // tn, L // tk),
        in_specs=[mat, mat, vec, vec, col, col],
        out_specs=[out, out],
        out_shape=[jax.ShapeDtypeStruct((L, dh), F32)] * 2,
        compiler_params=_params(("parallel", "parallel"), 40 * 1024 * 1024),
        name="filter_spectrum",
    )(cs, ss, cos_phi, sin_phi, hs, hd)


def _longconv_kernel(inv_scale, cs_ref, ss_ref, v_ref, x0_ref, kre_ref, kim_ref, bias_ref, y_ref):
    cs = cs_ref[...]
    ss = ss_ref[...]
    v = v_ref[...]
    vb = v.astype(BF16)
    xc = _dot(cs, vb)
    xs = _dot(ss, vb)
    kre = kre_ref[...]
    kim = kim_ref[...]
    yre = (xc * kre + xs * kim).astype(BF16)
    yim = (xc * kim - xs * kre).astype(BF16)
    conv = (_dot(cs, yre) - _dot(ss, yim)) * inv_scale
    y_ref[...] = (x0_ref[...] * (conv + bias_ref[...] * v)).astype(y_ref.dtype)


def _longconv(cs, ss, vv, x0, kre, kim, bias, tc=256):
    B, L, dh = vv.shape
    mat = pl.BlockSpec((L, L), lambda j, b: (0, 0), pipeline_mode=pl.Buffered(1))
    act = pl.BlockSpec((None, L, tc), lambda j, b: (b, 0, j))
    spec = pl.BlockSpec((L, tc), lambda j, b: (0, j))
    return pl.pallas_call(
        functools.partial(_longconv_kernel, 1.0 / L),
        grid=(dh // tc, B),
        in_specs=[mat, mat, act, act, spec, spec, pl.BlockSpec((1, tc), lambda j, b: (0, j))],
        out_specs=act,
        out_shape=jax.ShapeDtypeStruct((B, L, dh), BF16),
        compiler_params=_params(("parallel", "parallel"),
                                4 * L * L + 16 * L * tc + 16 * L * tc + 4 * L * tc + 40 * L * tc),
        name="hyena_longconv",
    )(cs, ss, vv, x0, kre, kim, bias)


def _merge_kernel(h_ref, yh_ref, ya_ref, wgh_ref, wga_ref, wh_ref, wa_ref, o_ref):
    h = h_ref[...]
    gh = jax.nn.sigmoid(_dot(h, wgh_ref[...]))
    ga = jax.nn.sigmoid(_dot(h, wga_ref[...]))
    o_ref[...] = (gh * _dot(yh_ref[...], wh_ref[...])
                  + ga * _dot(ya_ref[...], wa_ref[...])).astype(o_ref.dtype)


def _merge(hb, y_hy, y_att, w_in_bf, w_hy, w_att, layer, gate_col0, tm=1024, tn=512):
    T, D = hb.shape
    dh, da = y_hy.shape[1], y_att.shape[1]
    tm = min(tm, T)
    g0 = gate_col0 // tn
    nj = D // tn
    return pl.pallas_call(
        _merge_kernel,
        grid=(T // tm, nj),
        in_specs=[pl.BlockSpec((tm, D), lambda i, j: (i, 0)),
                  pl.BlockSpec((tm, dh), lambda i, j: (i, 0)),
                  pl.BlockSpec((tm, da), lambda i, j: (i, 0)),
                  pl.BlockSpec((None, D, tn), lambda i, j: (layer, 0, g0 + j)),
                  pl.BlockSpec((None, D, tn), lambda i, j: (layer, 0, g0 + nj + j)),
                  pl.BlockSpec((None, dh, tn), lambda i, j: (layer, 0, j)),
                  pl.BlockSpec((None, da, tn), lambda i, j: (layer, 0, j))],
        out_specs=pl.BlockSpec((tm, tn), lambda i, j: (i, j)),
        out_shape=jax.ShapeDtypeStruct((T, D), BF16),
        compiler_params=_params(("parallel", "parallel"),
                                4 * tm * (D + dh + da) + 4 * tn * (2 * D + dh + da) + 4 * tm * tn
                                + 32 * tm * tn),
        name="gated_merge",
    )(hb, y_hy, y_att, w_in_bf, w_in_bf, w_hy, w_att)


def _oproj_ln_kernel(alpha, x_ref, w_ref, h_ref, g_ref, b_ref, of_ref, ob_ref):
    tm = x_ref.shape[0]
    ts = min(tm, EPILOGUE_SUBTILE)
    w = w_ref[...]
    for r0 in range(0, tm, ts):
        y = alpha * h_ref[r0:r0 + ts, :] + _dot(x_ref[r0:r0 + ts, :], w)
        y = _layer_norm_rows(y, g_ref[...], b_ref[...])
        of_ref[r0:r0 + ts, :] = y
        ob_ref[r0:r0 + ts, :] = y.astype(BF16)


def _oproj_ln(xb, w_o, h, g, b, layer, alpha, tm=512):
    T, D = h.shape
    row = lambda i: (i, 0)
    vec = pl.BlockSpec((None, 1, D), lambda i: (layer, 0, 0))
    return pl.pallas_call(
        functools.partial(_oproj_ln_kernel, alpha),
        grid=(T // tm,),
        in_specs=[pl.BlockSpec((tm, D), row),
                  pl.BlockSpec((None, D, D), lambda i: (layer, 0, 0)),
                  pl.BlockSpec((tm, D), row), vec, vec],
        out_specs=[pl.BlockSpec((tm, D), row), pl.BlockSpec((tm, D), row)],
        out_shape=[jax.ShapeDtypeStruct((T, D), F32), jax.ShapeDtypeStruct((T, D), BF16)],
        compiler_params=_params(("parallel",), 4 * D * D + 24 * tm * D + 24 * tm * D),
        name="oproj_resid_ln",
    )(xb, w_o, h, g, b)


def _swiglu_partial(x_ref, wg_ref, wu_ref, wd_ref):
    x = x_ref[...]
    g = _dot(x, wg_ref[...].astype(BF16))
    u = _dot(x, wu_ref[...].astype(BF16))
    a = (g * jax.nn.sigmoid(g) * u).astype(BF16)
    return _dot(a, wd_ref[...].astype(BF16))


def _ffn_dense_kernel(alpha, x_ref, wg_ref, wu_ref, wd_ref, h_ref, g_ref, b_ref,
                      of_ref, ob_ref, acc_ref):
    f = pl.program_id(1)

    @pl.when(f == 0)
    def _():
        acc_ref[...] = jnp.zeros_like(acc_ref)

    acc_ref[...] += _swiglu_partial(x_ref, wg_ref, wu_ref, wd_ref)

    @pl.when(f == pl.num_programs(1) - 1)
    def _():
        y = _layer_norm_rows(alpha * h_ref[...] + acc_ref[...], g_ref[...], b_ref[...])
        of_ref[...] = y
        ob_ref[...] = y.astype(BF16)


def _ffn_dense(xb, wg, wu, wd, h, g, b, j, layer, alpha, tm=512, tf=512):
    T, D = xb.shape
    F = wg.shape[-1]
    row = lambda i, f: (i, 0)
    vec = pl.BlockSpec((None, 1, D), lambda i, f: (layer, 0, 0))
    return pl.pallas_call(
        functools.partial(_ffn_dense_kernel, alpha),
        grid=(T // tm, F // tf),
        in_specs=[pl.BlockSpec((tm, D), row),
                  pl.BlockSpec((None, D, tf), lambda i, f: (j, 0, f)),
                  pl.BlockSpec((None, D, tf), lambda i, f: (j, 0, f)),
                  pl.BlockSpec((None, tf, D), lambda i, f: (j, f, 0)),
                  pl.BlockSpec((tm, D), row), vec, vec],
        out_specs=[pl.BlockSpec((tm, D), row), pl.BlockSpec((tm, D), row)],
        out_shape=[jax.ShapeDtypeStruct((T, D), F32), jax.ShapeDtypeStruct((T, D), BF16)],
        scratch_shapes=[pltpu.VMEM((tm, D), F32)],
        compiler_params=_params(("parallel", "arbitrary"),
                                4 * tm * D + 12 * D * tf + 8 * tm * D + 12 * tm * D + 4 * tm * D
                                + 16 * tm * tf + 8 * tm * D),
        name="ffn_dense_swiglu_ln",
    )(xb, wg, wu, wd, h, g, b)


def _ffn_group_kernel(blk0, be_ref, rows_ref, x_ref, wg_ref, wu_ref, wd_ref, *rest):
    o_ref = rest[-1]
    f = pl.program_id(1)
    n_rows = rows_ref[blk0 + pl.program_id(0)]
    tm = x_ref.shape[0]

    @pl.when(jnp.logical_and(n_rows > 0, f == 0))
    def _():
        o_ref[...] = jnp.zeros_like(o_ref)

    step = min(tm, MOE_ROW_STEP)
    for hi in range(step, tm + 1, step):
        @pl.when(jnp.logical_and(n_rows > hi - step, n_rows <= hi))
        def _(hi=hi):
            o_ref[0:hi, :] += _swiglu_partial(x_ref.at[0:hi], wg_ref, wu_ref, wd_ref)


def _ffn_grouped(blk_e, blk_rows, xg, wg, wu, wd, j, tm, blk0, nblk, prev=None, tf=256):
    D = xg.shape[1]
    F = wg.shape[-1]
    nf = F // tf
    nb = xg.shape[0] // tm

    def live(i, rows):
        return rows[blk0 + i] > 0

    def x_rows(i, f, be, nv):
        return (jnp.where(live(i, nv), i, 0), 0)

    def out_rows(i, f, be, nv):
        return (jnp.where(live(i, nv), blk0 + i, nblk), 0)

    def hidden(i, f, nv):
        return jnp.where(live(i, nv), f, nf - 1)

    in_specs = [pl.BlockSpec((tm, D), x_rows),
                pl.BlockSpec((None, None, D, tf),
                             lambda i, f, be, nv: (j, be[blk0 + i], 0, hidden(i, f, nv))),
                pl.BlockSpec((None, None, D, tf),
                             lambda i, f, be, nv: (j, be[blk0 + i], 0, hidden(i, f, nv))),
                pl.BlockSpec((None, None, tf, D),
                             lambda i, f, be, nv: (j, be[blk0 + i], hidden(i, f, nv), 0))]
    args = [blk_e, blk_rows, xg, wg, wu, wd]
    aliases = {}
    if prev is not None:
        in_specs.append(pl.BlockSpec(memory_space=pl.ANY))
        aliases = {len(args): 0}
        args.append(prev)
    return pl.pallas_call(
        functools.partial(_ffn_group_kernel, blk0),
        grid_spec=pltpu.PrefetchScalarGridSpec(
            num_scalar_prefetch=2,
            grid=(nb, nf),
            in_specs=in_specs,
            out_specs=pl.BlockSpec((tm, D), out_rows)),
        out_shape=jax.ShapeDtypeStruct(((nblk + 1) * tm, D), F32),
        input_output_aliases=aliases,
        compiler_params=_params(("arbitrary", "arbitrary"),
                                4 * tm * D + 24 * D * tf + 8 * tm * D
                                + 6 * D * tf + 16 * tm * tf + 8 * tm * D),
        name="moe_grouped_swiglu",
    )(*args)


def _router_kernel(x_ref, wh_ref, wl_ref, b_ref, tri_ref, e_ref, w_ref, r_ref, cnt_ref):
    @pl.when(pl.program_id(0) == 0)
    def _():
        cnt_ref[...] = jnp.zeros_like(cnt_ref)

    x_hi, x_lo = _split_bf16(x_ref[...])
    nt = (((1,), (1,)), ((), ()))

    def dg(a, b):
        return lax.dot_general(a, b, nt, preferred_element_type=F32)

    wh = wh_ref[...]
    wl = wl_ref[...]
    logits = dg(wh, x_hi) + (dg(wh, x_lo) + dg(wl, x_hi)) + b_ref[...]
    n_e = logits.shape[0]
    eid = lax.broadcasted_iota(jnp.int32, logits.shape, 0)
    m1 = jnp.max(logits, axis=0, keepdims=True)
    i1 = jnp.min(jnp.where(logits == m1, eid, n_e), axis=0, keepdims=True)
    rest = jnp.where(eid == i1, -jnp.inf, logits)
    m2 = jnp.max(rest, axis=0, keepdims=True)
    i2 = jnp.min(jnp.where(rest == m2, eid, n_e), axis=0, keepdims=True)
    t = jnp.exp(m2 - m1)
    w1 = 1.0 / (1.0 + t)
    e_ref[...] = jnp.concatenate([i1, i2], axis=0)
    w_ref[...] = jnp.concatenate([w1, t * w1], axis=0)

    tri = tri_ref[...]
    oh1 = (eid == i1).astype(F32)
    oh2 = (eid == i2).astype(F32)
    tot1 = jnp.sum(oh1, axis=1, keepdims=True)
    tot2 = jnp.sum(oh2, axis=1, keepdims=True)
    base = cnt_ref[...][:, 0:1]
    c1 = base + _dot(oh1.astype(BF16), tri)
    c2 = base + tot1 + _dot(oh2.astype(BF16), tri)
    r1 = jnp.sum(oh1 * c1, axis=0, keepdims=True)
    r2 = jnp.sum(oh2 * c2, axis=0, keepdims=True)
    r_ref[...] = jnp.concatenate([r1, r2], axis=0).astype(jnp.int32)
    cnt_ref[...] = cnt_ref[...] + (tot1 + tot2)


def _router(h, rw_hi, rw_lo, rb, tm=512):
    T, D = h.shape
    E = rw_hi.shape[0]
    const = lambda i: (0, 0)
    out = pl.BlockSpec((TOP_K, tm), lambda i: (0, i))
    idx = jnp.arange(tm)
    tri = (idx[:, None] < idx[None, :]).astype(BF16)
    return pl.pallas_call(
        _router_kernel,
        grid=(T // tm,),
        in_specs=[pl.BlockSpec((tm, D), lambda i: (i, 0)), pl.BlockSpec((E, D), const),
                  pl.BlockSpec((E, D), const), pl.BlockSpec((E, 1), const),
                  pl.BlockSpec((tm, tm), const)],
        out_specs=[out, out, out, pl.BlockSpec((E, V7X_LANES), const)],
        out_shape=[jax.ShapeDtypeStruct((TOP_K, T), jnp.int32),
                   jax.ShapeDtypeStruct((TOP_K, T), F32),
                   jax.ShapeDtypeStruct((TOP_K, T), jnp.int32),
                   jax.ShapeDtypeStruct((E, V7X_LANES), F32)],
        compiler_params=_params(("arbitrary",), 32 * 1024 * 1024),
        name="moe_router_top2",
    )(h, rw_hi, rw_lo, rb, tri)


def _moe_combine_ln_kernel(alpha, h_ref, ya_ref, yb_ref, w_ref, g_ref, b_ref, of_ref, ob_ref):
    w = w_ref[...]
    f = ya_ref[...] * w[:, 0:1] + yb_ref[...] * w[:, 1:2]
    y = _layer_norm_rows(alpha * h_ref[...] + f, g_ref[...], b_ref[...])
    of_ref[...] = y
    ob_ref[...] = y.astype(BF16)


def _moe_combine_ln(h, ya, yb, w, g, b, alpha, tm=256):
    T, D = h.shape
    row = pl.BlockSpec((tm, D), lambda i: (i, 0))
    vec = pl.BlockSpec((1, D), lambda i: (0, 0))
    return pl.pallas_call(
        functools.partial(_moe_combine_ln_kernel, alpha),
        grid=(T // tm,),
        in_specs=[row, row, row, pl.BlockSpec((tm, w.shape[1]), lambda i: (i, 0)), vec, vec],
        out_specs=[row, row],
        out_shape=[jax.ShapeDtypeStruct((T, D), F32), jax.ShapeDtypeStruct((T, D), BF16)],
        compiler_params=_params(("parallel",), 64 * tm * D),
        name="moe_combine_ln",
    )(h, ya, yb, w, g.reshape(1, D), b.reshape(1, D))


def _dft_tables(L):
    n2 = 2 * L
    k = jnp.arange(L, dtype=jnp.int32)[:, None]
    n = jnp.arange(L, dtype=jnp.int32)[None, :]
    ang = (((2 * k + 1) * (2 * n + 1)) % (4 * n2)).astype(F32) * (math.pi / (2 * n2))
    phi = (2 * k + 1).astype(F32) * (math.pi / (2 * n2))
    return jnp.cos(ang).astype(BF16), jnp.sin(ang).astype(BF16), jnp.cos(phi), jnp.sin(phi)


def _rope_tables(L, hd):
    rows = L // GRID_W
    row = jnp.repeat(jnp.arange(rows, dtype=F32), GRID_W)
    col = jnp.tile(jnp.arange(GRID_W, dtype=F32), rows)
    n_pairs = hd // 4
    inv = ROPE_THETA ** (-jnp.arange(n_pairs, dtype=F32) / n_pairs)
    ang = jnp.concatenate([row[:, None] * inv, col[:, None] * inv], axis=-1)
    cos, sin = jnp.cos(ang), jnp.sin(ang)
    reps = V7X_LANES // hd
    return (jnp.tile(jnp.concatenate([cos, cos], axis=-1), (1, reps)),
            jnp.tile(jnp.concatenate([-sin, sin], axis=-1), (1, reps)))


def _filter_features(L, n_emb):
    t = jnp.linspace(0.0, 1.0, L, dtype=F32)[:, None]
    bands = (n_emb - 1) // 2
    w = 2.0 * math.pi * jnp.arange(L, dtype=F32)[:, None] / L
    f = jnp.linspace(1e-4, bands - 1, bands, dtype=F32)[None, :]
    z = jnp.concatenate([t, jnp.cos(f * w), -jnp.sin(f * w)], axis=-1)
    return jnp.pad(z, ((0, 0), (0, V7X_LANES - n_emb)))


def _moe_block_rows(T):
    return min(1024, max(V7X_MXU_DIM, T // 4))


def _moe(hf, hb, router_w, router_b, wg, wu, wd, j):
    T, D = hf.shape
    E = router_w.shape[1]
    rw_hi, rw_lo = _split_bf16(router_w.T)
    top_e, top_w, rank, cnt = _router(hf, rw_hi, rw_lo, router_b.reshape(E, 1))

    tm = _moe_block_rows(T)
    counts = cnt[:, 0].astype(jnp.int32)
    padded = (counts + tm - 1) // tm * tm
    pad_end = jnp.cumsum(padded)
    pad_start = pad_end - padded
    nblk = -(-(T * TOP_K + E * (tm - 1)) // tm)
    P = nblk * tm
    n_valid = (pad_end[-1] // tm).astype(jnp.int32)
    blk_first = jnp.arange(nblk, dtype=jnp.int32) * tm
    blk_e = jnp.sum((blk_first[:, None] >= pad_end[None, :]).astype(jnp.int32), axis=1)
    blk_e = jnp.minimum(blk_e, E - 1)
    row_end = (pad_start + counts)[blk_e]
    blk_rows = jnp.where(jnp.arange(nblk) < n_valid, jnp.clip(row_end - blk_first, 0, tm), 0)
    blk_e = jnp.where(jnp.arange(nblk) < n_valid, blk_e, blk_e[jnp.maximum(n_valid - 1, 0)])
    start_of = jnp.sum(jnp.where(top_e[:, :, None] == jnp.arange(E, dtype=jnp.int32),
                                 pad_start[None, None, :], 0), axis=-1)
    dest = start_of + rank

    tok = jnp.broadcast_to(jnp.arange(T, dtype=jnp.int32), (TOP_K, T))
    src_tok = jnp.zeros((P,), jnp.int32).at[dest.reshape(-1)].set(tok.reshape(-1))
    yg = None
    blk_rows = blk_rows.astype(jnp.int32)
    for blk0 in range(0, nblk, -(-nblk // MOE_CALLS)):
        nb = min(-(-nblk // MOE_CALLS), nblk - blk0)
        xg = hb[lax.slice_in_dim(src_tok, blk0 * tm, (blk0 + nb) * tm)]
        yg = _ffn_grouped(blk_e, blk_rows, xg, wg, wu, wd, j, tm, blk0, nblk, prev=yg)
    return yg[dest[0]], yg[dest[1]], top_w.T


def kernel(x, ln_in_g, ln_in_b, w_in, hy_conv_w, hy_conv_b, hy_fw1, hy_fb1, hy_fw2, hy_fb2, hy_fw3, hy_freq, hy_bias, q_norm_g, k_norm_g, w_hy_br, w_att_br, w_o, ln_mix_g, ln_mix_b, ffn_w_gate, ffn_w_up, ffn_w_down, router_w, router_b, exp_w_gate, exp_w_up, exp_w_down, ln_ffn_g, ln_ffn_b):
    B, L, D = x.shape
    T = B * L
    depth = w_in.shape[0]
    dh = w_hy_br.shape[1]
    d_attn = w_att_br.shape[1]
    hd = q_norm_g.shape[-1]
    n_q = d_attn // hd
    d_kv = (w_in.shape[2] - 3 * dh - d_attn - 2 * D) // 2
    n_kv = d_kv // hd
    alpha = (2 * depth) ** 0.25
    qkv0 = 3 * dh
    gate0 = qkv0 + d_attn + 2 * d_kv
    n_rot = (n_q + n_kv) * hd
    assert L % GRID_W == 0 and V7X_LANES % hd == 0 and n_rot % V7X_MXU_DIM == 0

    cs_bf, ss_bf, cos_phi, sin_phi = _dft_tables(L)
    cos_t, sin_t = _rope_tables(L, hd)
    z = _filter_features(L, hy_fw1.shape[1])
    min_decay = math.log(DECAY_TARGET) / SLOW_DECAY_PCT
    max_decay = math.log(DECAY_TARGET) / FAST_DECAY_PCT
    deltas = jnp.abs(jnp.linspace(min_decay, max_decay, dh, dtype=F32)).reshape(1, dh)
    lane = jnp.arange(V7X_MXU_DIM)
    bd = jnp.where((lane[:, None] // hd) == (lane[None, :] // hd), 1.0 / hd, 0.0).astype(BF16)
    qk_scale = jnp.concatenate([jnp.full((n_q * hd,), hd ** -0.5, F32),
                                jnp.ones((n_kv * hd,), F32)]).reshape(1, n_rot)

    w_in_bf = w_in.astype(BF16)
    w_hy_bf = w_hy_br.astype(BF16)
    w_att_bf = w_att_br.astype(BF16)
    w_o_bf = w_o.astype(BF16)
    ffn_g_bf = ffn_w_gate.astype(BF16)
    ffn_u_bf = ffn_w_up.astype(BF16)
    ffn_d_bf = ffn_w_down.astype(BF16)
    conv_b3 = hy_conv_b.reshape(depth, 1, 3 * dh)
    ln_mix_g3, ln_mix_b3 = ln_mix_g.reshape(depth, 1, D), ln_mix_b.reshape(depth, 1, D)
    ln_ffn_g3, ln_ffn_b3 = ln_ffn_g.reshape(depth, 1, D), ln_ffn_b.reshape(depth, 1, D)

    hf, hb = _resid_ln(x.reshape(T, D), [], ln_in_g, ln_in_b, alpha)
    for i in range(depth):
        x0, vv = _hy_proj(hb.reshape(B, L, D), w_in_bf, hy_conv_w, conv_b3, i, dh)
        fw1p = jnp.pad(hy_fw1[i], ((0, V7X_LANES - hy_fw1.shape[1]), (0, 0)))
        hs, hdf = _filters(z, fw1p, hy_fb1[i].reshape(1, -1), hy_fw2[i], hy_fb2[i].reshape(1, -1),
                           hy_freq[i], hy_fw3[i], deltas, dh)
        kre, kim = _spectrum(cs_bf, ss_bf, cos_phi, sin_phi, hs, hdf)
        y_hy = _longconv(cs_bf, ss_bf, vv, x0, kre, kim, hy_bias[i].reshape(1, dh))
        gain = jnp.concatenate([jnp.tile(q_norm_g[i], n_q), jnp.tile(k_norm_g[i], n_kv)]).reshape(1, n_rot)
        w_qkv = lax.slice_in_dim(w_in_bf[i], qkv0, gate0, axis=1)
        q, k, v = _qkv_proj(hb, w_qkv, bd, gain, qk_scale, cos_t, sin_t, B, L, n_q, n_kv, hd)
        y_att = _attention(q, k, v)
        merged = _merge(hb, y_hy.reshape(T, dh), y_att.reshape(T, d_attn), w_in_bf, w_hy_bf, w_att_bf,
                        i, gate0)
        hf, hb = _oproj_ln(merged, w_o_bf, hf, ln_mix_g3, ln_mix_b3, i, alpha)
        j = i // 2
        if i % 2 == 0:
            hf, hb = _ffn_dense(hb, ffn_g_bf, ffn_u_bf, ffn_d_bf, hf, ln_ffn_g3, ln_ffn_b3, j, i, alpha)
        else:
            ya, yb, wts = _moe(hf, hb, router_w[j], router_b[j], exp_w_gate, exp_w_up, exp_w_down, j)
            hf, hb = _moe_combine_ln(hf, ya, yb, wts, ln_ffn_g[i], ln_ffn_b[i], alpha)
    return hf.reshape(B, L, D)
```

```python
import functools
import math

import jax
import jax.numpy as jnp
from jax import lax
from jax.experimental import pallas as pl
from jax.experimental.pallas import tpu as pltpu

F32 = jnp.float32
BF16 = jnp.bfloat16

GRID_W = 64
ROPE_THETA = 10000.0
TOP_K = 2
DECAY_TARGET = 1e-2
FAST_DECAY_PCT = 0.3
SLOW_DECAY_PCT = 1.5
LN_EPS = 1e-5
RMS_EPS = 1e-6

V7X_VMEM_BYTES = 64 * 1024 * 1024
V7X_LANES = 128
V7X_MXU_DIM = 256
VMEM_BUDGET = V7X_VMEM_BYTES - 8 * 1024 * 1024
ATTN_SUBTILE = 512
EPILOGUE_SUBTILE = V7X_MXU_DIM
CONV_SUBTILE = 512
CONV_HALO = 16
MOE_CALLS = 4
MOE_ROW_STEP = V7X_MXU_DIM
COMBINE_CALLS = 2


def _params(semantics, vmem_bytes):
    return pltpu.CompilerParams(dimension_semantics=semantics,
                                vmem_limit_bytes=min(int(vmem_bytes), VMEM_BUDGET))


def _dot(a, b):
    return jnp.dot(a, b, preferred_element_type=F32)


def _split_bf16(a):
    hi = a.astype(BF16)
    lo = (a - hi.astype(F32)).astype(BF16)
    return hi, lo


def _dot3(a_hi, a_lo, b_hi, b_lo):
    return _dot(a_hi, b_hi) + (_dot(a_hi, b_lo) + _dot(a_lo, b_hi))


def _layer_norm_rows(x, g, b):
    mu = jnp.mean(x, axis=-1, keepdims=True)
    xc = x - mu
    var = jnp.mean(xc * xc, axis=-1, keepdims=True)
    return xc * lax.rsqrt(var + LN_EPS) * g + b


def _resid_ln_kernel(alpha, n_add, *refs):
    h_ref = refs[0]
    add_refs = refs[1:1 + n_add]
    g_ref, b_ref, of_ref, ob_ref = refs[1 + n_add:]
    x = h_ref[...]
    if n_add:
        acc = add_refs[0][...]
        for r in add_refs[1:]:
            acc = acc + r[...]
        x = alpha * x + acc
    y = _layer_norm_rows(x, g_ref[...], b_ref[...])
    of_ref[...] = y
    ob_ref[...] = y.astype(BF16)


def _resid_ln(h, addends, g, b, alpha, tm=256):
    T, D = h.shape
    n_add = len(addends)
    row = pl.BlockSpec((tm, D), lambda i: (i, 0))
    vec = pl.BlockSpec((1, D), lambda i: (0, 0))
    return pl.pallas_call(
        functools.partial(_resid_ln_kernel, alpha, n_add),
        grid=(T // tm,),
        in_specs=[row] * (1 + n_add) + [vec, vec],
        out_specs=[row, row],
        out_shape=[jax.ShapeDtypeStruct((T, D), F32), jax.ShapeDtypeStruct((T, D), BF16)],
        compiler_params=_params(("parallel",), (2 * (1 + n_add) * 4 + 12 + 16) * tm * D),
        name="resid_ln",
    )(h, *addends, g.reshape(1, D), b.reshape(1, D))


def _hy_proj_kernel(h_ref, w0_ref, w1_ref, w2_ref, cw0_ref, cw1_ref, cw2_ref,
                    cb0_ref, cb1_ref, cb2_ref, x0_ref, vv_ref):
    L = h_ref.shape[0]
    ts = min(L, CONV_SUBTILE)
    halo = CONV_HALO if ts < L else 0
    for r0 in range(0, L, ts):
        lo, hi = max(r0 - halo, 0), min(r0 + ts + halo, L)
        n = hi - lo
        h = h_ref[lo:hi, :]
        row = lo + lax.broadcasted_iota(jnp.int32, (n, 1), 0)

        def stream(w_ref, cw_ref, cb_ref):
            p = _dot(h, w_ref[...])
            prev = jnp.where(row == 0, 0.0, pltpu.roll(p, 1, 0))
            nxt = jnp.where(row == L - 1, 0.0, pltpu.roll(p, n - 1, 0))
            cw = cw_ref[...]
            u = prev * cw[0:1] + p * cw[1:2] + nxt * cw[2:3] + cb_ref[...]
            return u[r0 - lo:r0 - lo + ts, :]

        x0_ref[r0:r0 + ts, :] = stream(w0_ref, cw0_ref, cb0_ref)
        vv_ref[r0:r0 + ts, :] = (stream(w1_ref, cw1_ref, cb1_ref)
                                 * stream(w2_ref, cw2_ref, cb2_ref))


def _hy_proj(hb3, w_in_bf, conv_w, conv_b3, layer, dh, tn=256):
    B, L, D = hb3.shape
    nj = dh // tn

    def wspec(s):
        return pl.BlockSpec((None, D, tn), lambda j, b: (layer, 0, s * nj + j))

    def cwspec(s):
        return pl.BlockSpec((None, conv_w.shape[1], tn), lambda j, b: (layer, 0, s * nj + j))

    def cbspec(s):
        return pl.BlockSpec((None, 1, tn), lambda j, b: (layer, 0, s * nj + j))

    out = pl.BlockSpec((None, L, tn), lambda j, b: (b, 0, j))
    return pl.pallas_call(
        _hy_proj_kernel,
        grid=(nj, B),
        in_specs=[pl.BlockSpec((None, L, D), lambda j, b: (b, 0, 0)),
                  wspec(0), wspec(1), wspec(2), cwspec(0), cwspec(1), cwspec(2),
                  cbspec(0), cbspec(1), cbspec(2)],
        out_specs=[out, out],
        out_shape=[jax.ShapeDtypeStruct((B, L, dh), F32)] * 2,
        compiler_params=_params(("parallel", "parallel"),
                                4 * L * D + 12 * D * tn + 16 * L * tn + 40 * L * tn),
        name="hyena_proj_conv",
    )(hb3, w_in_bf, w_in_bf, w_in_bf, conv_w, conv_w, conv_w, conv_b3, conv_b3, conv_b3)


def _qkv_kernel(n_q, n_kv, hd, x_ref, w_ref, bd_ref, gain_ref, scale_ref, cos_ref, sin_ref,
                q_ref, k_ref, v_ref):
    n_rot = (n_q + n_kv) * hd
    tm = x_ref.shape[0]
    ts = min(tm, EPILOGUE_SUBTILE)
    w = w_ref[...]
    bd = bd_ref[...]
    gw = bd.shape[0]
    reps = n_rot // V7X_LANES
    lane = lax.broadcasted_iota(jnp.int32, (1, n_rot), 1)
    first_half = (lane % hd) < (hd // 2)
    half = hd // 2
    pad_lane = lax.broadcasted_iota(jnp.int32, (ts, V7X_LANES - hd), 1)
    ones_col = jnp.where(pad_lane == 0, 1.0, 0.0).astype(BF16)
    for r0 in range(0, tm, ts):
        rows = slice(r0, r0 + ts)
        p = _dot(x_ref[rows, :], w)
        pr = p[:, :n_rot]
        sq = (pr * pr).astype(BF16)
        ms = jnp.concatenate([_dot(sq[:, c:c + gw], bd) for c in range(0, n_rot, gw)], axis=1)
        xn = pr * lax.rsqrt(ms + RMS_EPS) * gain_ref[...]
        cos = jnp.concatenate([cos_ref[rows, :]] * reps, axis=1)
        sin = jnp.concatenate([sin_ref[rows, :]] * reps, axis=1)
        swapped = jnp.where(first_half, pltpu.roll(xn, n_rot - half, 1), pltpu.roll(xn, half, 1))
        rot = ((xn * cos + swapped * sin) * scale_ref[...]).astype(BF16)
        for hh in range(n_q):
            q_ref[hh, rows, :] = rot[:, hh * hd:(hh + 1) * hd]
        for hh in range(n_kv):
            k_ref[hh, rows, :] = rot[:, (n_q + hh) * hd:(n_q + hh + 1) * hd]
            v_h = p[:, n_rot + hh * hd:n_rot + (hh + 1) * hd].astype(BF16)
            v_ref[hh, rows, :] = jnp.concatenate([v_h, ones_col], axis=1)


def _qkv_proj(hb, w_qkv, bd, gain, scale, cos_t, sin_t, B, L, n_q, n_kv, hd, tm=512):
    T, D = hb.shape
    nw = w_qkv.shape[1]
    n_rot = (n_q + n_kv) * hd
    tm = min(tm, L)
    tpb = L // tm
    const = lambda i: (0, 0)
    return pl.pallas_call(
        functools.partial(_qkv_kernel, n_q, n_kv, hd),
        grid=(T // tm,),
        in_specs=[pl.BlockSpec((tm, D), lambda i: (i, 0)),
                  pl.BlockSpec((D, nw), const),
                  pl.BlockSpec(bd.shape, const),
                  pl.BlockSpec((1, n_rot), const),
                  pl.BlockSpec((1, n_rot), const),
                  pl.BlockSpec((tm, V7X_LANES), lambda i: (i % tpb, 0)),
                  pl.BlockSpec((tm, V7X_LANES), lambda i: (i % tpb, 0))],
        out_specs=[pl.BlockSpec((None, n_q, tm, hd), lambda i: (i // tpb, 0, i % tpb, 0)),
                   pl.BlockSpec((None, n_kv, tm, hd), lambda i: (i // tpb, 0, i % tpb, 0)),
                   pl.BlockSpec((None, n_kv, tm, V7X_LANES), lambda i: (i // tpb, 0, i % tpb, 0))],
        out_shape=[jax.ShapeDtypeStruct((B, n_q, L, hd), BF16),
                   jax.ShapeDtypeStruct((B, n_kv, L, hd), BF16),
                   jax.ShapeDtypeStruct((B, n_kv, L, V7X_LANES), BF16)],
        compiler_params=_params(("parallel",),
                                4 * tm * D + 4 * D * nw + 8 * tm * (n_q + 2 * n_kv) * V7X_LANES
                                + 48 * tm * nw),
        name="qkv_proj_norm_rope",
    )(hb, w_qkv, bd, gain, scale, cos_t, sin_t)


def _attn_kernel(group, hd, q_ref, k_ref, v_ref, o_ref):
    k = k_ref[...]
    v = v_ref[...]
    tq = q_ref.shape[1]
    ts = min(tq, ATTN_SUBTILE)
    for r0 in range(0, tq, ts):
        for g in range(group):
            s = lax.dot_general(q_ref[g, r0:r0 + ts, :], k, (((1,), (1,)), ((), ())),
                                preferred_element_type=F32)
            m = jnp.max(s, axis=-1, keepdims=True)
            e = jnp.exp(s - m).astype(BF16)
            o = _dot(e, v)
            o_ref[r0:r0 + ts, g * hd:(g + 1) * hd] = (o[:, :hd] / o[:, hd:hd + 1]).astype(o_ref.dtype)


def _attention(q, k, v, tq=2048):
    B, n_q, L, hd = q.shape
    n_kv = k.shape[1]
    group = n_q // n_kv
    tq = min(tq, L)
    return pl.pallas_call(
        functools.partial(_attn_kernel, group, hd),
        grid=(B, n_kv, L // tq),
        in_specs=[pl.BlockSpec((None, group, tq, hd), lambda b, j, t: (b, j, t, 0)),
                  pl.BlockSpec((None, None, L, hd), lambda b, j, t: (b, j, 0, 0)),
                  pl.BlockSpec((None, None, L, v.shape[-1]), lambda b, j, t: (b, j, 0, 0))],
        out_specs=pl.BlockSpec((None, tq, group * hd), lambda b, j, t: (b, t, j)),
        out_shape=jax.ShapeDtypeStruct((B, L, n_q * hd), BF16),
        compiler_params=_params(("parallel", "parallel", "parallel"),
                                8 * L * V7X_LANES + 8 * group * tq * V7X_LANES + 40 * tq * L),
        name="gqa_attention",
    )(q, k, v)


def _filter_kernel(z_ref, fw1_ref, fb1_ref, fw2_ref, fb2_ref, fr_ref, w3f_ref, w3b_ref, dl_ref,
                   hs_ref, hd_ref):
    z = z_ref[...]
    L = z.shape[0]
    fr = fr_ref[...]

    def hp_dot(a, b):
        a_hi, a_lo = _split_bf16(a)
        b_hi, b_lo = _split_bf16(b)
        return _dot3(a_hi, a_lo, b_hi, b_lo)

    a = jnp.sin(fr[0:1] * (hp_dot(z, fw1_ref[...]) + fb1_ref[...]))
    a = jnp.sin(fr[1:2] * (hp_dot(a, fw2_ref[...]) + fb2_ref[...]))
    window = jnp.exp(-z[:, 0:1] * dl_ref[...])
    h_fwd = hp_dot(a, w3f_ref[...]) * window
    h_bwd = hp_dot(a, w3b_ref[...]) * window
    row = lax.broadcasted_iota(jnp.int32, (L, 1), 0)
    h_bwd0 = jnp.where(row == 0, 0.0, h_bwd)
    hs_ref[...] = h_fwd + h_bwd0
    hd_ref[...] = h_bwd0 - h_fwd


def _filters(z, fw1p, fb1, fw2, fb2, fr, fw3, deltas, dh, tn=256):
    L, zw = z.shape
    nh = fw2.shape[0]
    nj = dh // tn
    const = lambda j: (0, 0)
    out = pl.BlockSpec((L, tn), lambda j: (0, j))
    return pl.pallas_call(
        _filter_kernel,
        grid=(nj,),
        in_specs=[pl.BlockSpec((L, zw), const), pl.BlockSpec((zw, nh), const),
                  pl.BlockSpec((1, nh), const), pl.BlockSpec((nh, nh), const),
                  pl.BlockSpec((1, nh), const), pl.BlockSpec((2, nh), const),
                  pl.BlockSpec((nh, tn), lambda j: (0, j)),
                  pl.BlockSpec((nh, tn), lambda j: (0, nj + j)),
                  pl.BlockSpec((1, tn), lambda j: (0, j))],
        out_specs=[out, out],
        out_shape=[jax.ShapeDtypeStruct((L, dh), F32)] * 2,
        compiler_params=_params(("parallel",), 32 * 1024 * 1024),
        name="hyena_filters",
    )(z, fw1p, fb1, fw2, fb2, fr, fw3, fw3, deltas)


def _spectrum_kernel(cs_ref, ss_ref, cp_ref, sp_ref, hs_ref, hd_ref, kre_ref, kim_ref):
    cs = cs_ref[...]
    ss = ss_ref[...]
    hs = hs_ref[...].astype(BF16)
    hd = hd_ref[...].astype(BF16)
    cp = cp_ref[...]
    sp = sp_ref[...]
    kre_ref[...] = cp * _dot(cs, hs) + sp * _dot(ss, hs)
    kim_ref[...] = cp * _dot(ss, hd) - sp * _dot(cs, hd)


def _spectrum(cs, ss, cos_phi, sin_phi, hs, hd, tk=512, tn=256):
    L, dh = hs.shape
    tk = min(tk, L)
    mat = pl.BlockSpec((tk, L), lambda j, i: (i, 0))
    vec = pl.BlockSpec((tk, 1), lambda j, i: (i, 0))
    col = pl.BlockSpec((L, tn), lambda j, i: (0, j))
    out = pl.BlockSpec((tk, tn), lambda j, i: (i, j))
    return pl.pallas_call(
        _spectrum_kernel,
        grid=(dh // tn, L // tk),
        in_specs=[mat, mat, vec, vec, col, col],
        out_specs=[out, out],
        out_shape=[jax.ShapeDtypeStruct((L, dh), F32)] * 2,
        compiler_params=_params(("parallel", "parallel"), 40 * 1024 * 1024),
        name="filter_spectrum",
    )(cs, ss, cos_phi, sin_phi, hs, hd)


def _longconv_kernel(inv_scale, cs_ref, ss_ref, v_ref, x0_ref, kre_ref, kim_ref, bias_ref, y_ref):
    cs = cs_ref[...]
    ss = ss_ref[...]
    v = v_ref[...]
    vb = v.astype(BF16)
    xc = _dot(cs, vb)
    xs = _dot(ss, vb)
    kre = kre_ref[...]
    kim = kim_ref[...]
    yre = (xc * kre + xs * kim).astype(BF16)
    yim = (xc * kim - xs * kre).astype(BF16)
    conv = (_dot(cs, yre) - _dot(ss, yim)) * inv_scale
    y_ref[...] = (x0_ref[...] * (conv + bias_ref[...] * v)).astype(y_ref.dtype)


def _longconv(cs, ss, vv, x0, kre, kim, bias, tc=256):
    B, L, dh = vv.shape
    mat = pl.BlockSpec((L, L), lambda j, b: (0, 0), pipeline_mode=pl.Buffered(1))
    act = pl.BlockSpec((None, L, tc), lambda j, b: (b, 0, j))
    spec = pl.BlockSpec((L, tc), lambda j, b: (0, j))
    return pl.pallas_call(
        functools.partial(_longconv_kernel, 1.0 / L),
        grid=(dh // tc, B),
        in_specs=[mat, mat, act, act, spec, spec, pl.BlockSpec((1, tc), lambda j, b: (0, j))],
        out_specs=act,
        out_shape=jax.ShapeDtypeStruct((B, L, dh), BF16),
        compiler_params=_params(("parallel", "parallel"),
                                4 * L * L + 16 * L * tc + 16 * L * tc + 4 * L * tc + 40 * L * tc),
        name="hyena_longconv",
    )(cs, ss, vv, x0, kre, kim, bias)


def _merge_kernel(h_ref, yh_ref, ya_ref, wgh_ref, wga_ref, wh_ref, wa_ref, o_ref):
    h = h_ref[...]
    gh = jax.nn.sigmoid(_dot(h, wgh_ref[...]))
    ga = jax.nn.sigmoid(_dot(h, wga_ref[...]))
    o_ref[...] = (gh * _dot(yh_ref[...], wh_ref[...])
                  + ga * _dot(ya_ref[...], wa_ref[...])).astype(o_ref.dtype)


def _merge(hb, y_hy, y_att, w_in_bf, w_hy, w_att, layer, gate_col0, tm=1024, tn=512):
    T, D = hb.shape
    dh, da = y_hy.shape[1], y_att.shape[1]
    tm = min(tm, T)
    g0 = gate_col0 // tn
    nj = D // tn
    return pl.pallas_call(
        _merge_kernel,
        grid=(T // tm, nj),
        in_specs=[pl.BlockSpec((tm, D), lambda i, j: (i, 0)),
                  pl.BlockSpec((tm, dh), lambda i, j: (i, 0)),
                  pl.BlockSpec((tm, da), lambda i, j: (i, 0)),
                  pl.BlockSpec((None, D, tn), lambda i, j: (layer, 0, g0 + j)),
                  pl.BlockSpec((None, D, tn), lambda i, j: (layer, 0, g0 + nj + j)),
                  pl.BlockSpec((None, dh, tn), lambda i, j: (layer, 0, j)),
                  pl.BlockSpec((None, da, tn), lambda i, j: (layer, 0, j))],
        out_specs=pl.BlockSpec((tm, tn), lambda i, j: (i, j)),
        out_shape=jax.ShapeDtypeStruct((T, D), BF16),
        compiler_params=_params(("parallel", "parallel"),
                                4 * tm * (D + dh + da) + 4 * tn * (2 * D + dh + da) + 4 * tm * tn
                                + 32 * tm * tn),
        name="gated_merge",
    )(hb, y_hy, y_att, w_in_bf, w_in_bf, w_hy, w_att)


def _oproj_ln_kernel(alpha, x_ref, w_ref, h_ref, g_ref, b_ref, of_ref, ob_ref):
    tm = x_ref.shape[0]
    ts = min(tm, EPILOGUE_SUBTILE)
    w = w_ref[...]
    for r0 in range(0, tm, ts):
        y = alpha * h_ref[r0:r0 + ts, :] + _dot(x_ref[r0:r0 + ts, :], w)
        y = _layer_norm_rows(y, g_ref[...], b_ref[...])
        of_ref[r0:r0 + ts, :] = y
        ob_ref[r0:r0 + ts, :] = y.astype(BF16)


def _oproj_ln(xb, w_o, h, g, b, layer, alpha, tm=512):
    T, D = h.shape
    row = lambda i: (i, 0)
    vec = pl.BlockSpec((None, 1, D), lambda i: (layer, 0, 0))
    return pl.pallas_call(
        functools.partial(_oproj_ln_kernel, alpha),
        grid=(T // tm,),
        in_specs=[pl.BlockSpec((tm, D), row),
                  pl.BlockSpec((None, D, D), lambda i: (layer, 0, 0)),
                  pl.BlockSpec((tm, D), row), vec, vec],
        out_specs=[pl.BlockSpec((tm, D), row), pl.BlockSpec((tm, D), row)],
        out_shape=[jax.ShapeDtypeStruct((T, D), F32), jax.ShapeDtypeStruct((T, D), BF16)],
        compiler_params=_params(("parallel",), 4 * D * D + 24 * tm * D + 24 * tm * D),
        name="oproj_resid_ln",
    )(xb, w_o, h, g, b)


def _swiglu_partial(x_ref, wg_ref, wu_ref, wd_ref):
    x = x_ref[...]
    g = _dot(x, wg_ref[...].astype(BF16))
    u = _dot(x, wu_ref[...].astype(BF16))
    a = (g * jax.nn.sigmoid(g) * u).astype(BF16)
    return _dot(a, wd_ref[...].astype(BF16))


def _ffn_dense_kernel(alpha, x_ref, wg_ref, wu_ref, wd_ref, h_ref, g_ref, b_ref,
                      of_ref, ob_ref, acc_ref):
    f = pl.program_id(1)

    @pl.when(f == 0)
    def _():
        acc_ref[...] = jnp.zeros_like(acc_ref)

    acc_ref[...] += _swiglu_partial(x_ref, wg_ref, wu_ref, wd_ref)

    @pl.when(f == pl.num_programs(1) - 1)
    def _():
        y = _layer_norm_rows(alpha * h_ref[...] + acc_ref[...], g_ref[...], b_ref[...])
        of_ref[...] = y
        ob_ref[...] = y.astype(BF16)


def _ffn_dense(xb, wg, wu, wd, h, g, b, j, layer, alpha, tm=512, tf=512):
    T, D = xb.shape
    F = wg.shape[-1]
    row = lambda i, f: (i, 0)
    vec = pl.BlockSpec((None, 1, D), lambda i, f: (layer, 0, 0))
    return pl.pallas_call(
        functools.partial(_ffn_dense_kernel, alpha),
        grid=(T // tm, F // tf),
        in_specs=[pl.BlockSpec((tm, D), row),
                  pl.BlockSpec((None, D, tf), lambda i, f: (j, 0, f)),
                  pl.BlockSpec((None, D, tf), lambda i, f: (j, 0, f)),
                  pl.BlockSpec((None, tf, D), lambda i, f: (j, f, 0)),
                  pl.BlockSpec((tm, D), row), vec, vec],
        out_specs=[pl.BlockSpec((tm, D), row), pl.BlockSpec((tm, D), row)],
        out_shape=[jax.ShapeDtypeStruct((T, D), F32), jax.ShapeDtypeStruct((T, D), BF16)],
        scratch_shapes=[pltpu.VMEM((tm, D), F32)],
        compiler_params=_params(("parallel", "arbitrary"),
                                4 * tm * D + 12 * D * tf + 8 * tm * D + 12 * tm * D + 4 * tm * D
                                + 16 * tm * tf + 8 * tm * D),
        name="ffn_dense_swiglu_ln",
    )(xb, wg, wu, wd, h, g, b)


def _ffn_group_kernel(blk0, be_ref, rows_ref, x_ref, wg_ref, wu_ref, wd_ref, *rest):
    o_ref = rest[-1]
    f = pl.program_id(1)
    n_rows = rows_ref[blk0 + pl.program_id(0)]
    tm = x_ref.shape[0]

    @pl.when(jnp.logical_and(n_rows > 0, f == 0))
    def _():
        o_ref[...] = jnp.zeros_like(o_ref)

    step = min(tm, MOE_ROW_STEP)
    for hi in range(step, tm + 1, step):
        @pl.when(jnp.logical_and(n_rows > hi - step, n_rows <= hi))
        def _(hi=hi):
            o_ref[0:hi, :] += _swiglu_partial(x_ref.at[0:hi], wg_ref, wu_ref, wd_ref)


def _ffn_grouped(blk_e, blk_rows, xg, wg, wu, wd, j, tm, blk0, nblk, prev=None, tf=256):
    D = xg.shape[1]
    F = wg.shape[-1]
    nf = F // tf
    nb = xg.shape[0] // tm

    def live(i, rows):
        return rows[blk0 + i] > 0

    def x_rows(i, f, be, nv):
        return (jnp.where(live(i, nv), i, 0), 0)

    def out_rows(i, f, be, nv):
        return (jnp.where(live(i, nv), blk0 + i, nblk), 0)

    def hidden(i, f, nv):
        return jnp.where(live(i, nv), f, nf - 1)

    in_specs = [pl.BlockSpec((tm, D), x_rows),
                pl.BlockSpec((None, None, D, tf),
                             lambda i, f, be, nv: (j, be[blk0 + i], 0, hidden(i, f, nv))),
                pl.BlockSpec((None, None, D, tf),
                             lambda i, f, be, nv: (j, be[blk0 + i], 0, hidden(i, f, nv))),
                pl.BlockSpec((None, None, tf, D),
                             lambda i, f, be, nv: (j, be[blk0 + i], hidden(i, f, nv), 0))]
    args = [blk_e, blk_rows, xg, wg, wu, wd]
    aliases = {}
    if prev is not None:
        in_specs.append(pl.BlockSpec(memory_space=pl.ANY))
        aliases = {len(args): 0}
        args.append(prev)
    return pl.pallas_call(
        functools.partial(_ffn_group_kernel, blk0),
        grid_spec=pltpu.PrefetchScalarGridSpec(
            num_scalar_prefetch=2,
            grid=(nb, nf),
            in_specs=in_specs,
            out_specs=pl.BlockSpec((tm, D), out_rows)),
        out_shape=jax.ShapeDtypeStruct(((nblk + 1) * tm, D), F32),
        input_output_aliases=aliases,
        compiler_params=_params(("arbitrary", "arbitrary"),
                                4 * tm * D + 24 * D * tf + 8 * tm * D
                                + 6 * D * tf + 16 * tm * tf + 8 * tm * D),
        name="moe_grouped_swiglu",
    )(*args)


def _router_kernel(x_ref, wh_ref, wl_ref, b_ref, tri_ref, e_ref, w_ref, r_ref, cnt_ref):
    @pl.when(pl.program_id(0) == 0)
    def _():
        cnt_ref[...] = jnp.zeros_like(cnt_ref)

    x_hi, x_lo = _split_bf16(x_ref[...])
    nt = (((1,), (1,)), ((), ()))

    def dg(a, b):
        return lax.dot_general(a, b, nt, preferred_element_type=F32)

    wh = wh_ref[...]
    wl = wl_ref[...]
    logits = dg(wh, x_hi) + (dg(wh, x_lo) + dg(wl, x_hi)) + b_ref[...]
    n_e = logits.shape[0]
    eid = lax.broadcasted_iota(jnp.int32, logits.shape, 0)
    m1 = jnp.max(logits, axis=0, keepdims=True)
    i1 = jnp.min(jnp.where(logits == m1, eid, n_e), axis=0, keepdims=True)
    rest = jnp.where(eid == i1, -jnp.inf, logits)
    m2 = jnp.max(rest, axis=0, keepdims=True)
    i2 = jnp.min(jnp.where(rest == m2, eid, n_e), axis=0, keepdims=True)
    t = jnp.exp(m2 - m1)
    w1 = 1.0 / (1.0 + t)
    e_ref[...] = jnp.concatenate([i1, i2], axis=0)
    w_ref[...] = jnp.concatenate([w1, t * w1], axis=0)

    tri = tri_ref[...]
    oh1 = (eid == i1).astype(F32)
    oh2 = (eid == i2).astype(F32)
    tot1 = jnp.sum(oh1, axis=1, keepdims=True)
    tot2 = jnp.sum(oh2, axis=1, keepdims=True)
    base = cnt_ref[...][:, 0:1]
    c1 = base + _dot(oh1.astype(BF16), tri)
    c2 = base + tot1 + _dot(oh2.astype(BF16), tri)
    r1 = jnp.sum(oh1 * c1, axis=0, keepdims=True)
    r2 = jnp.sum(oh2 * c2, axis=0, keepdims=True)
    r_ref[...] = jnp.concatenate([r1, r2], axis=0).astype(jnp.int32)
    cnt_ref[...] = cnt_ref[...] + (tot1 + tot2)


def _router(h, rw_hi, rw_lo, rb, tm=512):
    T, D = h.shape
    E = rw_hi.shape[0]
    const = lambda i: (0, 0)
    out = pl.BlockSpec((TOP_K, tm), lambda i: (0, i))
    idx = jnp.arange(tm)
    tri = (idx[:, None] < idx[None, :]).astype(BF16)
    return pl.pallas_call(
        _router_kernel,
        grid=(T // tm,),
        in_specs=[pl.BlockSpec((tm, D), lambda i: (i, 0)), pl.BlockSpec((E, D), const),
                  pl.BlockSpec((E, D), const), pl.BlockSpec((E, 1), const),
                  pl.BlockSpec((tm, tm), const)],
        out_specs=[out, out, out, pl.BlockSpec((E, V7X_LANES), const)],
        out_shape=[jax.ShapeDtypeStruct((TOP_K, T), jnp.int32),
                   jax.ShapeDtypeStruct((TOP_K, T), F32),
                   jax.ShapeDtypeStruct((TOP_K, T), jnp.int32),
                   jax.ShapeDtypeStruct((E, V7X_LANES), F32)],
        compiler_params=_params(("arbitrary",), 32 * 1024 * 1024),
        name="moe_router_top2",
    )(h, rw_hi, rw_lo, rb, tri)


def _moe_combine_ln_kernel(alpha, n_prev, h_ref, ya_ref, yb_ref, w_ref, g_ref, b_ref, *rest):
    out_refs = rest[n_prev:]
    w = w_ref[...]
    f = ya_ref[...] * w[:, 0:1] + yb_ref[...] * w[:, 1:2]
    y = _layer_norm_rows(alpha * h_ref[...] + f, g_ref[...], b_ref[...])
    out_refs[0][...] = y
    if len(out_refs) > 1:
        out_refs[1][...] = y.astype(BF16)


def _moe_combine_ln(h, ya, yb, w, g, b, alpha, row0, prev, emit_bf16, tm=256):
    T, D = h.shape
    tm = min(tm, ya.shape[0])
    i0 = row0 // tm
    part = pl.BlockSpec((tm, D), lambda i: (i, 0))
    full = pl.BlockSpec((tm, D), lambda i: (i0 + i, 0))
    vec = pl.BlockSpec((1, D), lambda i: (0, 0))
    out_shape = [jax.ShapeDtypeStruct((T, D), F32)]
    if emit_bf16:
        out_shape.append(jax.ShapeDtypeStruct((T, D), BF16))
    args = [h, ya, yb, w, g.reshape(1, D), b.reshape(1, D)]
    in_specs = [full, part, part, pl.BlockSpec((tm, w.shape[1]), lambda i: (i0 + i, 0)), vec, vec]
    aliases = {}
    for k, p in enumerate(prev):
        aliases[len(args)] = k
        in_specs.append(pl.BlockSpec(memory_space=pl.ANY))
        args.append(p)
    return pl.pallas_call(
        functools.partial(_moe_combine_ln_kernel, alpha, len(prev)),
        grid=(ya.shape[0] // tm,),
        in_specs=in_specs,
        out_specs=[full] * len(out_shape),
        out_shape=out_shape,
        input_output_aliases=aliases,
        compiler_params=_params(("parallel",), 64 * tm * D),
        name="moe_combine_ln",
    )(*args)


def _dft_tables(L):
    n2 = 2 * L
    k = jnp.arange(L, dtype=jnp.int32)[:, None]
    n = jnp.arange(L, dtype=jnp.int32)[None, :]
    ang = (((2 * k + 1) * (2 * n + 1)) % (4 * n2)).astype(F32) * (math.pi / (2 * n2))
    phi = (2 * k + 1).astype(F32) * (math.pi / (2 * n2))
    return jnp.cos(ang).astype(BF16), jnp.sin(ang).astype(BF16), jnp.cos(phi), jnp.sin(phi)


def _rope_tables(L, hd):
    rows = L // GRID_W
    row = jnp.repeat(jnp.arange(rows, dtype=F32), GRID_W)
    col = jnp.tile(jnp.arange(GRID_W, dtype=F32), rows)
    n_pairs = hd // 4
    inv = ROPE_THETA ** (-jnp.arange(n_pairs, dtype=F32) / n_pairs)
    ang = jnp.concatenate([row[:, None] * inv, col[:, None] * inv], axis=-1)
    cos, sin = jnp.cos(ang), jnp.sin(ang)
    reps = V7X_LANES // hd
    return (jnp.tile(jnp.concatenate([cos, cos], axis=-1), (1, reps)),
            jnp.tile(jnp.concatenate([-sin, sin], axis=-1), (1, reps)))


def _filter_features(L, n_emb):
    t = jnp.linspace(0.0, 1.0, L, dtype=F32)[:, None]
    bands = (n_emb - 1) // 2
    w = 2.0 * math.pi * jnp.arange(L, dtype=F32)[:, None] / L
    f = jnp.linspace(1e-4, bands - 1, bands, dtype=F32)[None, :]
    z = jnp.concatenate([t, jnp.cos(f * w), -jnp.sin(f * w)], axis=-1)
    return jnp.pad(z, ((0, 0), (0, V7X_LANES - n_emb)))


def _moe_block_rows(T):
    return min(1024, max(V7X_MXU_DIM, T // 4))


def _moe(hf, hb, router_w, router_b, wg, wu, wd, j):
    T, D = hf.shape
    E = router_w.shape[1]
    rw_hi, rw_lo = _split_bf16(router_w.T)
    top_e, top_w, rank, cnt = _router(hf, rw_hi, rw_lo, router_b.reshape(E, 1))

    tm = _moe_block_rows(T)
    counts = cnt[:, 0].astype(jnp.int32)
    padded = (counts + tm - 1) // tm * tm
    pad_end = jnp.cumsum(padded)
    pad_start = pad_end - padded
    nblk = -(-(T * TOP_K + E * (tm - 1)) // tm)
    P = nblk * tm
    n_valid = (pad_end[-1] // tm).astype(jnp.int32)
    blk_first = jnp.arange(nblk, dtype=jnp.int32) * tm
    blk_e = jnp.sum((blk_first[:, None] >= pad_end[None, :]).astype(jnp.int32), axis=1)
    blk_e = jnp.minimum(blk_e, E - 1)
    row_end = (pad_start + counts)[blk_e]
    blk_rows = jnp.where(jnp.arange(nblk) < n_valid, jnp.clip(row_end - blk_first, 0, tm), 0)
    blk_e = jnp.where(jnp.arange(nblk) < n_valid, blk_e, blk_e[jnp.maximum(n_valid - 1, 0)])
    start_of = jnp.sum(jnp.where(top_e[:, :, None] == jnp.arange(E, dtype=jnp.int32),
                                 pad_start[None, None, :], 0), axis=-1)
    dest = start_of + rank

    tok = jnp.broadcast_to(jnp.arange(T, dtype=jnp.int32), (TOP_K, T))
    src_tok = jnp.zeros((P,), jnp.int32).at[dest.reshape(-1)].set(tok.reshape(-1))
    yg = None
    blk_rows = blk_rows.astype(jnp.int32)
    for blk0 in range(0, nblk, -(-nblk // MOE_CALLS)):
        nb = min(-(-nblk // MOE_CALLS), nblk - blk0)
        xg = hb[lax.slice_in_dim(src_tok, blk0 * tm, (blk0 + nb) * tm)]
        yg = _ffn_grouped(blk_e, blk_rows, xg, wg, wu, wd, j, tm, blk0, nblk, prev=yg)
    return yg, dest, top_w.T


def kernel(x, ln_in_g, ln_in_b, w_in, hy_conv_w, hy_conv_b, hy_fw1, hy_fb1, hy_fw2, hy_fb2, hy_fw3, hy_freq, hy_bias, q_norm_g, k_norm_g, w_hy_br, w_att_br, w_o, ln_mix_g, ln_mix_b, ffn_w_gate, ffn_w_up, ffn_w_down, router_w, router_b, exp_w_gate, exp_w_up, exp_w_down, ln_ffn_g, ln_ffn_b):
    B, L, D = x.shape
    T = B * L
    depth = w_in.shape[0]
    dh = w_hy_br.shape[1]
    d_attn = w_att_br.shape[1]
    hd = q_norm_g.shape[-1]
    n_q = d_attn // hd
    d_kv = (w_in.shape[2] - 3 * dh - d_attn - 2 * D) // 2
    n_kv = d_kv // hd
    alpha = (2 * depth) ** 0.25
    qkv0 = 3 * dh
    gate0 = qkv0 + d_attn + 2 * d_kv
    n_rot = (n_q + n_kv) * hd
    assert L % GRID_W == 0 and V7X_LANES % hd == 0 and n_rot % V7X_MXU_DIM == 0

    cs_bf, ss_bf, cos_phi, sin_phi = _dft_tables(L)
    cos_t, sin_t = _rope_tables(L, hd)
    z = _filter_features(L, hy_fw1.shape[1])
    min_decay = math.log(DECAY_TARGET) / SLOW_DECAY_PCT
    max_decay = math.log(DECAY_TARGET) / FAST_DECAY_PCT
    deltas = jnp.abs(jnp.linspace(min_decay, max_decay, dh, dtype=F32)).reshape(1, dh)
    lane = jnp.arange(V7X_MXU_DIM)
    bd = jnp.where((lane[:, None] // hd) == (lane[None, :] // hd), 1.0 / hd, 0.0).astype(BF16)
    qk_scale = jnp.concatenate([jnp.full((n_q * hd,), hd ** -0.5, F32),
                                jnp.ones((n_kv * hd,), F32)]).reshape(1, n_rot)

    w_in_bf = w_in.astype(BF16)
    w_hy_bf = w_hy_br.astype(BF16)
    w_att_bf = w_att_br.astype(BF16)
    w_o_bf = w_o.astype(BF16)
    ffn_g_bf = ffn_w_gate.astype(BF16)
    ffn_u_bf = ffn_w_up.astype(BF16)
    ffn_d_bf = ffn_w_down.astype(BF16)
    conv_b3 = hy_conv_b.reshape(depth, 1, 3 * dh)
    ln_mix_g3, ln_mix_b3 = ln_mix_g.reshape(depth, 1, D), ln_mix_b.reshape(depth, 1, D)
    ln_ffn_g3, ln_ffn_b3 = ln_ffn_g.reshape(depth, 1, D), ln_ffn_b.reshape(depth, 1, D)

    hf, hb = _resid_ln(x.reshape(T, D), [], ln_in_g, ln_in_b, alpha)
    for i in range(depth):
        x0, vv = _hy_proj(hb.reshape(B, L, D), w_in_bf, hy_conv_w, conv_b3, i, dh)
        fw1p = jnp.pad(hy_fw1[i], ((0, V7X_LANES - hy_fw1.shape[1]), (0, 0)))
        hs, hdf = _filters(z, fw1p, hy_fb1[i].reshape(1, -1), hy_fw2[i], hy_fb2[i].reshape(1, -1),
                           hy_freq[i], hy_fw3[i], deltas, dh)
        kre, kim = _spectrum(cs_bf, ss_bf, cos_phi, sin_phi, hs, hdf)
        y_hy = _longconv(cs_bf, ss_bf, vv, x0, kre, kim, hy_bias[i].reshape(1, dh))
        gain = jnp.concatenate([jnp.tile(q_norm_g[i], n_q), jnp.tile(k_norm_g[i], n_kv)]).reshape(1, n_rot)
        w_qkv = lax.slice_in_dim(w_in_bf[i], qkv0, gate0, axis=1)
        q, k, v = _qkv_proj(hb, w_qkv, bd, gain, qk_scale, cos_t, sin_t, B, L, n_q, n_kv, hd)
        y_att = _attention(q, k, v)
        merged = _merge(hb, y_hy.reshape(T, dh), y_att.reshape(T, d_attn), w_in_bf, w_hy_bf, w_att_bf,
                        i, gate0)
        hf, hb = _oproj_ln(merged, w_o_bf, hf, ln_mix_g3, ln_mix_b3, i, alpha)
        j = i // 2
        if i % 2 == 0:
            hf, hb = _ffn_dense(hb, ffn_g_bf, ffn_u_bf, ffn_d_bf, hf, ln_ffn_g3, ln_ffn_b3, j, i, alpha)
        else:
            yg, dest, wts = _moe(hf, hb, router_w[j], router_b[j], exp_w_gate, exp_w_up, exp_w_down, j)
            outs = []
            for row0 in range(0, T, -(-T // COMBINE_CALLS)):
                rows = slice(row0, min(row0 + -(-T // COMBINE_CALLS), T))
                outs = _moe_combine_ln(hf, yg[dest[0, rows]], yg[dest[1, rows]], wts,
                                       ln_ffn_g[i], ln_ffn_b[i], alpha, row0, outs,
                                       emit_bf16=i + 1 < depth)
            hf, hb = outs[0], outs[-1]
    return hf.reshape(B, L, D)
```

```python
import functools
import math

import jax
import jax.numpy as jnp
from jax import lax
from jax.experimental import pallas as pl
from jax.experimental.pallas import tpu as pltpu

F32 = jnp.float32
BF16 = jnp.bfloat16

GRID_W = 64
ROPE_THETA = 10000.0
TOP_K = 2
DECAY_TARGET = 1e-2
FAST_DECAY_PCT = 0.3
SLOW_DECAY_PCT = 1.5
LN_EPS = 1e-5
RMS_EPS = 1e-6

V7X_VMEM_BYTES = 64 * 1024 * 1024
V7X_LANES = 128
V7X_MXU_DIM = 256
VMEM_BUDGET = V7X_VMEM_BYTES - 8 * 1024 * 1024
ATTN_SUBTILE = 512
EPILOGUE_SUBTILE = V7X_MXU_DIM
MOE_CALLS = 4
MOE_ROW_STEP = V7X_MXU_DIM


def _params(semantics, vmem_bytes):
    return pltpu.CompilerParams(dimension_semantics=semantics,
                                vmem_limit_bytes=min(int(vmem_bytes), VMEM_BUDGET))


def _dot(a, b):
    return jnp.dot(a, b, preferred_element_type=F32)


def _split_bf16(a):
    hi = a.astype(BF16)
    lo = (a - hi.astype(F32)).astype(BF16)
    return hi, lo


def _dot3(a_hi, a_lo, b_hi, b_lo):
    return _dot(a_hi, b_hi) + (_dot(a_hi, b_lo) + _dot(a_lo, b_hi))


def _layer_norm_rows(x, g, b):
    mu = jnp.mean(x, axis=-1, keepdims=True)
    xc = x - mu
    var = jnp.mean(xc * xc, axis=-1, keepdims=True)
    return xc * lax.rsqrt(var + LN_EPS) * g + b


def _resid_ln_kernel(alpha, n_add, *refs):
    h_ref = refs[0]
    add_refs = refs[1:1 + n_add]
    g_ref, b_ref, of_ref, ob_ref = refs[1 + n_add:]
    x = h_ref[...]
    if n_add:
        acc = add_refs[0][...]
        for r in add_refs[1:]:
            acc = acc + r[...]
        x = alpha * x + acc
    y = _layer_norm_rows(x, g_ref[...], b_ref[...])
    of_ref[...] = y
    ob_ref[...] = y.astype(BF16)


def _resid_ln(h, addends, g, b, alpha, tm=256):
    T, D = h.shape
    n_add = len(addends)
    row = pl.BlockSpec((tm, D), lambda i: (i, 0))
    vec = pl.BlockSpec((1, D), lambda i: (0, 0))
    return pl.pallas_call(
        functools.partial(_resid_ln_kernel, alpha, n_add),
        grid=(T // tm,),
        in_specs=[row] * (1 + n_add) + [vec, vec],
        out_specs=[row, row],
        out_shape=[jax.ShapeDtypeStruct((T, D), F32), jax.ShapeDtypeStruct((T, D), BF16)],
        compiler_params=_params(("parallel",), (2 * (1 + n_add) * 4 + 12 + 16) * tm * D),
        name="resid_ln",
    )(h, *addends, g.reshape(1, D), b.reshape(1, D))


def _hy_proj_kernel(h_ref, w0_ref, w1_ref, w2_ref, cw0_ref, cw1_ref, cw2_ref,
                    cb0_ref, cb1_ref, cb2_ref, x0_ref, vv_ref):
    h = h_ref[...]
    L = h.shape[0]
    row = lax.broadcasted_iota(jnp.int32, (L, 1), 0)

    def stream(w_ref, cw_ref, cb_ref):
        p = _dot(h, w_ref[...])
        prev = jnp.where(row == 0, 0.0, pltpu.roll(p, 1, 0))
        nxt = jnp.where(row == L - 1, 0.0, pltpu.roll(p, L - 1, 0))
        cw = cw_ref[...]
        return prev * cw[0:1] + p * cw[1:2] + nxt * cw[2:3] + cb_ref[...]

    x0_ref[...] = stream(w0_ref, cw0_ref, cb0_ref)
    vv_ref[...] = stream(w1_ref, cw1_ref, cb1_ref) * stream(w2_ref, cw2_ref, cb2_ref)


def _hy_proj(hb3, w_in_bf, conv_w, conv_b3, layer, dh, tn=256):
    B, L, D = hb3.shape
    nj = dh // tn

    def wspec(s):
        return pl.BlockSpec((None, D, tn), lambda j, b: (layer, 0, s * nj + j))

    def cwspec(s):
        return pl.BlockSpec((None, conv_w.shape[1], tn), lambda j, b: (layer, 0, s * nj + j))

    def cbspec(s):
        return pl.BlockSpec((None, 1, tn), lambda j, b: (layer, 0, s * nj + j))

    out = pl.BlockSpec((None, L, tn), lambda j, b: (b, 0, j))
    return pl.pallas_call(
        _hy_proj_kernel,
        grid=(nj, B),
        in_specs=[pl.BlockSpec((None, L, D), lambda j, b: (b, 0, 0)),
                  wspec(0), wspec(1), wspec(2), cwspec(0), cwspec(1), cwspec(2),
                  cbspec(0), cbspec(1), cbspec(2)],
        out_specs=[out, out],
        out_shape=[jax.ShapeDtypeStruct((B, L, dh), F32)] * 2,
        compiler_params=_params(("parallel", "parallel"),
                                4 * L * D + 12 * D * tn + 16 * L * tn + 40 * L * tn),
        name="hyena_proj_conv",
    )(hb3, w_in_bf, w_in_bf, w_in_bf, conv_w, conv_w, conv_w, conv_b3, conv_b3, conv_b3)


def _qkv_kernel(n_q, n_kv, hd, x_ref, w_ref, bd_ref, gain_ref, scale_ref, cos_ref, sin_ref,
                q_ref, k_ref, v_ref):
    n_rot = (n_q + n_kv) * hd
    tm = x_ref.shape[0]
    ts = min(tm, EPILOGUE_SUBTILE)
    w = w_ref[...]
    bd = bd_ref[...]
    gw = bd.shape[0]
    reps = n_rot // V7X_LANES
    lane = lax.broadcasted_iota(jnp.int32, (1, n_rot), 1)
    first_half = (lane % hd) < (hd // 2)
    half = hd // 2
    pad_lane = lax.broadcasted_iota(jnp.int32, (ts, V7X_LANES - hd), 1)
    ones_col = jnp.where(pad_lane == 0, 1.0, 0.0).astype(BF16)
    for r0 in range(0, tm, ts):
        rows = slice(r0, r0 + ts)
        p = _dot(x_ref[rows, :], w)
        pr = p[:, :n_rot]
        sq = (pr * pr).astype(BF16)
        ms = jnp.concatenate([_dot(sq[:, c:c + gw], bd) for c in range(0, n_rot, gw)], axis=1)
        xn = pr * lax.rsqrt(ms + RMS_EPS) * gain_ref[...]
        cos = jnp.concatenate([cos_ref[rows, :]] * reps, axis=1)
        sin = jnp.concatenate([sin_ref[rows, :]] * reps, axis=1)
        swapped = jnp.where(first_half, pltpu.roll(xn, n_rot - half, 1), pltpu.roll(xn, half, 1))
        rot = ((xn * cos + swapped * sin) * scale_ref[...]).astype(BF16)
        for hh in range(n_q):
            q_ref[hh, rows, :] = rot[:, hh * hd:(hh + 1) * hd]
        for hh in range(n_kv):
            k_ref[hh, rows, :] = rot[:, (n_q + hh) * hd:(n_q + hh + 1) * hd]
            v_h = p[:, n_rot + hh * hd:n_rot + (hh + 1) * hd].astype(BF16)
            v_ref[hh, rows, :] = jnp.concatenate([v_h, ones_col], axis=1)


def _qkv_proj(hb, w_qkv, bd, gain, scale, cos_t, sin_t, B, L, n_q, n_kv, hd, tm=512):
    T, D = hb.shape
    nw = w_qkv.shape[1]
    n_rot = (n_q + n_kv) * hd
    tm = min(tm, L)
    tpb = L // tm
    const = lambda i: (0, 0)
    return pl.pallas_call(
        functools.partial(_qkv_kernel, n_q, n_kv, hd),
        grid=(T // tm,),
        in_specs=[pl.BlockSpec((tm, D), lambda i: (i, 0)),
                  pl.BlockSpec((D, nw), const),
                  pl.BlockSpec(bd.shape, const),
                  pl.BlockSpec((1, n_rot), const),
                  pl.BlockSpec((1, n_rot), const),
                  pl.BlockSpec((tm, V7X_LANES), lambda i: (i % tpb, 0)),
                  pl.BlockSpec((tm, V7X_LANES), lambda i: (i % tpb, 0))],
        out_specs=[pl.BlockSpec((None, n_q, tm, hd), lambda i: (i // tpb, 0, i % tpb, 0)),
                   pl.BlockSpec((None, n_kv, tm, hd), lambda i: (i // tpb, 0, i % tpb, 0)),
                   pl.BlockSpec((None, n_kv, tm, V7X_LANES), lambda i: (i // tpb, 0, i % tpb, 0))],
        out_shape=[jax.ShapeDtypeStruct((B, n_q, L, hd), BF16),
                   jax.ShapeDtypeStruct((B, n_kv, L, hd), BF16),
                   jax.ShapeDtypeStruct((B, n_kv, L, V7X_LANES), BF16)],
        compiler_params=_params(("parallel",),
                                4 * tm * D + 4 * D * nw + 8 * tm * (n_q + 2 * n_kv) * V7X_LANES
                                + 48 * tm * nw),
        name="qkv_proj_norm_rope",
    )(hb, w_qkv, bd, gain, scale, cos_t, sin_t)


def _attn_kernel(group, hd, q_ref, k_ref, v_ref, o_ref):
    k = k_ref[...]
    v = v_ref[...]
    tq = q_ref.shape[1]
    ts = min(tq, ATTN_SUBTILE)
    for r0 in range(0, tq, ts):
        for g in range(group):
            s = lax.dot_general(q_ref[g, r0:r0 + ts, :], k, (((1,), (1,)), ((), ())),
                                preferred_element_type=F32)
            m = jnp.max(s, axis=-1, keepdims=True)
            e = jnp.exp(s - m).astype(BF16)
            o = _dot(e, v)
            o_ref[r0:r0 + ts, g * hd:(g + 1) * hd] = (o[:, :hd] / o[:, hd:hd + 1]).astype(o_ref.dtype)


def _attention(q, k, v, tq=2048):
    B, n_q, L, hd = q.shape
    n_kv = k.shape[1]
    group = n_q // n_kv
    tq = min(tq, L)
    return pl.pallas_call(
        functools.partial(_attn_kernel, group, hd),
        grid=(B, n_kv, L // tq),
        in_specs=[pl.BlockSpec((None, group, tq, hd), lambda b, j, t: (b, j, t, 0)),
                  pl.BlockSpec((None, None, L, hd), lambda b, j, t: (b, j, 0, 0)),
                  pl.BlockSpec((None, None, L, v.shape[-1]), lambda b, j, t: (b, j, 0, 0))],
        out_specs=pl.BlockSpec((None, tq, group * hd), lambda b, j, t: (b, t, j)),
        out_shape=jax.ShapeDtypeStruct((B, L, n_q * hd), BF16),
        compiler_params=_params(("parallel", "parallel", "parallel"),
                                8 * L * V7X_LANES + 8 * group * tq * V7X_LANES + 40 * tq * L),
        name="gqa_attention",
    )(q, k, v)


def _filter_kernel(z_ref, fw1_ref, fb1_ref, fw2_ref, fb2_ref, fr_ref, w3f_ref, w3b_ref, dl_ref,
                   hs_ref, hd_ref):
    z = z_ref[...]
    L = z.shape[0]
    fr = fr_ref[...]

    def hp_dot(a, b):
        a_hi, a_lo = _split_bf16(a)
        b_hi, b_lo = _split_bf16(b)
        return _dot3(a_hi, a_lo, b_hi, b_lo)

    a = jnp.sin(fr[0:1] * (hp_dot(z, fw1_ref[...]) + fb1_ref[...]))
    a = jnp.sin(fr[1:2] * (hp_dot(a, fw2_ref[...]) + fb2_ref[...]))
    window = jnp.exp(-z[:, 0:1] * dl_ref[...])
    h_fwd = hp_dot(a, w3f_ref[...]) * window
    h_bwd = hp_dot(a, w3b_ref[...]) * window
    row = lax.broadcasted_iota(jnp.int32, (L, 1), 0)
    h_bwd0 = jnp.where(row == 0, 0.0, h_bwd)
    hs_ref[...] = h_fwd + h_bwd0
    hd_ref[...] = h_bwd0 - h_fwd


def _filters(z, fw1p, fb1, fw2, fb2, fr, fw3, deltas, dh, tn=256):
    L, zw = z.shape
    nh = fw2.shape[0]
    nj = dh // tn
    const = lambda j: (0, 0)
    out = pl.BlockSpec((L, tn), lambda j: (0, j))
    return pl.pallas_call(
        _filter_kernel,
        grid=(nj,),
        in_specs=[pl.BlockSpec((L, zw), const), pl.BlockSpec((zw, nh), const),
                  pl.BlockSpec((1, nh), const), pl.BlockSpec((nh, nh), const),
                  pl.BlockSpec((1, nh), const), pl.BlockSpec((2, nh), const),
                  pl.BlockSpec((nh, tn), lambda j: (0, j)),
                  pl.BlockSpec((nh, tn), lambda j: (0, nj + j)),
                  pl.BlockSpec((1, tn), lambda j: (0, j))],
        out_specs=[out, out],
        out_shape=[jax.ShapeDtypeStruct((L, dh), F32)] * 2,
        compiler_params=_params(("parallel",), 32 * 1024 * 1024),
        name="hyena_filters",
    )(z, fw1p, fb1, fw2, fb2, fr, fw3, fw3, deltas)


def _spectrum_kernel(cs_ref, ss_ref, cp_ref, sp_ref, hs_ref, hd_ref, kre_ref, kim_ref):
    cs = cs_ref[...]
    ss = ss_ref[...]
    hs = hs_ref[...].astype(BF16)
    hd = hd_ref[...].astype(BF16)
    cp = cp_ref[...]
    sp = sp_ref[...]
    kre_ref[...] = cp * _dot(cs, hs) + sp * _dot(ss, hs)
    kim_ref[...] = cp * _dot(ss, hd) - sp * _dot(cs, hd)


def _spectrum(cs, ss, cos_phi, sin_phi, hs, hd, tk=512, tn=256):
    L, dh = hs.shape
    tk = min(tk, L)
    mat = pl.BlockSpec((tk, L), lambda j, i: (i, 0))
    vec = pl.BlockSpec((tk, 1), lambda j, i: (i, 0))
    col = pl.BlockSpec((L, tn), lambda j, i: (0, j))
    out = pl.BlockSpec((tk, tn), lambda j, i: (i, j))
    return pl.pallas_call(
        _spectrum_kernel,
        grid=(dh // tn, L // tk),
        in_specs=[mat, mat, vec, vec, col, col],
        out_specs=[out, out],
        out_shape=[jax.ShapeDtypeStruct((L, dh), F32)] * 2,
        compiler_params=_params(("parallel", "parallel"), 40 * 1024 * 1024),
        name="filter_spectrum",
    )(cs, ss, cos_phi, sin_phi, hs, hd)


def _longconv_kernel(inv_scale, cs_ref, ss_ref, v_ref, x0_ref, kre_ref, kim_ref, bias_ref, y_ref):
    cs = cs_ref[...]
    ss = ss_ref[...]
    v = v_ref[...]
    vb = v.astype(BF16)
    xc = _dot(cs, vb)
    xs = _dot(ss, vb)
    kre = kre_ref[...]
    kim = kim_ref[...]
    yre = (xc * kre + xs * kim).astype(BF16)
    yim = (xc * kim - xs * kre).astype(BF16)
    conv = (_dot(cs, yre) - _dot(ss, yim)) * inv_scale
    y_ref[...] = (x0_ref[...] * (conv + bias_ref[...] * v)).astype(y_ref.dtype)


def _longconv(cs, ss, vv, x0, kre, kim, bias, tc=256):
    B, L, dh = vv.shape
    mat = pl.BlockSpec((L, L), lambda j, b: (0, 0), pipeline_mode=pl.Buffered(1))
    act = pl.BlockSpec((None, L, tc), lambda j, b: (b, 0, j))
    spec = pl.BlockSpec((L, tc), lambda j, b: (0, j))
    return pl.pallas_call(
        functools.partial(_longconv_kernel, 1.0 / L),
        grid=(dh // tc, B),
        in_specs=[mat, mat, act, act, spec, spec, pl.BlockSpec((1, tc), lambda j, b: (0, j))],
        out_specs=act,
        out_shape=jax.ShapeDtypeStruct((B, L, dh), BF16),
        compiler_params=_params(("parallel", "parallel"),
                                4 * L * L + 16 * L * tc + 16 * L * tc + 4 * L * tc + 40 * L * tc),
        name="hyena_longconv",
    )(cs, ss, vv, x0, kre, kim, bias)


def _merge_kernel(h_ref, yh_ref, ya_ref, wgh_ref, wga_ref, wh_ref, wa_ref, o_ref):
    h = h_ref[...]
    gh = jax.nn.sigmoid(_dot(h, wgh_ref[...]))
    ga = jax.nn.sigmoid(_dot(h, wga_ref[...]))
    o_ref[...] = (gh * _dot(yh_ref[...], wh_ref[...])
                  + ga * _dot(ya_ref[...], wa_ref[...])).astype(o_ref.dtype)


def _merge(hb, y_hy, y_att, w_in_bf, w_hy, w_att, layer, gate_col0, tm=1024, tn=512):
    T, D = hb.shape
    dh, da = y_hy.shape[1], y_att.shape[1]
    tm = min(tm, T)
    g0 = gate_col0 // tn
    nj = D // tn
    return pl.pallas_call(
        _merge_kernel,
        grid=(T // tm, nj),
        in_specs=[pl.BlockSpec((tm, D), lambda i, j: (i, 0)),
                  pl.BlockSpec((tm, dh), lambda i, j: (i, 0)),
                  pl.BlockSpec((tm, da), lambda i, j: (i, 0)),
                  pl.BlockSpec((None, D, tn), lambda i, j: (layer, 0, g0 + j)),
                  pl.BlockSpec((None, D, tn), lambda i, j: (layer, 0, g0 + nj + j)),
                  pl.BlockSpec((None, dh, tn), lambda i, j: (layer, 0, j)),
                  pl.BlockSpec((None, da, tn), lambda i, j: (layer, 0, j))],
        out_specs=pl.BlockSpec((tm, tn), lambda i, j: (i, j)),
        out_shape=jax.ShapeDtypeStruct((T, D), BF16),
        compiler_params=_params(("parallel", "parallel"),
                                4 * tm * (D + dh + da) + 4 * tn * (2 * D + dh + da) + 4 * tm * tn
                                + 32 * tm * tn),
        name="gated_merge",
    )(hb, y_hy, y_att, w_in_bf, w_in_bf, w_hy, w_att)


def _oproj_ln_kernel(alpha, x_ref, w_ref, h_ref, g_ref, b_ref, of_ref, ob_ref):
    tm = x_ref.shape[0]
    ts = min(tm, EPILOGUE_SUBTILE)
    w = w_ref[...]
    for r0 in range(0, tm, ts):
        y = alpha * h_ref[r0:r0 + ts, :] + _dot(x_ref[r0:r0 + ts, :], w)
        y = _layer_norm_rows(y, g_ref[...], b_ref[...])
        of_ref[r0:r0 + ts, :] = y
        ob_ref[r0:r0 + ts, :] = y.astype(BF16)


def _oproj_ln(xb, w_o, h, g, b, layer, alpha, tm=512):
    T, D = h.shape
    row = lambda i: (i, 0)
    vec = pl.BlockSpec((None, 1, D), lambda i: (layer, 0, 0))
    return pl.pallas_call(
        functools.partial(_oproj_ln_kernel, alpha),
        grid=(T // tm,),
        in_specs=[pl.BlockSpec((tm, D), row),
                  pl.BlockSpec((None, D, D), lambda i: (layer, 0, 0)),
                  pl.BlockSpec((tm, D), row), vec, vec],
        out_specs=[pl.BlockSpec((tm, D), row), pl.BlockSpec((tm, D), row)],
        out_shape=[jax.ShapeDtypeStruct((T, D), F32), jax.ShapeDtypeStruct((T, D), BF16)],
        compiler_params=_params(("parallel",), 4 * D * D + 24 * tm * D + 24 * tm * D),
        name="oproj_resid_ln",
    )(xb, w_o, h, g, b)


def _swiglu_partial(x_ref, wg_ref, wu_ref, wd_ref):
    x = x_ref[...]
    g = _dot(x, wg_ref[...].astype(BF16))
    u = _dot(x, wu_ref[...].astype(BF16))
    a = (g * jax.nn.sigmoid(g) * u).astype(BF16)
    return _dot(a, wd_ref[...].astype(BF16))


def _ffn_dense_kernel(alpha, x_ref, wg_ref, wu_ref, wd_ref, h_ref, g_ref, b_ref,
                      of_ref, ob_ref, acc_ref):
    f = pl.program_id(1)

    @pl.when(f == 0)
    def _():
        acc_ref[...] = jnp.zeros_like(acc_ref)

    acc_ref[...] += _swiglu_partial(x_ref, wg_ref, wu_ref, wd_ref)

    @pl.when(f == pl.num_programs(1) - 1)
    def _():
        y = _layer_norm_rows(alpha * h_ref[...] + acc_ref[...], g_ref[...], b_ref[...])
        of_ref[...] = y
        ob_ref[...] = y.astype(BF16)


def _ffn_dense(xb, wg, wu, wd, h, g, b, j, layer, alpha, tm=512, tf=512):
    T, D = xb.shape
    F = wg.shape[-1]
    row = lambda i, f: (i, 0)
    vec = pl.BlockSpec((None, 1, D), lambda i, f: (layer, 0, 0))
    return pl.pallas_call(
        functools.partial(_ffn_dense_kernel, alpha),
        grid=(T // tm, F // tf),
        in_specs=[pl.BlockSpec((tm, D), row),
                  pl.BlockSpec((None, D, tf), lambda i, f: (j, 0, f)),
                  pl.BlockSpec((None, D, tf), lambda i, f: (j, 0, f)),
                  pl.BlockSpec((None, tf, D), lambda i, f: (j, f, 0)),
                  pl.BlockSpec((tm, D), row), vec, vec],
        out_specs=[pl.BlockSpec((tm, D), row), pl.BlockSpec((tm, D), row)],
        out_shape=[jax.ShapeDtypeStruct((T, D), F32), jax.ShapeDtypeStruct((T, D), BF16)],
        scratch_shapes=[pltpu.VMEM((tm, D), F32)],
        compiler_params=_params(("parallel", "arbitrary"),
                                4 * tm * D + 12 * D * tf + 8 * tm * D + 12 * tm * D + 4 * tm * D
                                + 16 * tm * tf + 8 * tm * D),
        name="ffn_dense_swiglu_ln",
    )(xb, wg, wu, wd, h, g, b)


def _ffn_group_kernel(blk0, be_ref, rows_ref, x_ref, wg_ref, wu_ref, wd_ref, *rest):
    o_ref = rest[-1]
    f = pl.program_id(1)
    n_rows = rows_ref[blk0 + pl.program_id(0)]
    tm = x_ref.shape[0]

    @pl.when(jnp.logical_and(n_rows > 0, f == 0))
    def _():
        o_ref[...] = jnp.zeros_like(o_ref)

    step = min(tm, MOE_ROW_STEP)
    for hi in range(step, tm + 1, step):
        @pl.when(jnp.logical_and(n_rows > hi - step, n_rows <= hi))
        def _(hi=hi):
            o_ref[0:hi, :] += _swiglu_partial(x_ref.at[0:hi], wg_ref, wu_ref, wd_ref)


def _ffn_grouped(blk_e, blk_rows, xg, wg, wu, wd, j, tm, blk0, nblk, prev=None, tf=512):
    D = xg.shape[1]
    F = wg.shape[-1]
    nf = F // tf
    nb = xg.shape[0] // tm

    def live(i, rows):
        return rows[blk0 + i] > 0

    def x_rows(i, f, be, nv):
        return (jnp.where(live(i, nv), i, 0), 0)

    def out_rows(i, f, be, nv):
        return (jnp.where(live(i, nv), blk0 + i, nblk), 0)

    def hidden(i, f, nv):
        return jnp.where(live(i, nv), f, nf - 1)

    in_specs = [pl.BlockSpec((tm, D), x_rows),
                pl.BlockSpec((None, None, D, tf),
                             lambda i, f, be, nv: (j, be[blk0 + i], 0, hidden(i, f, nv))),
                pl.BlockSpec((None, None, D, tf),
                             lambda i, f, be, nv: (j, be[blk0 + i], 0, hidden(i, f, nv))),
                pl.BlockSpec((None, None, tf, D),
                             lambda i, f, be, nv: (j, be[blk0 + i], hidden(i, f, nv), 0))]
    args = [blk_e, blk_rows, xg, wg, wu, wd]
    aliases = {}
    if prev is not None:
        in_specs.append(pl.BlockSpec(memory_space=pl.ANY))
        aliases = {len(args): 0}
        args.append(prev)
    return pl.pallas_call(
        functools.partial(_ffn_group_kernel, blk0),
        grid_spec=pltpu.PrefetchScalarGridSpec(
            num_scalar_prefetch=2,
            grid=(nb, nf),
            in_specs=in_specs,
            out_specs=pl.BlockSpec((tm, D), out_rows)),
        out_shape=jax.ShapeDtypeStruct(((nblk + 1) * tm, D), F32),
        input_output_aliases=aliases,
        compiler_params=_params(("arbitrary", "arbitrary"),
                                4 * tm * D + 24 * D * tf + 8 * tm * D
                                + 6 * D * tf + 16 * tm * tf + 8 * tm * D),
        name="moe_grouped_swiglu",
    )(*args)


def _router_kernel(x_ref, wh_ref, wl_ref, b_ref, tri_ref, e_ref, w_ref, r_ref, cnt_ref):
    @pl.when(pl.program_id(0) == 0)
    def _():
        cnt_ref[...] = jnp.zeros_like(cnt_ref)

    x_hi, x_lo = _split_bf16(x_ref[...])
    nt = (((1,), (1,)), ((), ()))

    def dg(a, b):
        return lax.dot_general(a, b, nt, preferred_element_type=F32)

    wh = wh_ref[...]
    wl = wl_ref[...]
    logits = dg(wh, x_hi) + (dg(wh, x_lo) + dg(wl, x_hi)) + b_ref[...]
    n_e = logits.shape[0]
    eid = lax.broadcasted_iota(jnp.int32, logits.shape, 0)
    m1 = jnp.max(logits, axis=0, keepdims=True)
    i1 = jnp.min(jnp.where(logits == m1, eid, n_e), axis=0, keepdims=True)
    rest = jnp.where(eid == i1, -jnp.inf, logits)
    m2 = jnp.max(rest, axis=0, keepdims=True)
    i2 = jnp.min(jnp.where(rest == m2, eid, n_e), axis=0, keepdims=True)
    t = jnp.exp(m2 - m1)
    w1 = 1.0 / (1.0 + t)
    e_ref[...] = jnp.concatenate([i1, i2], axis=0)
    w_ref[...] = jnp.concatenate([w1, t * w1], axis=0)

    tri = tri_ref[...]
    oh1 = (eid == i1).astype(F32)
    oh2 = (eid == i2).astype(F32)
    tot1 = jnp.sum(oh1, axis=1, keepdims=True)
    tot2 = jnp.sum(oh2, axis=1, keepdims=True)
    base = cnt_ref[...][:, 0:1]
    c1 = base + _dot(oh1.astype(BF16), tri)
    c2 = base + tot1 + _dot(oh2.astype(BF16), tri)
    r1 = jnp.sum(oh1 * c1, axis=0, keepdims=True)
    r2 = jnp.sum(oh2 * c2, axis=0, keepdims=True)
    r_ref[...] = jnp.concatenate([r1, r2], axis=0).astype(jnp.int32)
    cnt_ref[...] = cnt_ref[...] + (tot1 + tot2)


def _router(h, rw_hi, rw_lo, rb, tm=512):
    T, D = h.shape
    E = rw_hi.shape[0]
    const = lambda i: (0, 0)
    out = pl.BlockSpec((TOP_K, tm), lambda i: (0, i))
    idx = jnp.arange(tm)
    tri = (idx[:, None] < idx[None, :]).astype(BF16)
    return pl.pallas_call(
        _router_kernel,
        grid=(T // tm,),
        in_specs=[pl.BlockSpec((tm, D), lambda i: (i, 0)), pl.BlockSpec((E, D), const),
                  pl.BlockSpec((E, D), const), pl.BlockSpec((E, 1), const),
                  pl.BlockSpec((tm, tm), const)],
        out_specs=[out, out, out, pl.BlockSpec((E, V7X_LANES), const)],
        out_shape=[jax.ShapeDtypeStruct((TOP_K, T), jnp.int32),
                   jax.ShapeDtypeStruct((TOP_K, T), F32),
                   jax.ShapeDtypeStruct((TOP_K, T), jnp.int32),
                   jax.ShapeDtypeStruct((E, V7X_LANES), F32)],
        compiler_params=_params(("arbitrary",), 32 * 1024 * 1024),
        name="moe_router_top2",
    )(h, rw_hi, rw_lo, rb, tri)


def _moe_combine_ln_kernel(alpha, h_ref, ya_ref, yb_ref, w_ref, g_ref, b_ref, of_ref, ob_ref):
    w = w_ref[...]
    f = ya_ref[...] * w[:, 0:1] + yb_ref[...] * w[:, 1:2]
    y = _layer_norm_rows(alpha * h_ref[...] + f, g_ref[...], b_ref[...])
    of_ref[...] = y
    ob_ref[...] = y.astype(BF16)


def _moe_combine_ln(h, ya, yb, w, g, b, alpha, tm=256):
    T, D = h.shape
    row = pl.BlockSpec((tm, D), lambda i: (i, 0))
    vec = pl.BlockSpec((1, D), lambda i: (0, 0))
    return pl.pallas_call(
        functools.partial(_moe_combine_ln_kernel, alpha),
        grid=(T // tm,),
        in_specs=[row, row, row, pl.BlockSpec((tm, w.shape[1]), lambda i: (i, 0)), vec, vec],
        out_specs=[row, row],
        out_shape=[jax.ShapeDtypeStruct((T, D), F32), jax.ShapeDtypeStruct((T, D), BF16)],
        compiler_params=_params(("parallel",), 64 * tm * D),
        name="moe_combine_ln",
    )(h, ya, yb, w, g.reshape(1, D), b.reshape(1, D))


def _dft_tables(L):
    n2 = 2 * L
    k = jnp.arange(L, dtype=jnp.int32)[:, None]
    n = jnp.arange(L, dtype=jnp.int32)[None, :]
    ang = (((2 * k + 1) * (2 * n + 1)) % (4 * n2)).astype(F32) * (math.pi / (2 * n2))
    phi = (2 * k + 1).astype(F32) * (math.pi / (2 * n2))
    return jnp.cos(ang).astype(BF16), jnp.sin(ang).astype(BF16), jnp.cos(phi), jnp.sin(phi)


def _rope_tables(L, hd):
    rows = L // GRID_W
    row = jnp.repeat(jnp.arange(rows, dtype=F32), GRID_W)
    col = jnp.tile(jnp.arange(GRID_W, dtype=F32), rows)
    n_pairs = hd // 4
    inv = ROPE_THETA ** (-jnp.arange(n_pairs, dtype=F32) / n_pairs)
    ang = jnp.concatenate([row[:, None] * inv, col[:, None] * inv], axis=-1)
    cos, sin = jnp.cos(ang), jnp.sin(ang)
    reps = V7X_LANES // hd
    return (jnp.tile(jnp.concatenate([cos, cos], axis=-1), (1, reps)),
            jnp.tile(jnp.concatenate([-sin, sin], axis=-1), (1, reps)))


def _filter_features(L, n_emb):
    t = jnp.linspace(0.0, 1.0, L, dtype=F32)[:, None]
    bands = (n_emb - 1) // 2
    w = 2.0 * math.pi * jnp.arange(L, dtype=F32)[:, None] / L
    f = jnp.linspace(1e-4, bands - 1, bands, dtype=F32)[None, :]
    z = jnp.concatenate([t, jnp.cos(f * w), -jnp.sin(f * w)], axis=-1)
    return jnp.pad(z, ((0, 0), (0, V7X_LANES - n_emb)))


def _moe_block_rows(T):
    return min(1024, max(V7X_MXU_DIM, T // 4))


def _moe(hf, hb, router_w, router_b, wg, wu, wd, j):
    T, D = hf.shape
    E = router_w.shape[1]
    rw_hi, rw_lo = _split_bf16(router_w.T)
    top_e, top_w, rank, cnt = _router(hf, rw_hi, rw_lo, router_b.reshape(E, 1))

    tm = _moe_block_rows(T)
    counts = cnt[:, 0].astype(jnp.int32)
    padded = (counts + tm - 1) // tm * tm
    pad_end = jnp.cumsum(padded)
    pad_start = pad_end - padded
    nblk = -(-(T * TOP_K + E * (tm - 1)) // tm)
    P = nblk * tm
    n_valid = (pad_end[-1] // tm).astype(jnp.int32)
    blk_first = jnp.arange(nblk, dtype=jnp.int32) * tm
    blk_e = jnp.sum((blk_first[:, None] >= pad_end[None, :]).astype(jnp.int32), axis=1)
    blk_e = jnp.minimum(blk_e, E - 1)
    row_end = (pad_start + counts)[blk_e]
    blk_rows = jnp.where(jnp.arange(nblk) < n_valid, jnp.clip(row_end - blk_first, 0, tm), 0)
    blk_e = jnp.where(jnp.arange(nblk) < n_valid, blk_e, blk_e[jnp.maximum(n_valid - 1, 0)])
    start_of = jnp.sum(jnp.where(top_e[:, :, None] == jnp.arange(E, dtype=jnp.int32),
                                 pad_start[None, None, :], 0), axis=-1)
    dest = start_of + rank

    tok = jnp.broadcast_to(jnp.arange(T, dtype=jnp.int32), (TOP_K, T))
    src_tok = jnp.zeros((P,), jnp.int32).at[dest.reshape(-1)].set(tok.reshape(-1))
    yg = None
    blk_rows = blk_rows.astype(jnp.int32)
    for blk0 in range(0, nblk, -(-nblk // MOE_CALLS)):
        nb = min(-(-nblk // MOE_CALLS), nblk - blk0)
        xg = hb[lax.slice_in_dim(src_tok, blk0 * tm, (blk0 + nb) * tm)]
        yg = _ffn_grouped(blk_e, blk_rows, xg, wg, wu, wd, j, tm, blk0, nblk, prev=yg)
    return yg[dest[0]], yg[dest[1]], top_w.T


def kernel(x, ln_in_g, ln_in_b, w_in, hy_conv_w, hy_conv_b, hy_fw1, hy_fb1, hy_fw2, hy_fb2, hy_fw3, hy_freq, hy_bias, q_norm_g, k_norm_g, w_hy_br, w_att_br, w_o, ln_mix_g, ln_mix_b, ffn_w_gate, ffn_w_up, ffn_w_down, router_w, router_b, exp_w_gate, exp_w_up, exp_w_down, ln_ffn_g, ln_ffn_b):
    B, L, D = x.shape
    T = B * L
    depth = w_in.shape[0]
    dh = w_hy_br.shape[1]
    d_attn = w_att_br.shape[1]
    hd = q_norm_g.shape[-1]
    n_q = d_attn // hd
    d_kv = (w_in.shape[2] - 3 * dh - d_attn - 2 * D) // 2
    n_kv = d_kv // hd
    alpha = (2 * depth) ** 0.25
    qkv0 = 3 * dh
    gate0 = qkv0 + d_attn + 2 * d_kv
    n_rot = (n_q + n_kv) * hd
    assert L % GRID_W == 0 and V7X_LANES % hd == 0 and n_rot % V7X_MXU_DIM == 0

    cs_bf, ss_bf, cos_phi, sin_phi = _dft_tables(L)
    cos_t, sin_t = _rope_tables(L, hd)
    z = _filter_features(L, hy_fw1.shape[1])
    min_decay = math.log(DECAY_TARGET) / SLOW_DECAY_PCT
    max_decay = math.log(DECAY_TARGET) / FAST_DECAY_PCT
    deltas = jnp.abs(jnp.linspace(min_decay, max_decay, dh, dtype=F32)).reshape(1, dh)
    lane = jnp.arange(V7X_MXU_DIM)
    bd = jnp.where((lane[:, None] // hd) == (lane[None, :] // hd), 1.0 / hd, 0.0).astype(BF16)
    qk_scale = jnp.concatenate([jnp.full((n_q * hd,), hd ** -0.5, F32),
                                jnp.ones((n_kv * hd,), F32)]).reshape(1, n_rot)

    w_in_bf = w_in.astype(BF16)
    w_hy_bf = w_hy_br.astype(BF16)
    w_att_bf = w_att_br.astype(BF16)
    w_o_bf = w_o.astype(BF16)
    ffn_g_bf = ffn_w_gate.astype(BF16)
    ffn_u_bf = ffn_w_up.astype(BF16)
    ffn_d_bf = ffn_w_down.astype(BF16)
    conv_b3 = hy_conv_b.reshape(depth, 1, 3 * dh)
    ln_mix_g3, ln_mix_b3 = ln_mix_g.reshape(depth, 1, D), ln_mix_b.reshape(depth, 1, D)
    ln_ffn_g3, ln_ffn_b3 = ln_ffn_g.reshape(depth, 1, D), ln_ffn_b.reshape(depth, 1, D)

    hf, hb = _resid_ln(x.reshape(T, D), [], ln_in_g, ln_in_b, alpha)
    for i in range(depth):
        x0, vv = _hy_proj(hb.reshape(B, L, D), w_in_bf, hy_conv_w, conv_b3, i, dh)
        fw1p = jnp.pad(hy_fw1[i], ((0, V7X_LANES - hy_fw1.shape[1]), (0, 0)))
        hs, hdf = _filters(z, fw1p, hy_fb1[i].reshape(1, -1), hy_fw2[i], hy_fb2[i].reshape(1, -1),
                           hy_freq[i], hy_fw3[i], deltas, dh)
        kre, kim = _spectrum(cs_bf, ss_bf, cos_phi, sin_phi, hs, hdf)
        y_hy = _longconv(cs_bf, ss_bf, vv, x0, kre, kim, hy_bias[i].reshape(1, dh))
        gain = jnp.concatenate([jnp.tile(q_norm_g[i], n_q), jnp.tile(k_norm_g[i], n_kv)]).reshape(1, n_rot)
        w_qkv = lax.slice_in_dim(w_in_bf[i], qkv0, gate0, axis=1)
        q, k, v = _qkv_proj(hb, w_qkv, bd, gain, qk_scale, cos_t, sin_t, B, L, n_q, n_kv, hd)
        y_att = _attention(q, k, v)
        merged = _merge(hb, y_hy.reshape(T, dh), y_att.reshape(T, d_attn), w_in_bf, w_hy_bf, w_att_bf,
                        i, gate0)
        hf, hb = _oproj_ln(merged, w_o_bf, hf, ln_mix_g3, ln_mix_b3, i, alpha)
        j = i // 2
        if i % 2 == 0:
            hf, hb = _ffn_dense(hb, ffn_g_bf, ffn_u_bf, ffn_d_bf, hf, ln_ffn_g3, ln_ffn_b3, j, i, alpha)
        else:
            ya, yb, wts = _moe(hf, hb, router_w[j], router_b[j], exp_w_gate, exp_w_up, exp_w_down, j)
            hf, hb = _moe_combine_ln(hf, ya, yb, wts, ln_ffn_g[i], ln_ffn_b[i], alpha)
    return hf.reshape(B, L, D)
```

```python
import functools
import math

import jax
import jax.numpy as jnp
from jax import lax
from jax.experimental import pallas as pl
from jax.experimental.pallas import tpu as pltpu

F32 = jnp.float32
BF16 = jnp.bfloat16

GRID_W = 64
ROPE_THETA = 10000.0
TOP_K = 2
DECAY_TARGET = 1e-2
FAST_DECAY_PCT = 0.3
SLOW_DECAY_PCT = 1.5
LN_EPS = 1e-5
RMS_EPS = 1e-6

V7X_VMEM_BYTES = 64 * 1024 * 1024
V7X_LANES = 128
V7X_MXU_DIM = 256
VMEM_BUDGET = V7X_VMEM_BYTES - 8 * 1024 * 1024
ATTN_SUBTILE = 512
EPILOGUE_SUBTILE = V7X_MXU_DIM
MOE_CALLS = 4
MOE_ROW_STEP = V7X_MXU_DIM


def _params(semantics, vmem_bytes):
    return pltpu.CompilerParams(dimension_semantics=semantics,
                                vmem_limit_bytes=min(int(vmem_bytes), VMEM_BUDGET))


def _dot(a, b):
    return jnp.dot(a, b, preferred_element_type=F32)


def _split_bf16(a):
    hi = a.astype(BF16)
    lo = (a - hi.astype(F32)).astype(BF16)
    return hi, lo


def _dot3(a_hi, a_lo, b_hi, b_lo):
    return _dot(a_hi, b_hi) + (_dot(a_hi, b_lo) + _dot(a_lo, b_hi))


def _layer_norm_rows(x, g, b):
    mu = jnp.mean(x, axis=-1, keepdims=True)
    xc = x - mu
    var = jnp.mean(xc * xc, axis=-1, keepdims=True)
    return xc * lax.rsqrt(var + LN_EPS) * g + b


def _resid_ln_kernel(alpha, n_add, *refs):
    h_ref = refs[0]
    add_refs = refs[1:1 + n_add]
    g_ref, b_ref, of_ref, ob_ref = refs[1 + n_add:]
    x = h_ref[...]
    if n_add:
        acc = add_refs[0][...]
        for r in add_refs[1:]:
            acc = acc + r[...]
        x = alpha * x + acc
    y = _layer_norm_rows(x, g_ref[...], b_ref[...])
    of_ref[...] = y
    ob_ref[...] = y.astype(BF16)


def _resid_ln(h, addends, g, b, alpha, tm=256):
    T, D = h.shape
    n_add = len(addends)
    row = pl.BlockSpec((tm, D), lambda i: (i, 0))
    vec = pl.BlockSpec((1, D), lambda i: (0, 0))
    return pl.pallas_call(
        functools.partial(_resid_ln_kernel, alpha, n_add),
        grid=(T // tm,),
        in_specs=[row] * (1 + n_add) + [vec, vec],
        out_specs=[row, row],
        out_shape=[jax.ShapeDtypeStruct((T, D), F32), jax.ShapeDtypeStruct((T, D), BF16)],
        compiler_params=_params(("parallel",), (2 * (1 + n_add) * 4 + 12 + 16) * tm * D),
        name="resid_ln",
    )(h, *addends, g.reshape(1, D), b.reshape(1, D))


def _hy_proj_kernel(h_ref, w0_ref, w1_ref, w2_ref, cw0_ref, cw1_ref, cw2_ref,
                    cb0_ref, cb1_ref, cb2_ref, x0_ref, vv_ref):
    h = h_ref[...]
    L = h.shape[0]
    row = lax.broadcasted_iota(jnp.int32, (L, 1), 0)

    def stream(w_ref, cw_ref, cb_ref):
        p = _dot(h, w_ref[...])
        prev = jnp.where(row == 0, 0.0, pltpu.roll(p, 1, 0))
        nxt = jnp.where(row == L - 1, 0.0, pltpu.roll(p, L - 1, 0))
        cw = cw_ref[...]
        return prev * cw[0:1] + p * cw[1:2] + nxt * cw[2:3] + cb_ref[...]

    x0_ref[...] = stream(w0_ref, cw0_ref, cb0_ref)
    vv_ref[...] = stream(w1_ref, cw1_ref, cb1_ref) * stream(w2_ref, cw2_ref, cb2_ref)


def _hy_proj(hb3, w_in_bf, conv_w, conv_b3, layer, dh, tn=256):
    B, L, D = hb3.shape
    nj = dh // tn

    def wspec(s):
        return pl.BlockSpec((None, D, tn), lambda j, b: (layer, 0, s * nj + j))

    def cwspec(s):
        return pl.BlockSpec((None, conv_w.shape[1], tn), lambda j, b: (layer, 0, s * nj + j))

    def cbspec(s):
        return pl.BlockSpec((None, 1, tn), lambda j, b: (layer, 0, s * nj + j))

    out = pl.BlockSpec((None, L, tn), lambda j, b: (b, 0, j))
    return pl.pallas_call(
        _hy_proj_kernel,
        grid=(nj, B),
        in_specs=[pl.BlockSpec((None, L, D), lambda j, b: (b, 0, 0)),
                  wspec(0), wspec(1), wspec(2), cwspec(0), cwspec(1), cwspec(2),
                  cbspec(0), cbspec(1), cbspec(2)],
        out_specs=[out, out],
        out_shape=[jax.ShapeDtypeStruct((B, L, dh), F32)] * 2,
        compiler_params=_params(("parallel", "parallel"),
                                4 * L * D + 12 * D * tn + 16 * L * tn + 40 * L * tn),
        name="hyena_proj_conv",
    )(hb3, w_in_bf, w_in_bf, w_in_bf, conv_w, conv_w, conv_w, conv_b3, conv_b3, conv_b3)


def _qkv_kernel(n_q, n_kv, hd, x_ref, w_ref, bd_ref, gain_ref, scale_ref, cos_ref, sin_ref,
                q_ref, k_ref, v_ref):
    n_rot = (n_q + n_kv) * hd
    tm = x_ref.shape[0]
    ts = min(tm, EPILOGUE_SUBTILE)
    w = w_ref[...]
    bd = bd_ref[...]
    gw = bd.shape[0]
    reps = n_rot // V7X_LANES
    lane = lax.broadcasted_iota(jnp.int32, (1, n_rot), 1)
    first_half = (lane % hd) < (hd // 2)
    half = hd // 2
    pad_lane = lax.broadcasted_iota(jnp.int32, (ts, V7X_LANES - hd), 1)
    ones_col = jnp.where(pad_lane == 0, 1.0, 0.0).astype(BF16)
    for r0 in range(0, tm, ts):
        rows = slice(r0, r0 + ts)
        p = _dot(x_ref[rows, :], w)
        pr = p[:, :n_rot]
        sq = (pr * pr).astype(BF16)
        ms = jnp.concatenate([_dot(sq[:, c:c + gw], bd) for c in range(0, n_rot, gw)], axis=1)
        xn = pr * lax.rsqrt(ms + RMS_EPS) * gain_ref[...]
        cos = jnp.concatenate([cos_ref[rows, :]] * reps, axis=1)
        sin = jnp.concatenate([sin_ref[rows, :]] * reps, axis=1)
        swapped = jnp.where(first_half, pltpu.roll(xn, n_rot - half, 1), pltpu.roll(xn, half, 1))
        rot = ((xn * cos + swapped * sin) * scale_ref[...]).astype(BF16)
        for hh in range(n_q):
            q_ref[hh, rows, :] = rot[:, hh * hd:(hh + 1) * hd]
        for hh in range(n_kv):
            k_ref[hh, rows, :] = rot[:, (n_q + hh) * hd:(n_q + hh + 1) * hd]
            v_h = p[:, n_rot + hh * hd:n_rot + (hh + 1) * hd].astype(BF16)
            v_ref[hh, rows, :] = jnp.concatenate([v_h, ones_col], axis=1)


def _qkv_proj(hb, w_qkv, bd, gain, scale, cos_t, sin_t, B, L, n_q, n_kv, hd, tm=1024):
    T, D = hb.shape
    nw = w_qkv.shape[1]
    n_rot = (n_q + n_kv) * hd
    tm = min(tm, L)
    tpb = L // tm
    const = lambda i: (0, 0)
    return pl.pallas_call(
        functools.partial(_qkv_kernel, n_q, n_kv, hd),
        grid=(T // tm,),
        in_specs=[pl.BlockSpec((tm, D), lambda i: (i, 0)),
                  pl.BlockSpec((D, nw), const),
                  pl.BlockSpec(bd.shape, const),
                  pl.BlockSpec((1, n_rot), const),
                  pl.BlockSpec((1, n_rot), const),
                  pl.BlockSpec((tm, V7X_LANES), lambda i: (i % tpb, 0)),
                  pl.BlockSpec((tm, V7X_LANES), lambda i: (i % tpb, 0))],
        out_specs=[pl.BlockSpec((None, n_q, tm, hd), lambda i: (i // tpb, 0, i % tpb, 0)),
                   pl.BlockSpec((None, n_kv, tm, hd), lambda i: (i // tpb, 0, i % tpb, 0)),
                   pl.BlockSpec((None, n_kv, tm, V7X_LANES), lambda i: (i // tpb, 0, i % tpb, 0))],
        out_shape=[jax.ShapeDtypeStruct((B, n_q, L, hd), BF16),
                   jax.ShapeDtypeStruct((B, n_kv, L, hd), BF16),
                   jax.ShapeDtypeStruct((B, n_kv, L, V7X_LANES), BF16)],
        compiler_params=_params(("parallel",),
                                4 * tm * D + 4 * D * nw + 8 * tm * (n_q + 2 * n_kv) * V7X_LANES
                                + 48 * tm * nw),
        name="qkv_proj_norm_rope",
    )(hb, w_qkv, bd, gain, scale, cos_t, sin_t)


def _attn_kernel(group, hd, q_ref, k_ref, v_ref, o_ref):
    k = k_ref[...]
    v = v_ref[...]
    tq = q_ref.shape[1]
    ts = min(tq, ATTN_SUBTILE)
    for r0 in range(0, tq, ts):
        for g in range(group):
            s = lax.dot_general(q_ref[g, r0:r0 + ts, :], k, (((1,), (1,)), ((), ())),
                                preferred_element_type=F32)
            m = jnp.max(s, axis=-1, keepdims=True)
            e = jnp.exp(s - m).astype(BF16)
            o = _dot(e, v)
            o_ref[r0:r0 + ts, g * hd:(g + 1) * hd] = (o[:, :hd] / o[:, hd:hd + 1]).astype(o_ref.dtype)


def _attention(q, k, v, tq=2048):
    B, n_q, L, hd = q.shape
    n_kv = k.shape[1]
    group = n_q // n_kv
    tq = min(tq, L)
    return pl.pallas_call(
        functools.partial(_attn_kernel, group, hd),
        grid=(B, n_kv, L // tq),
        in_specs=[pl.BlockSpec((None, group, tq, hd), lambda b, j, t: (b, j, t, 0)),
                  pl.BlockSpec((None, None, L, hd), lambda b, j, t: (b, j, 0, 0)),
                  pl.BlockSpec((None, None, L, v.shape[-1]), lambda b, j, t: (b, j, 0, 0))],
        out_specs=pl.BlockSpec((None, tq, group * hd), lambda b, j, t: (b, t, j)),
        out_shape=jax.ShapeDtypeStruct((B, L, n_q * hd), BF16),
        compiler_params=_params(("parallel", "parallel", "parallel"),
                                8 * L * V7X_LANES + 8 * group * tq * V7X_LANES + 40 * tq * L),
        name="gqa_attention",
    )(q, k, v)


def _filter_kernel(z_ref, fw1_ref, fb1_ref, fw2_ref, fb2_ref, fr_ref, w3f_ref, w3b_ref, dl_ref,
                   hs_ref, hd_ref):
    z = z_ref[...]
    L = z.shape[0]
    fr = fr_ref[...]

    def hp_dot(a, b):
        a_hi, a_lo = _split_bf16(a)
        b_hi, b_lo = _split_bf16(b)
        return _dot3(a_hi, a_lo, b_hi, b_lo)

    a = jnp.sin(fr[0:1] * (hp_dot(z, fw1_ref[...]) + fb1_ref[...]))
    a = jnp.sin(fr[1:2] * (hp_dot(a, fw2_ref[...]) + fb2_ref[...]))
    window = jnp.exp(-z[:, 0:1] * dl_ref[...])
    h_fwd = hp_dot(a, w3f_ref[...]) * window
    h_bwd = hp_dot(a, w3b_ref[...]) * window
    row = lax.broadcasted_iota(jnp.int32, (L, 1), 0)
    h_bwd0 = jnp.where(row == 0, 0.0, h_bwd)
    hs_ref[...] = h_fwd + h_bwd0
    hd_ref[...] = h_bwd0 - h_fwd


def _filters(z, fw1p, fb1, fw2, fb2, fr, fw3, deltas, dh, tn=256):
    L, zw = z.shape
    nh = fw2.shape[0]
    nj = dh // tn
    const = lambda j: (0, 0)
    out = pl.BlockSpec((L, tn), lambda j: (0, j))
    return pl.pallas_call(
        _filter_kernel,
        grid=(nj,),
        in_specs=[pl.BlockSpec((L, zw), const), pl.BlockSpec((zw, nh), const),
                  pl.BlockSpec((1, nh), const), pl.BlockSpec((nh, nh), const),
                  pl.BlockSpec((1, nh), const), pl.BlockSpec((2, nh), const),
                  pl.BlockSpec((nh, tn), lambda j: (0, j)),
                  pl.BlockSpec((nh, tn), lambda j: (0, nj + j)),
                  pl.BlockSpec((1, tn), lambda j: (0, j))],
        out_specs=[out, out],
        out_shape=[jax.ShapeDtypeStruct((L, dh), F32)] * 2,
        compiler_params=_params(("parallel",), 32 * 1024 * 1024),
        name="hyena_filters",
    )(z, fw1p, fb1, fw2, fb2, fr, fw3, fw3, deltas)


def _spectrum_kernel(cs_ref, ss_ref, cp_ref, sp_ref, hs_ref, hd_ref, kre_ref, kim_ref):
    cs = cs_ref[...]
    ss = ss_ref[...]
    hs = hs_ref[...].astype(BF16)
    hd = hd_ref[...].astype(BF16)
    cp = cp_ref[...]
    sp = sp_ref[...]
    kre_ref[...] = cp * _dot(cs, hs) + sp * _dot(ss, hs)
    kim_ref[...] = cp * _dot(ss, hd) - sp * _dot(cs, hd)


def _spectrum(cs, ss, cos_phi, sin_phi, hs, hd, tk=512, tn=256):
    L, dh = hs.shape
    tk = min(tk, L)
    mat = pl.BlockSpec((tk, L), lambda j, i: (i, 0))
    vec = pl.BlockSpec((tk, 1), lambda j, i: (i, 0))
    col = pl.BlockSpec((L, tn), lambda j, i: (0, j))
    out = pl.BlockSpec((tk, tn), lambda j, i: (i, j))
    return pl.pallas_call(
        _spectrum_kernel,
        grid=(dh // tn, L // tk),
        in_specs=[mat, mat, vec, vec, col, col],
        out_specs=[out, out],
        out_shape=[jax.ShapeDtypeStruct((L, dh), F32)] * 2,
        compiler_params=_params(("parallel", "parallel"), 40 * 1024 * 1024),
        name="filter_spectrum",
    )(cs, ss, cos_phi, sin_phi, hs, hd)


def _longconv_kernel(inv_scale, cs_ref, ss_ref, v_ref, x0_ref, kre_ref, kim_ref, bias_ref, y_ref):
    cs = cs_ref[...]
    ss = ss_ref[...]
    v = v_ref[...]
    vb = v.astype(BF16)
    xc = _dot(cs, vb)
    xs = _dot(ss, vb)
    kre = kre_ref[...]
    kim = kim_ref[...]
    yre = (xc * kre + xs * kim).astype(BF16)
    yim = (xc * kim - xs * kre).astype(BF16)
    conv = (_dot(cs, yre) - _dot(ss, yim)) * inv_scale
    y_ref[...] = (x0_ref[...] * (conv + bias_ref[...] * v)).astype(y_ref.dtype)


def _longconv(cs, ss, vv, x0, kre, kim, bias, tc=256):
    B, L, dh = vv.shape
    mat = pl.BlockSpec((L, L), lambda j, b: (0, 0), pipeline_mode=pl.Buffered(1))
    act = pl.BlockSpec((None, L, tc), lambda j, b: (b, 0, j))
    spec = pl.BlockSpec((L, tc), lambda j, b: (0, j))
    return pl.pallas_call(
        functools.partial(_longconv_kernel, 1.0 / L),
        grid=(dh // tc, B),
        in_specs=[mat, mat, act, act, spec, spec, pl.BlockSpec((1, tc), lambda j, b: (0, j))],
        out_specs=act,
        out_shape=jax.ShapeDtypeStruct((B, L, dh), BF16),
        compiler_params=_params(("parallel", "parallel"),
                                4 * L * L + 16 * L * tc + 16 * L * tc + 4 * L * tc + 40 * L * tc),
        name="hyena_longconv",
    )(cs, ss, vv, x0, kre, kim, bias)


def _merge_kernel(h_ref, yh_ref, ya_ref, wgh_ref, wga_ref, wh_ref, wa_ref, o_ref):
    h = h_ref[...]
    gh = jax.nn.sigmoid(_dot(h, wgh_ref[...]))
    ga = jax.nn.sigmoid(_dot(h, wga_ref[...]))
    o_ref[...] = (gh * _dot(yh_ref[...], wh_ref[...])
                  + ga * _dot(ya_ref[...], wa_ref[...])).astype(o_ref.dtype)


def _merge(hb, y_hy, y_att, w_in_bf, w_hy, w_att, layer, gate_col0, tm=1024, tn=512):
    T, D = hb.shape
    dh, da = y_hy.shape[1], y_att.shape[1]
    tm = min(tm, T)
    g0 = gate_col0 // tn
    nj = D // tn
    return pl.pallas_call(
        _merge_kernel,
        grid=(T // tm, nj),
        in_specs=[pl.BlockSpec((tm, D), lambda i, j: (i, 0)),
                  pl.BlockSpec((tm, dh), lambda i, j: (i, 0)),
                  pl.BlockSpec((tm, da), lambda i, j: (i, 0)),
                  pl.BlockSpec((None, D, tn), lambda i, j: (layer, 0, g0 + j)),
                  pl.BlockSpec((None, D, tn), lambda i, j: (layer, 0, g0 + nj + j)),
                  pl.BlockSpec((None, dh, tn), lambda i, j: (layer, 0, j)),
                  pl.BlockSpec((None, da, tn), lambda i, j: (layer, 0, j))],
        out_specs=pl.BlockSpec((tm, tn), lambda i, j: (i, j)),
        out_shape=jax.ShapeDtypeStruct((T, D), BF16),
        compiler_params=_params(("parallel", "parallel"),
                                4 * tm * (D + dh + da) + 4 * tn * (2 * D + dh + da) + 4 * tm * tn
                                + 32 * tm * tn),
        name="gated_merge",
    )(hb, y_hy, y_att, w_in_bf, w_in_bf, w_hy, w_att)


def _oproj_ln_kernel(alpha, x_ref, w_ref, h_ref, g_ref, b_ref, of_ref, ob_ref):
    tm = x_ref.shape[0]
    ts = min(tm, EPILOGUE_SUBTILE)
    w = w_ref[...]
    for r0 in range(0, tm, ts):
        y = alpha * h_ref[r0:r0 + ts, :] + _dot(x_ref[r0:r0 + ts, :], w)
        y = _layer_norm_rows(y, g_ref[...], b_ref[...])
        of_ref[r0:r0 + ts, :] = y
        ob_ref[r0:r0 + ts, :] = y.astype(BF16)


def _oproj_ln(xb, w_o, h, g, b, layer, alpha, tm=512):
    T, D = h.shape
    row = lambda i: (i, 0)
    vec = pl.BlockSpec((None, 1, D), lambda i: (layer, 0, 0))
    return pl.pallas_call(
        functools.partial(_oproj_ln_kernel, alpha),
        grid=(T // tm,),
        in_specs=[pl.BlockSpec((tm, D), row),
                  pl.BlockSpec((None, D, D), lambda i: (layer, 0, 0)),
                  pl.BlockSpec((tm, D), row), vec, vec],
        out_specs=[pl.BlockSpec((tm, D), row), pl.BlockSpec((tm, D), row)],
        out_shape=[jax.ShapeDtypeStruct((T, D), F32), jax.ShapeDtypeStruct((T, D), BF16)],
        compiler_params=_params(("parallel",), 4 * D * D + 24 * tm * D + 24 * tm * D),
        name="oproj_resid_ln",
    )(xb, w_o, h, g, b)


def _swiglu_partial(x_ref, wg_ref, wu_ref, wd_ref):
    x = x_ref[...]
    g = _dot(x, wg_ref[...].astype(BF16))
    u = _dot(x, wu_ref[...].astype(BF16))
    a = (g * jax.nn.sigmoid(g) * u).astype(BF16)
    return _dot(a, wd_ref[...].astype(BF16))


def _ffn_dense_kernel(alpha, x_ref, wg_ref, wu_ref, wd_ref, h_ref, g_ref, b_ref,
                      of_ref, ob_ref, acc_ref):
    f = pl.program_id(1)

    @pl.when(f == 0)
    def _():
        acc_ref[...] = jnp.zeros_like(acc_ref)

    acc_ref[...] += _swiglu_partial(x_ref, wg_ref, wu_ref, wd_ref)

    @pl.when(f == pl.num_programs(1) - 1)
    def _():
        y = _layer_norm_rows(alpha * h_ref[...] + acc_ref[...], g_ref[...], b_ref[...])
        of_ref[...] = y
        ob_ref[...] = y.astype(BF16)


def _ffn_dense(xb, wg, wu, wd, h, g, b, j, layer, alpha, tm=512, tf=512):
    T, D = xb.shape
    F = wg.shape[-1]
    row = lambda i, f: (i, 0)
    vec = pl.BlockSpec((None, 1, D), lambda i, f: (layer, 0, 0))
    return pl.pallas_call(
        functools.partial(_ffn_dense_kernel, alpha),
        grid=(T // tm, F // tf),
        in_specs=[pl.BlockSpec((tm, D), row),
                  pl.BlockSpec((None, D, tf), lambda i, f: (j, 0, f)),
                  pl.BlockSpec((None, D, tf), lambda i, f: (j, 0, f)),
                  pl.BlockSpec((None, tf, D), lambda i, f: (j, f, 0)),
                  pl.BlockSpec((tm, D), row), vec, vec],
        out_specs=[pl.BlockSpec((tm, D), row), pl.BlockSpec((tm, D), row)],
        out_shape=[jax.ShapeDtypeStruct((T, D), F32), jax.ShapeDtypeStruct((T, D), BF16)],
        scratch_shapes=[pltpu.VMEM((tm, D), F32)],
        compiler_params=_params(("parallel", "arbitrary"),
                                4 * tm * D + 12 * D * tf + 8 * tm * D + 12 * tm * D + 4 * tm * D
                                + 16 * tm * tf + 8 * tm * D),
        name="ffn_dense_swiglu_ln",
    )(xb, wg, wu, wd, h, g, b)


def _ffn_group_kernel(blk0, be_ref, rows_ref, x_ref, wg_ref, wu_ref, wd_ref, *rest):
    o_ref = rest[-1]
    f = pl.program_id(1)
    n_rows = rows_ref[blk0 + pl.program_id(0)]
    tm = x_ref.shape[0]

    @pl.when(jnp.logical_and(n_rows > 0, f == 0))
    def _():
        o_ref[...] = jnp.zeros_like(o_ref)

    step = min(tm, MOE_ROW_STEP)
    for hi in range(step, tm + 1, step):
        @pl.when(jnp.logical_and(n_rows > hi - step, n_rows <= hi))
        def _(hi=hi):
            o_ref[0:hi, :] += _swiglu_partial(x_ref.at[0:hi], wg_ref, wu_ref, wd_ref)


def _ffn_grouped(blk_e, blk_rows, xg, wg, wu, wd, j, tm, blk0, nblk, prev=None, tf=512):
    D = xg.shape[1]
    F = wg.shape[-1]
    nf = F // tf
    nb = xg.shape[0] // tm

    def live(i, rows):
        return rows[blk0 + i] > 0

    def x_rows(i, f, be, nv):
        return (jnp.where(live(i, nv), i, 0), 0)

    def out_rows(i, f, be, nv):
        return (jnp.where(live(i, nv), blk0 + i, nblk), 0)

    def hidden(i, f, nv):
        return jnp.where(live(i, nv), f, nf - 1)

    in_specs = [pl.BlockSpec((tm, D), x_rows),
                pl.BlockSpec((None, None, D, tf),
                             lambda i, f, be, nv: (j, be[blk0 + i], 0, hidden(i, f, nv))),
                pl.BlockSpec((None, None, D, tf),
                             lambda i, f, be, nv: (j, be[blk0 + i], 0, hidden(i, f, nv))),
                pl.BlockSpec((None, None, tf, D),
                             lambda i, f, be, nv: (j, be[blk0 + i], hidden(i, f, nv), 0))]
    args = [blk_e, blk_rows, xg, wg, wu, wd]
    aliases = {}
    if prev is not None:
        in_specs.append(pl.BlockSpec(memory_space=pl.ANY))
        aliases = {len(args): 0}
        args.append(prev)
    return pl.pallas_call(
        functools.partial(_ffn_group_kernel, blk0),
        grid_spec=pltpu.PrefetchScalarGridSpec(
            num_scalar_prefetch=2,
            grid=(nb, nf),
            in_specs=in_specs,
            out_specs=pl.BlockSpec((tm, D), out_rows)),
        out_shape=jax.ShapeDtypeStruct(((nblk + 1) * tm, D), F32),
        input_output_aliases=aliases,
        compiler_params=_params(("arbitrary", "arbitrary"),
                                4 * tm * D + 24 * D * tf + 8 * tm * D
                                + 6 * D * tf + 16 * tm * tf + 8 * tm * D),
        name="moe_grouped_swiglu",
    )(*args)


def _router_kernel(x_ref, wh_ref, wl_ref, b_ref, tri_ref, e_ref, w_ref, r_ref, cnt_ref):
    @pl.when(pl.program_id(0) == 0)
    def _():
        cnt_ref[...] = jnp.zeros_like(cnt_ref)

    x_hi, x_lo = _split_bf16(x_ref[...])
    nt = (((1,), (1,)), ((), ()))

    def dg(a, b):
        return lax.dot_general(a, b, nt, preferred_element_type=F32)

    wh = wh_ref[...]
    wl = wl_ref[...]
    logits = dg(wh, x_hi) + (dg(wh, x_lo) + dg(wl, x_hi)) + b_ref[...]
    n_e = logits.shape[0]
    eid = lax.broadcasted_iota(jnp.int32, logits.shape, 0)
    m1 = jnp.max(logits, axis=0, keepdims=True)
    i1 = jnp.min(jnp.where(logits == m1, eid, n_e), axis=0, keepdims=True)
    rest = jnp.where(eid == i1, -jnp.inf, logits)
    m2 = jnp.max(rest, axis=0, keepdims=True)
    i2 = jnp.min(jnp.where(rest == m2, eid, n_e), axis=0, keepdims=True)
    t = jnp.exp(m2 - m1)
    w1 = 1.0 / (1.0 + t)
    e_ref[...] = jnp.concatenate([i1, i2], axis=0)
    w_ref[...] = jnp.concatenate([w1, t * w1], axis=0)

    tri = tri_ref[...]
    oh1 = (eid == i1).astype(F32)
    oh2 = (eid == i2).astype(F32)
    tot1 = jnp.sum(oh1, axis=1, keepdims=True)
    tot2 = jnp.sum(oh2, axis=1, keepdims=True)
    base = cnt_ref[...][:, 0:1]
    c1 = base + _dot(oh1.astype(BF16), tri)
    c2 = base + tot1 + _dot(oh2.astype(BF16), tri)
    r1 = jnp.sum(oh1 * c1, axis=0, keepdims=True)
    r2 = jnp.sum(oh2 * c2, axis=0, keepdims=True)
    r_ref[...] = jnp.concatenate([r1, r2], axis=0).astype(jnp.int32)
    cnt_ref[...] = cnt_ref[...] + (tot1 + tot2)


def _router(h, rw_hi, rw_lo, rb, tm=512):
    T, D = h.shape
    E = rw_hi.shape[0]
    const = lambda i: (0, 0)
    out = pl.BlockSpec((TOP_K, tm), lambda i: (0, i))
    idx = jnp.arange(tm)
    tri = (idx[:, None] < idx[None, :]).astype(BF16)
    return pl.pallas_call(
        _router_kernel,
        grid=(T // tm,),
        in_specs=[pl.BlockSpec((tm, D), lambda i: (i, 0)), pl.BlockSpec((E, D), const),
                  pl.BlockSpec((E, D), const), pl.BlockSpec((E, 1), const),
                  pl.BlockSpec((tm, tm), const)],
        out_specs=[out, out, out, pl.BlockSpec((E, V7X_LANES), const)],
        out_shape=[jax.ShapeDtypeStruct((TOP_K, T), jnp.int32),
                   jax.ShapeDtypeStruct((TOP_K, T), F32),
                   jax.ShapeDtypeStruct((TOP_K, T), jnp.int32),
                   jax.ShapeDtypeStruct((E, V7X_LANES), F32)],
        compiler_params=_params(("arbitrary",), 32 * 1024 * 1024),
        name="moe_router_top2",
    )(h, rw_hi, rw_lo, rb, tri)


def _invert_kernel(tt, dest_ref, src_ref):
    base = pl.program_id(0) * tt

    @pl.when(pl.program_id(0) == 0)
    def _():
        def clear(p, carry):
            src_ref[p] = 0
            return carry
        lax.fori_loop(0, src_ref.shape[0], clear, 0, unroll=16)

    def body(r, carry):
        for k in range(TOP_K):
            src_ref[dest_ref[k, r]] = base + r
        return carry

    lax.fori_loop(0, tt, body, 0, unroll=8)


def _invert_dest(dest, P, tt=1024):
    T = dest.shape[1]
    tt = min(tt, T)
    nt = T // tt
    dest3 = dest.reshape(TOP_K, nt, tt).transpose(1, 0, 2)
    return pl.pallas_call(
        functools.partial(_invert_kernel, tt),
        grid=(nt,),
        in_specs=[pl.BlockSpec((None, TOP_K, tt), lambda i: (i, 0, 0), memory_space=pltpu.SMEM)],
        out_specs=pl.BlockSpec(memory_space=pltpu.SMEM),
        out_shape=jax.ShapeDtypeStruct((P,), jnp.int32),
        compiler_params=_params(("arbitrary",), 16 * 1024 * 1024),
        name="moe_invert_dest",
    )(dest3)


def _moe_combine_ln_kernel(alpha, h_ref, ya_ref, yb_ref, w_ref, g_ref, b_ref, of_ref, ob_ref):
    w = w_ref[...]
    f = ya_ref[...] * w[:, 0:1] + yb_ref[...] * w[:, 1:2]
    y = _layer_norm_rows(alpha * h_ref[...] + f, g_ref[...], b_ref[...])
    of_ref[...] = y
    ob_ref[...] = y.astype(BF16)


def _moe_combine_ln(h, ya, yb, w, g, b, alpha, tm=256):
    T, D = h.shape
    row = pl.BlockSpec((tm, D), lambda i: (i, 0))
    vec = pl.BlockSpec((1, D), lambda i: (0, 0))
    return pl.pallas_call(
        functools.partial(_moe_combine_ln_kernel, alpha),
        grid=(T // tm,),
        in_specs=[row, row, row, pl.BlockSpec((tm, w.shape[1]), lambda i: (i, 0)), vec, vec],
        out_specs=[row, row],
        out_shape=[jax.ShapeDtypeStruct((T, D), F32), jax.ShapeDtypeStruct((T, D), BF16)],
        compiler_params=_params(("parallel",), 64 * tm * D),
        name="moe_combine_ln",
    )(h, ya, yb, w, g.reshape(1, D), b.reshape(1, D))


def _dft_tables(L):
    n2 = 2 * L
    k = jnp.arange(L, dtype=jnp.int32)[:, None]
    n = jnp.arange(L, dtype=jnp.int32)[None, :]
    ang = (((2 * k + 1) * (2 * n + 1)) % (4 * n2)).astype(F32) * (math.pi / (2 * n2))
    phi = (2 * k + 1).astype(F32) * (math.pi / (2 * n2))
    return jnp.cos(ang).astype(BF16), jnp.sin(ang).astype(BF16), jnp.cos(phi), jnp.sin(phi)


def _rope_tables(L, hd):
    rows = L // GRID_W
    row = jnp.repeat(jnp.arange(rows, dtype=F32), GRID_W)
    col = jnp.tile(jnp.arange(GRID_W, dtype=F32), rows)
    n_pairs = hd // 4
    inv = ROPE_THETA ** (-jnp.arange(n_pairs, dtype=F32) / n_pairs)
    ang = jnp.concatenate([row[:, None] * inv, col[:, None] * inv], axis=-1)
    cos, sin = jnp.cos(ang), jnp.sin(ang)
    reps = V7X_LANES // hd
    return (jnp.tile(jnp.concatenate([cos, cos], axis=-1), (1, reps)),
            jnp.tile(jnp.concatenate([-sin, sin], axis=-1), (1, reps)))


def _filter_features(L, n_emb):
    t = jnp.linspace(0.0, 1.0, L, dtype=F32)[:, None]
    bands = (n_emb - 1) // 2
    w = 2.0 * math.pi * jnp.arange(L, dtype=F32)[:, None] / L
    f = jnp.linspace(1e-4, bands - 1, bands, dtype=F32)[None, :]
    z = jnp.concatenate([t, jnp.cos(f * w), -jnp.sin(f * w)], axis=-1)
    return jnp.pad(z, ((0, 0), (0, V7X_LANES - n_emb)))


def _moe_block_rows(T):
    return min(1024, max(V7X_MXU_DIM, T // 4))


def _moe(hf, hb, router_w, router_b, wg, wu, wd, j):
    T, D = hf.shape
    E = router_w.shape[1]
    rw_hi, rw_lo = _split_bf16(router_w.T)
    top_e, top_w, rank, cnt = _router(hf, rw_hi, rw_lo, router_b.reshape(E, 1))

    tm = _moe_block_rows(T)
    counts = cnt[:, 0].astype(jnp.int32)
    padded = (counts + tm - 1) // tm * tm
    pad_end = jnp.cumsum(padded)
    pad_start = pad_end - padded
    nblk = -(-(T * TOP_K + E * (tm - 1)) // tm)
    P = nblk * tm
    n_valid = (pad_end[-1] // tm).astype(jnp.int32)
    blk_first = jnp.arange(nblk, dtype=jnp.int32) * tm
    blk_e = jnp.sum((blk_first[:, None] >= pad_end[None, :]).astype(jnp.int32), axis=1)
    blk_e = jnp.minimum(blk_e, E - 1)
    row_end = (pad_start + counts)[blk_e]
    blk_rows = jnp.where(jnp.arange(nblk) < n_valid, jnp.clip(row_end - blk_first, 0, tm), 0)
    blk_e = jnp.where(jnp.arange(nblk) < n_valid, blk_e, blk_e[jnp.maximum(n_valid - 1, 0)])
    start_of = jnp.sum(jnp.where(top_e[:, :, None] == jnp.arange(E, dtype=jnp.int32),
                                 pad_start[None, None, :], 0), axis=-1)
    dest = start_of + rank

    src_tok = _invert_dest(dest, P)
    yg = None
    blk_rows = blk_rows.astype(jnp.int32)
    for blk0 in range(0, nblk, -(-nblk // MOE_CALLS)):
        nb = min(-(-nblk // MOE_CALLS), nblk - blk0)
        xg = hb[lax.slice_in_dim(src_tok, blk0 * tm, (blk0 + nb) * tm)]
        yg = _ffn_grouped(blk_e, blk_rows, xg, wg, wu, wd, j, tm, blk0, nblk, prev=yg)
    return yg[dest[0]], yg[dest[1]], top_w.T


def kernel(x, ln_in_g, ln_in_b, w_in, hy_conv_w, hy_conv_b, hy_fw1, hy_fb1, hy_fw2, hy_fb2, hy_fw3, hy_freq, hy_bias, q_norm_g, k_norm_g, w_hy_br, w_att_br, w_o, ln_mix_g, ln_mix_b, ffn_w_gate, ffn_w_up, ffn_w_down, router_w, router_b, exp_w_gate, exp_w_up, exp_w_down, ln_ffn_g, ln_ffn_b):
    B, L, D = x.shape
    T = B * L
    depth = w_in.shape[0]
    dh = w_hy_br.shape[1]
    d_attn = w_att_br.shape[1]
    hd = q_norm_g.shape[-1]
    n_q = d_attn // hd
    d_kv = (w_in.shape[2] - 3 * dh - d_attn - 2 * D) // 2
    n_kv = d_kv // hd
    alpha = (2 * depth) ** 0.25
    qkv0 = 3 * dh
    gate0 = qkv0 + d_attn + 2 * d_kv
    n_rot = (n_q + n_kv) * hd
    assert L % GRID_W == 0 and V7X_LANES % hd == 0 and n_rot % V7X_MXU_DIM == 0

    cs_bf, ss_bf, cos_phi, sin_phi = _dft_tables(L)
    cos_t, sin_t = _rope_tables(L, hd)
    z = _filter_features(L, hy_fw1.shape[1])
    min_decay = math.log(DECAY_TARGET) / SLOW_DECAY_PCT
    max_decay = math.log(DECAY_TARGET) / FAST_DECAY_PCT
    deltas = jnp.abs(jnp.linspace(min_decay, max_decay, dh, dtype=F32)).reshape(1, dh)
    lane = jnp.arange(V7X_MXU_DIM)
    bd = jnp.where((lane[:, None] // hd) == (lane[None, :] // hd), 1.0 / hd, 0.0).astype(BF16)
    qk_scale = jnp.concatenate([jnp.full((n_q * hd,), hd ** -0.5, F32),
                                jnp.ones((n_kv * hd,), F32)]).reshape(1, n_rot)

    w_in_bf = w_in.astype(BF16)
    w_hy_bf = w_hy_br.astype(BF16)
    w_att_bf = w_att_br.astype(BF16)
    w_o_bf = w_o.astype(BF16)
    ffn_g_bf = ffn_w_gate.astype(BF16)
    ffn_u_bf = ffn_w_up.astype(BF16)
    ffn_d_bf = ffn_w_down.astype(BF16)
    conv_b3 = hy_conv_b.reshape(depth, 1, 3 * dh)
    ln_mix_g3, ln_mix_b3 = ln_mix_g.reshape(depth, 1, D), ln_mix_b.reshape(depth, 1, D)
    ln_ffn_g3, ln_ffn_b3 = ln_ffn_g.reshape(depth, 1, D), ln_ffn_b.reshape(depth, 1, D)

    hf, hb = _resid_ln(x.reshape(T, D), [], ln_in_g, ln_in_b, alpha)
    for i in range(depth):
        x0, vv = _hy_proj(hb.reshape(B, L, D), w_in_bf, hy_conv_w, conv_b3, i, dh)
        fw1p = jnp.pad(hy_fw1[i], ((0, V7X_LANES - hy_fw1.shape[1]), (0, 0)))
        hs, hdf = _filters(z, fw1p, hy_fb1[i].reshape(1, -1), hy_fw2[i], hy_fb2[i].reshape(1, -1),
                           hy_freq[i], hy_fw3[i], deltas, dh)
        kre, kim = _spectrum(cs_bf, ss_bf, cos_phi, sin_phi, hs, hdf)
        y_hy = _longconv(cs_bf, ss_bf, vv, x0, kre, kim, hy_bias[i].reshape(1, dh))
        gain = jnp.concatenate([jnp.tile(q_norm_g[i], n_q), jnp.tile(k_norm_g[i], n_kv)]).reshape(1, n_rot)
        w_qkv = lax.slice_in_dim(w_in_bf[i], qkv0, gate0, axis=1)
        q, k, v = _qkv_proj(hb, w_qkv, bd, gain, qk_scale, cos_t, sin_t, B, L, n_q, n_kv, hd)
        y_att = _attention(q, k, v)
        merged = _merge(hb, y_hy.reshape(T, dh), y_att.reshape(T, d_attn), w_in_bf, w_hy_bf, w_att_bf,
                        i, gate0)
        hf, hb = _oproj_ln(merged, w_o_bf, hf, ln_mix_g3, ln_mix_b3, i, alpha)
        j = i // 2
        if i % 2 == 0:
            hf, hb = _ffn_dense(hb, ffn_g_bf, ffn_u_bf, ffn_d_bf, hf, ln_ffn_g3, ln_ffn_b3, j, i, alpha)
        else:
            ya, yb, wts = _moe(hf, hb, router_w[j], router_b[j], exp_w_gate, exp_w_up, exp_w_down, j)
            hf, hb = _moe_combine_ln(hf, ya, yb, wts, ln_ffn_g[i], ln_ffn_b[i], alpha)
    return hf.reshape(B, L, D)
```

```python
import functools
import math

import jax
import jax.numpy as jnp
from jax import lax
from jax.experimental import pallas as pl
from jax.experimental.pallas import tpu as pltpu

F32 = jnp.float32
BF16 = jnp.bfloat16

GRID_W = 64
ROPE_THETA = 10000.0
TOP_K = 2
DECAY_TARGET = 1e-2
FAST_DECAY_PCT = 0.3
SLOW_DECAY_PCT = 1.5
LN_EPS = 1e-5
RMS_EPS = 1e-6

V7X_VMEM_BYTES = 64 * 1024 * 1024
V7X_LANES = 128
V7X_MXU_DIM = 256
VMEM_BUDGET = V7X_VMEM_BYTES - 8 * 1024 * 1024
ATTN_SUBTILE = 512
EPILOGUE_SUBTILE = V7X_MXU_DIM
MOE_CALLS = 4
MOE_ROW_STEP = V7X_MXU_DIM


def _params(semantics, vmem_bytes):
    return pltpu.CompilerParams(dimension_semantics=semantics,
                                vmem_limit_bytes=min(int(vmem_bytes), VMEM_BUDGET))


def _dot(a, b):
    return jnp.dot(a, b, preferred_element_type=F32)


def _split_bf16(a):
    hi = a.astype(BF16)
    lo = (a - hi.astype(F32)).astype(BF16)
    return hi, lo


def _dot3(a_hi, a_lo, b_hi, b_lo):
    return _dot(a_hi, b_hi) + (_dot(a_hi, b_lo) + _dot(a_lo, b_hi))


def _layer_norm_rows(x, g, b):
    mu = jnp.mean(x, axis=-1, keepdims=True)
    xc = x - mu
    var = jnp.mean(xc * xc, axis=-1, keepdims=True)
    return xc * lax.rsqrt(var + LN_EPS) * g + b


def _resid_ln_kernel(alpha, n_add, *refs):
    h_ref = refs[0]
    add_refs = refs[1:1 + n_add]
    g_ref, b_ref, of_ref, ob_ref = refs[1 + n_add:]
    x = h_ref[...]
    if n_add:
        acc = add_refs[0][...]
        for r in add_refs[1:]:
            acc = acc + r[...]
        x = alpha * x + acc
    y = _layer_norm_rows(x, g_ref[...], b_ref[...])
    of_ref[...] = y
    ob_ref[...] = y.astype(BF16)


def _resid_ln(h, addends, g, b, alpha, tm=256):
    T, D = h.shape
    n_add = len(addends)
    row = pl.BlockSpec((tm, D), lambda i: (i, 0))
    vec = pl.BlockSpec((1, D), lambda i: (0, 0))
    return pl.pallas_call(
        functools.partial(_resid_ln_kernel, alpha, n_add),
        grid=(T // tm,),
        in_specs=[row] * (1 + n_add) + [vec, vec],
        out_specs=[row, row],
        out_shape=[jax.ShapeDtypeStruct((T, D), F32), jax.ShapeDtypeStruct((T, D), BF16)],
        compiler_params=_params(("parallel",), (2 * (1 + n_add) * 4 + 12 + 16) * tm * D),
        name="resid_ln",
    )(h, *addends, g.reshape(1, D), b.reshape(1, D))


def _hy_proj_kernel(h_ref, w0_ref, w1_ref, w2_ref, cw0_ref, cw1_ref, cw2_ref,
                    cb0_ref, cb1_ref, cb2_ref, x0_ref, vv_ref):
    h = h_ref[...]
    L = h.shape[0]
    row = lax.broadcasted_iota(jnp.int32, (L, 1), 0)

    def stream(w_ref, cw_ref, cb_ref):
        p = _dot(h, w_ref[...])
        prev = jnp.where(row == 0, 0.0, pltpu.roll(p, 1, 0))
        nxt = jnp.where(row == L - 1, 0.0, pltpu.roll(p, L - 1, 0))
        cw = cw_ref[...]
        return prev * cw[0:1] + p * cw[1:2] + nxt * cw[2:3] + cb_ref[...]

    x0_ref[...] = stream(w0_ref, cw0_ref, cb0_ref)
    vv_ref[...] = stream(w1_ref, cw1_ref, cb1_ref) * stream(w2_ref, cw2_ref, cb2_ref)


def _hy_proj(hb3, w_in_bf, conv_w, conv_b3, layer, dh, tn=256):
    B, L, D = hb3.shape
    nj = dh // tn

    def wspec(s):
        return pl.BlockSpec((None, D, tn), lambda j, b: (layer, 0, s * nj + j))

    def cwspec(s):
        return pl.BlockSpec((None, conv_w.shape[1], tn), lambda j, b: (layer, 0, s * nj + j))

    def cbspec(s):
        return pl.BlockSpec((None, 1, tn), lambda j, b: (layer, 0, s * nj + j))

    out = pl.BlockSpec((None, L, tn), lambda j, b: (b, 0, j))
    return pl.pallas_call(
        _hy_proj_kernel,
        grid=(nj, B),
        in_specs=[pl.BlockSpec((None, L, D), lambda j, b: (b, 0, 0)),
                  wspec(0), wspec(1), wspec(2), cwspec(0), cwspec(1), cwspec(2),
                  cbspec(0), cbspec(1), cbspec(2)],
        out_specs=[out, out],
        out_shape=[jax.ShapeDtypeStruct((B, L, dh), F32)] * 2,
        compiler_params=_params(("parallel", "parallel"),
                                4 * L * D + 12 * D * tn + 16 * L * tn + 40 * L * tn),
        name="hyena_proj_conv",
    )(hb3, w_in_bf, w_in_bf, w_in_bf, conv_w, conv_w, conv_w, conv_b3, conv_b3, conv_b3)


def _qkv_kernel(n_q, n_kv, hd, x_ref, w_ref, bd_ref, gain_ref, scale_ref, cos_ref, sin_ref,
                q_ref, k_ref, v_ref):
    n_rot = (n_q + n_kv) * hd
    tm = x_ref.shape[0]
    ts = min(tm, EPILOGUE_SUBTILE)
    w = w_ref[...]
    bd = bd_ref[...]
    gw = bd.shape[0]
    reps = n_rot // V7X_LANES
    lane = lax.broadcasted_iota(jnp.int32, (1, n_rot), 1)
    first_half = (lane % hd) < (hd // 2)
    half = hd // 2
    pad_lane = lax.broadcasted_iota(jnp.int32, (ts, V7X_LANES - hd), 1)
    ones_col = jnp.where(pad_lane == 0, 1.0, 0.0).astype(BF16)
    for r0 in range(0, tm, ts):
        rows = slice(r0, r0 + ts)
        p = _dot(x_ref[rows, :], w)
        pr = p[:, :n_rot]
        sq = (pr * pr).astype(BF16)
        ms = jnp.concatenate([_dot(sq[:, c:c + gw], bd) for c in range(0, n_rot, gw)], axis=1)
        xn = pr * lax.rsqrt(ms + RMS_EPS) * gain_ref[...]
        cos = jnp.concatenate([cos_ref[rows, :]] * reps, axis=1)
        sin = jnp.concatenate([sin_ref[rows, :]] * reps, axis=1)
        swapped = jnp.where(first_half, pltpu.roll(xn, n_rot - half, 1), pltpu.roll(xn, half, 1))
        rot = ((xn * cos + swapped * sin) * scale_ref[...]).astype(BF16)
        for hh in range(n_q):
            q_ref[hh, rows, :] = rot[:, hh * hd:(hh + 1) * hd]
        for hh in range(n_kv):
            k_ref[hh, rows, :] = rot[:, (n_q + hh) * hd:(n_q + hh + 1) * hd]
            v_h = p[:, n_rot + hh * hd:n_rot + (hh + 1) * hd].astype(BF16)
            v_ref[hh, rows, :] = jnp.concatenate([v_h, ones_col], axis=1)


def _qkv_proj(hb, w_qkv, bd, gain, scale, cos_t, sin_t, B, L, n_q, n_kv, hd, tm=512):
    T, D = hb.shape
    nw = w_qkv.shape[1]
    n_rot = (n_q + n_kv) * hd
    tm = min(tm, L)
    tpb = L // tm
    const = lambda i: (0, 0)
    return pl.pallas_call(
        functools.partial(_qkv_kernel, n_q, n_kv, hd),
        grid=(T // tm,),
        in_specs=[pl.BlockSpec((tm, D), lambda i: (i, 0)),
                  pl.BlockSpec((D, nw), const),
                  pl.BlockSpec(bd.shape, const),
                  pl.BlockSpec((1, n_rot), const),
                  pl.BlockSpec((1, n_rot), const),
                  pl.BlockSpec((tm, V7X_LANES), lambda i: (i % tpb, 0)),
                  pl.BlockSpec((tm, V7X_LANES), lambda i: (i % tpb, 0))],
        out_specs=[pl.BlockSpec((None, n_q, tm, hd), lambda i: (i // tpb, 0, i % tpb, 0)),
                   pl.BlockSpec((None, n_kv, tm, hd), lambda i: (i // tpb, 0, i % tpb, 0)),
                   pl.BlockSpec((None, n_kv, tm, V7X_LANES), lambda i: (i // tpb, 0, i % tpb, 0))],
        out_shape=[jax.ShapeDtypeStruct((B, n_q, L, hd), BF16),
                   jax.ShapeDtypeStruct((B, n_kv, L, hd), BF16),
                   jax.ShapeDtypeStruct((B, n_kv, L, V7X_LANES), BF16)],
        compiler_params=_params(("parallel",),
                                4 * tm * D + 4 * D * nw + 8 * tm * (n_q + 2 * n_kv) * V7X_LANES
                                + 48 * tm * nw),
        name="qkv_proj_norm_rope",
    )(hb, w_qkv, bd, gain, scale, cos_t, sin_t)


def _attn_kernel(group, hd, q_ref, k_ref, v_ref, o_ref):
    k = k_ref[...]
    v = v_ref[...]
    tq = q_ref.shape[1]
    ts = min(tq, ATTN_SUBTILE)
    for r0 in range(0, tq, ts):
        for g in range(group):
            s = lax.dot_general(q_ref[g, r0:r0 + ts, :], k, (((1,), (1,)), ((), ())),
                                preferred_element_type=F32)
            m = jnp.max(s, axis=-1, keepdims=True)
            e = jnp.exp(s - m).astype(BF16)
            o = _dot(e, v)
            o_ref[r0:r0 + ts, g * hd:(g + 1) * hd] = (o[:, :hd] / o[:, hd:hd + 1]).astype(o_ref.dtype)


def _attention(q, k, v, tq=2048):
    B, n_q, L, hd = q.shape
    n_kv = k.shape[1]
    group = n_q // n_kv
    tq = min(tq, L)
    return pl.pallas_call(
        functools.partial(_attn_kernel, group, hd),
        grid=(B, n_kv, L // tq),
        in_specs=[pl.BlockSpec((None, group, tq, hd), lambda b, j, t: (b, j, t, 0)),
                  pl.BlockSpec((None, None, L, hd), lambda b, j, t: (b, j, 0, 0)),
                  pl.BlockSpec((None, None, L, v.shape[-1]), lambda b, j, t: (b, j, 0, 0))],
        out_specs=pl.BlockSpec((None, tq, group * hd), lambda b, j, t: (b, t, j)),
        out_shape=jax.ShapeDtypeStruct((B, L, n_q * hd), BF16),
        compiler_params=_params(("parallel", "parallel", "parallel"),
                                8 * L * V7X_LANES + 8 * group * tq * V7X_LANES + 40 * tq * L),
        name="gqa_attention",
    )(q, k, v)


def _filter_kernel(z_ref, fw1_ref, fb1_ref, fw2_ref, fb2_ref, fr_ref, w3f_ref, w3b_ref, dl_ref,
                   hs_ref, hd_ref):
    z = z_ref[...]
    L = z.shape[0]
    fr = fr_ref[...]

    def hp_dot(a, b):
        a_hi, a_lo = _split_bf16(a)
        b_hi, b_lo = _split_bf16(b)
        return _dot3(a_hi, a_lo, b_hi, b_lo)

    a = jnp.sin(fr[0:1] * (hp_dot(z, fw1_ref[...]) + fb1_ref[...]))
    a = jnp.sin(fr[1:2] * (hp_dot(a, fw2_ref[...]) + fb2_ref[...]))
    window = jnp.exp(-z[:, 0:1] * dl_ref[...])
    h_fwd = hp_dot(a, w3f_ref[...]) * window
    h_bwd = hp_dot(a, w3b_ref[...]) * window
    row = lax.broadcasted_iota(jnp.int32, (L, 1), 0)
    h_bwd0 = jnp.where(row == 0, 0.0, h_bwd)
    hs_ref[...] = h_fwd + h_bwd0
    hd_ref[...] = h_bwd0 - h_fwd


def _filters(z, fw1p, fb1, fw2, fb2, fr, fw3, deltas, dh, tn=256):
    L, zw = z.shape
    nh = fw2.shape[0]
    nj = dh // tn
    const = lambda j: (0, 0)
    out = pl.BlockSpec((L, tn), lambda j: (0, j))
    return pl.pallas_call(
        _filter_kernel,
        grid=(nj,),
        in_specs=[pl.BlockSpec((L, zw), const), pl.BlockSpec((zw, nh), const),
                  pl.BlockSpec((1, nh), const), pl.BlockSpec((nh, nh), const),
                  pl.BlockSpec((1, nh), const), pl.BlockSpec((2, nh), const),
                  pl.BlockSpec((nh, tn), lambda j: (0, j)),
                  pl.BlockSpec((nh, tn), lambda j: (0, nj + j)),
                  pl.BlockSpec((1, tn), lambda j: (0, j))],
        out_specs=[out, out],
        out_shape=[jax.ShapeDtypeStruct((L, dh), F32)] * 2,
        compiler_params=_params(("parallel",), 32 * 1024 * 1024),
        name="hyena_filters",
    )(z, fw1p, fb1, fw2, fb2, fr, fw3, fw3, deltas)


def _spectrum_kernel(cs_ref, ss_ref, cp_ref, sp_ref, hs_ref, hd_ref, kre_ref, kim_ref):
    cs = cs_ref[...]
    ss = ss_ref[...]
    hs = hs_ref[...].astype(BF16)
    hd = hd_ref[...].astype(BF16)
    cp = cp_ref[...]
    sp = sp_ref[...]
    kre_ref[...] = cp * _dot(cs, hs) + sp * _dot(ss, hs)
    kim_ref[...] = cp * _dot(ss, hd) - sp * _dot(cs, hd)


def _spectrum(cs, ss, cos_phi, sin_phi, hs, hd, tk=512, tn=256):
    L, dh = hs.shape
    tk = min(tk, L)
    mat = pl.BlockSpec((tk, L), lambda j, i: (i, 0))
    vec = pl.BlockSpec((tk, 1), lambda j, i: (i, 0))
    col = pl.BlockSpec((L, tn), lambda j, i: (0, j))
    out = pl.BlockSpec((tk, tn), lambda j, i: (i, j))
    return pl.pallas_call(
        _spectrum_kernel,
        grid=(dh // tn, L // tk),
        in_specs=[mat, mat, vec, vec, col, col],
        out_specs=[out, out],
        out_shape=[jax.ShapeDtypeStruct((L, dh), F32)] * 2,
        compiler_params=_params(("parallel", "parallel"), 40 * 1024 * 1024),
        name="filter_spectrum",
    )(cs, ss, cos_phi, sin_phi, hs, hd)


def _longconv_kernel(inv_scale, cs_ref, ss_ref, v_ref, x0_ref, kre_ref, kim_ref, bias_ref, y_ref):
    cs = cs_ref[...]
    ss = ss_ref[...]
    v = v_ref[...]
    vb = v.astype(BF16)
    xc = _dot(cs, vb)
    xs = _dot(ss, vb)
    kre = kre_ref[...]
    kim = kim_ref[...]
    yre = (xc * kre + xs * kim).astype(BF16)
    yim = (xc * kim - xs * kre).astype(BF16)
    conv = (_dot(cs, yre) - _dot(ss, yim)) * inv_scale
    y_ref[...] = (x0_ref[...] * (conv + bias_ref[...] * v)).astype(y_ref.dtype)


def _longconv(cs, ss, vv, x0, kre, kim, bias, tc=256):
    B, L, dh = vv.shape
    mat = pl.BlockSpec((L, L), lambda j, b: (0, 0), pipeline_mode=pl.Buffered(1))
    act = pl.BlockSpec((None, L, tc), lambda j, b: (b, 0, j))
    spec = pl.BlockSpec((L, tc), lambda j, b: (0, j))
    return pl.pallas_call(
        functools.partial(_longconv_kernel, 1.0 / L),
        grid=(dh // tc, B),
        in_specs=[mat, mat, act, act, spec, spec, pl.BlockSpec((1, tc), lambda j, b: (0, j))],
        out_specs=act,
        out_shape=jax.ShapeDtypeStruct((B, L, dh), BF16),
        compiler_params=_params(("parallel", "parallel"),
                                4 * L * L + 16 * L * tc + 16 * L * tc + 4 * L * tc + 40 * L * tc),
        name="hyena_longconv",
    )(cs, ss, vv, x0, kre, kim, bias)


def _merge_kernel(h_ref, yh_ref, ya_ref, wgh_ref, wga_ref, wh_ref, wa_ref, o_ref):
    h = h_ref[...]
    gh = jax.nn.sigmoid(_dot(h, wgh_ref[...]))
    ga = jax.nn.sigmoid(_dot(h, wga_ref[...]))
    o_ref[...] = (gh * _dot(yh_ref[...], wh_ref[...])
                  + ga * _dot(ya_ref[...], wa_ref[...])).astype(o_ref.dtype)


def _merge(hb, y_hy, y_att, w_in_bf, w_hy, w_att, layer, gate_col0, tm=1024, tn=512):
    T, D = hb.shape
    dh, da = y_hy.shape[1], y_att.shape[1]
    tm = min(tm, T)
    g0 = gate_col0 // tn
    nj = D // tn
    return pl.pallas_call(
        _merge_kernel,
        grid=(T // tm, nj),
        in_specs=[pl.BlockSpec((tm, D), lambda i, j: (i, 0)),
                  pl.BlockSpec((tm, dh), lambda i, j: (i, 0)),
                  pl.BlockSpec((tm, da), lambda i, j: (i, 0)),
                  pl.BlockSpec((None, D, tn), lambda i, j: (layer, 0, g0 + j)),
                  pl.BlockSpec((None, D, tn), lambda i, j: (layer, 0, g0 + nj + j)),
                  pl.BlockSpec((None, dh, tn), lambda i, j: (layer, 0, j)),
                  pl.BlockSpec((None, da, tn), lambda i, j: (layer, 0, j))],
        out_specs=pl.BlockSpec((tm, tn), lambda i, j: (i, j)),
        out_shape=jax.ShapeDtypeStruct((T, D), BF16),
        compiler_params=_params(("parallel", "parallel"),
                                4 * tm * (D + dh + da) + 4 * tn * (2 * D + dh + da) + 4 * tm * tn
                                + 32 * tm * tn),
        name="gated_merge",
    )(hb, y_hy, y_att, w_in_bf, w_in_bf, w_hy, w_att)


def _pack_pairs(yb):
    c = yb.shape[1] // 2
    lo = lax.bitcast_convert_type(yb[:, :c].astype(F32), jnp.uint32) >> 16
    hi = lax.bitcast_convert_type(yb[:, c:].astype(F32), jnp.uint32) & jnp.uint32(0xFFFF0000)
    return lo | hi


def _unpack_pairs(u):
    lo = lax.bitcast_convert_type(u << 16, F32).astype(BF16)
    hi = lax.bitcast_convert_type(u & jnp.uint32(0xFFFF0000), F32).astype(BF16)
    return lo, hi


def _oproj_ln_kernel(alpha, packed, x_ref, w_ref, h_ref, g_ref, b_ref, of_ref, ob_ref):
    tm = x_ref.shape[0]
    ts = min(tm, EPILOGUE_SUBTILE)
    w = w_ref[...]
    for r0 in range(0, tm, ts):
        y = alpha * h_ref[r0:r0 + ts, :] + _dot(x_ref[r0:r0 + ts, :], w)
        y = _layer_norm_rows(y, g_ref[...], b_ref[...])
        of_ref[r0:r0 + ts, :] = y
        yb = y.astype(BF16)
        ob_ref[r0:r0 + ts, :] = _pack_pairs(yb) if packed else yb


def _oproj_ln(xb, w_o, h, g, b, layer, alpha, packed, tm=512):
    T, D = h.shape
    row = lambda i: (i, 0)
    vec = pl.BlockSpec((None, 1, D), lambda i: (layer, 0, 0))
    wb, tb = (D // 2, jnp.uint32) if packed else (D, BF16)
    return pl.pallas_call(
        functools.partial(_oproj_ln_kernel, alpha, packed),
        grid=(T // tm,),
        in_specs=[pl.BlockSpec((tm, D), row),
                  pl.BlockSpec((None, D, D), lambda i: (layer, 0, 0)),
                  pl.BlockSpec((tm, D), row), vec, vec],
        out_specs=[pl.BlockSpec((tm, D), row), pl.BlockSpec((tm, wb), row)],
        out_shape=[jax.ShapeDtypeStruct((T, D), F32), jax.ShapeDtypeStruct((T, wb), tb)],
        compiler_params=_params(("parallel",), 4 * D * D + 24 * tm * D + 24 * tm * D),
        name="oproj_resid_ln",
    )(xb, w_o, h, g, b)


def _swiglu_partial(x_ref, wg_ref, wu_ref, wd_ref):
    x = x_ref[...]
    g = _dot(x, wg_ref[...].astype(BF16))
    u = _dot(x, wu_ref[...].astype(BF16))
    a = (g * jax.nn.sigmoid(g) * u).astype(BF16)
    return _dot(a, wd_ref[...].astype(BF16))


def _swiglu_partial_packed(x_ref, wg_ref, wu_ref, wd_ref):
    lo, hi = _unpack_pairs(x_ref[...])
    c = lo.shape[1]
    wg = wg_ref[...].astype(BF16)
    wu = wu_ref[...].astype(BF16)
    g = _dot(lo, wg[:c]) + _dot(hi, wg[c:])
    u = _dot(lo, wu[:c]) + _dot(hi, wu[c:])
    a = (g * jax.nn.sigmoid(g) * u).astype(BF16)
    return _dot(a, wd_ref[...].astype(BF16))


def _ffn_dense_kernel(alpha, x_ref, wg_ref, wu_ref, wd_ref, h_ref, g_ref, b_ref,
                      of_ref, ob_ref, acc_ref):
    f = pl.program_id(1)

    @pl.when(f == 0)
    def _():
        acc_ref[...] = jnp.zeros_like(acc_ref)

    acc_ref[...] += _swiglu_partial(x_ref, wg_ref, wu_ref, wd_ref)

    @pl.when(f == pl.num_programs(1) - 1)
    def _():
        y = _layer_norm_rows(alpha * h_ref[...] + acc_ref[...], g_ref[...], b_ref[...])
        of_ref[...] = y
        ob_ref[...] = y.astype(BF16)


def _ffn_dense(xb, wg, wu, wd, h, g, b, j, layer, alpha, tm=512, tf=512):
    T, D = xb.shape
    F = wg.shape[-1]
    row = lambda i, f: (i, 0)
    vec = pl.BlockSpec((None, 1, D), lambda i, f: (layer, 0, 0))
    return pl.pallas_call(
        functools.partial(_ffn_dense_kernel, alpha),
        grid=(T // tm, F // tf),
        in_specs=[pl.BlockSpec((tm, D), row),
                  pl.BlockSpec((None, D, tf), lambda i, f: (j, 0, f)),
                  pl.BlockSpec((None, D, tf), lambda i, f: (j, 0, f)),
                  pl.BlockSpec((None, tf, D), lambda i, f: (j, f, 0)),
                  pl.BlockSpec((tm, D), row), vec, vec],
        out_specs=[pl.BlockSpec((tm, D), row), pl.BlockSpec((tm, D), row)],
        out_shape=[jax.ShapeDtypeStruct((T, D), F32), jax.ShapeDtypeStruct((T, D), BF16)],
        scratch_shapes=[pltpu.VMEM((tm, D), F32)],
        compiler_params=_params(("parallel", "arbitrary"),
                                4 * tm * D + 12 * D * tf + 8 * tm * D + 12 * tm * D + 4 * tm * D
                                + 16 * tm * tf + 8 * tm * D),
        name="ffn_dense_swiglu_ln",
    )(xb, wg, wu, wd, h, g, b)


def _ffn_group_kernel(blk0, be_ref, rows_ref, x_ref, wg_ref, wu_ref, wd_ref, *rest):
    o_ref = rest[-1]
    f = pl.program_id(1)
    n_rows = rows_ref[blk0 + pl.program_id(0)]
    tm = x_ref.shape[0]

    @pl.when(jnp.logical_and(n_rows > 0, f == 0))
    def _():
        o_ref[...] = jnp.zeros_like(o_ref)

    step = min(tm, MOE_ROW_STEP)
    for hi in range(step, tm + 1, step):
        @pl.when(jnp.logical_and(n_rows > hi - step, n_rows <= hi))
        def _(hi=hi):
            o_ref[0:hi, :] += _swiglu_partial_packed(x_ref.at[0:hi], wg_ref, wu_ref, wd_ref)


def _ffn_grouped(blk_e, blk_rows, xg, wg, wu, wd, j, tm, blk0, nblk, prev=None, tf=512):
    D = wg.shape[-2]
    F = wg.shape[-1]
    nf = F // tf
    nb = xg.shape[0] // tm

    def live(i, rows):
        return rows[blk0 + i] > 0

    def x_rows(i, f, be, nv):
        return (jnp.where(live(i, nv), i, 0), 0)

    def out_rows(i, f, be, nv):
        return (jnp.where(live(i, nv), blk0 + i, nblk), 0)

    def hidden(i, f, nv):
        return jnp.where(live(i, nv), f, nf - 1)

    in_specs = [pl.BlockSpec((tm, xg.shape[1]), x_rows),
                pl.BlockSpec((None, None, D, tf),
                             lambda i, f, be, nv: (j, be[blk0 + i], 0, hidden(i, f, nv))),
                pl.BlockSpec((None, None, D, tf),
                             lambda i, f, be, nv: (j, be[blk0 + i], 0, hidden(i, f, nv))),
                pl.BlockSpec((None, None, tf, D),
                             lambda i, f, be, nv: (j, be[blk0 + i], hidden(i, f, nv), 0))]
    args = [blk_e, blk_rows, xg, wg, wu, wd]
    aliases = {}
    if prev is not None:
        in_specs.append(pl.BlockSpec(memory_space=pl.ANY))
        aliases = {len(args): 0}
        args.append(prev)
    return pl.pallas_call(
        functools.partial(_ffn_group_kernel, blk0),
        grid_spec=pltpu.PrefetchScalarGridSpec(
            num_scalar_prefetch=2,
            grid=(nb, nf),
            in_specs=in_specs,
            out_specs=pl.BlockSpec((tm, D), out_rows)),
        out_shape=jax.ShapeDtypeStruct(((nblk + 1) * tm, D), F32),
        input_output_aliases=aliases,
        compiler_params=_params(("arbitrary", "arbitrary"),
                                4 * tm * D + 24 * D * tf + 8 * tm * D
                                + 6 * D * tf + 16 * tm * tf + 8 * tm * D),
        name="moe_grouped_swiglu",
    )(*args)


def _router_kernel(x_ref, wh_ref, wl_ref, b_ref, tri_ref, e_ref, w_ref, r_ref, cnt_ref):
    @pl.when(pl.program_id(0) == 0)
    def _():
        cnt_ref[...] = jnp.zeros_like(cnt_ref)

    x_hi, x_lo = _split_bf16(x_ref[...])
    nt = (((1,), (1,)), ((), ()))

    def dg(a, b):
        return lax.dot_general(a, b, nt, preferred_element_type=F32)

    wh = wh_ref[...]
    wl = wl_ref[...]
    logits = dg(wh, x_hi) + (dg(wh, x_lo) + dg(wl, x_hi)) + b_ref[...]
    n_e = logits.shape[0]
    eid = lax.broadcasted_iota(jnp.int32, logits.shape, 0)
    m1 = jnp.max(logits, axis=0, keepdims=True)
    i1 = jnp.min(jnp.where(logits == m1, eid, n_e), axis=0, keepdims=True)
    rest = jnp.where(eid == i1, -jnp.inf, logits)
    m2 = jnp.max(rest, axis=0, keepdims=True)
    i2 = jnp.min(jnp.where(rest == m2, eid, n_e), axis=0, keepdims=True)
    t = jnp.exp(m2 - m1)
    w1 = 1.0 / (1.0 + t)
    e_ref[...] = jnp.concatenate([i1, i2], axis=0)
    w_ref[...] = jnp.concatenate([w1, t * w1], axis=0)

    tri = tri_ref[...]
    oh1 = (eid == i1).astype(F32)
    oh2 = (eid == i2).astype(F32)
    tot1 = jnp.sum(oh1, axis=1, keepdims=True)
    tot2 = jnp.sum(oh2, axis=1, keepdims=True)
    base = cnt_ref[...][:, 0:1]
    c1 = base + _dot(oh1.astype(BF16), tri)
    c2 = base + tot1 + _dot(oh2.astype(BF16), tri)
    r1 = jnp.sum(oh1 * c1, axis=0, keepdims=True)
    r2 = jnp.sum(oh2 * c2, axis=0, keepdims=True)
    r_ref[...] = jnp.concatenate([r1, r2], axis=0).astype(jnp.int32)
    cnt_ref[...] = cnt_ref[...] + (tot1 + tot2)


def _router(h, rw_hi, rw_lo, rb, tm=512):
    T, D = h.shape
    E = rw_hi.shape[0]
    const = lambda i: (0, 0)
    out = pl.BlockSpec((TOP_K, tm), lambda i: (0, i))
    idx = jnp.arange(tm)
    tri = (idx[:, None] < idx[None, :]).astype(BF16)
    return pl.pallas_call(
        _router_kernel,
        grid=(T // tm,),
        in_specs=[pl.BlockSpec((tm, D), lambda i: (i, 0)), pl.BlockSpec((E, D), const),
                  pl.BlockSpec((E, D), const), pl.BlockSpec((E, 1), const),
                  pl.BlockSpec((tm, tm), const)],
        out_specs=[out, out, out, pl.BlockSpec((E, V7X_LANES), const)],
        out_shape=[jax.ShapeDtypeStruct((TOP_K, T), jnp.int32),
                   jax.ShapeDtypeStruct((TOP_K, T), F32),
                   jax.ShapeDtypeStruct((TOP_K, T), jnp.int32),
                   jax.ShapeDtypeStruct((E, V7X_LANES), F32)],
        compiler_params=_params(("arbitrary",), 32 * 1024 * 1024),
        name="moe_router_top2",
    )(h, rw_hi, rw_lo, rb, tri)


def _moe_combine_ln_kernel(alpha, h_ref, ya_ref, yb_ref, w_ref, g_ref, b_ref, of_ref, ob_ref):
    w = w_ref[...]
    f = ya_ref[...] * w[:, 0:1] + yb_ref[...] * w[:, 1:2]
    y = _layer_norm_rows(alpha * h_ref[...] + f, g_ref[...], b_ref[...])
    of_ref[...] = y
    ob_ref[...] = y.astype(BF16)


def _moe_combine_ln(h, ya, yb, w, g, b, alpha, tm=256):
    T, D = h.shape
    row = pl.BlockSpec((tm, D), lambda i: (i, 0))
    vec = pl.BlockSpec((1, D), lambda i: (0, 0))
    return pl.pallas_call(
        functools.partial(_moe_combine_ln_kernel, alpha),
        grid=(T // tm,),
        in_specs=[row, row, row, pl.BlockSpec((tm, w.shape[1]), lambda i: (i, 0)), vec, vec],
        out_specs=[row, row],
        out_shape=[jax.ShapeDtypeStruct((T, D), F32), jax.ShapeDtypeStruct((T, D), BF16)],
        compiler_params=_params(("parallel",), 64 * tm * D),
        name="moe_combine_ln",
    )(h, ya, yb, w, g.reshape(1, D), b.reshape(1, D))


def _dft_tables(L):
    n2 = 2 * L
    k = jnp.arange(L, dtype=jnp.int32)[:, None]
    n = jnp.arange(L, dtype=jnp.int32)[None, :]
    ang = (((2 * k + 1) * (2 * n + 1)) % (4 * n2)).astype(F32) * (math.pi / (2 * n2))
    phi = (2 * k + 1).astype(F32) * (math.pi / (2 * n2))
    return jnp.cos(ang).astype(BF16), jnp.sin(ang).astype(BF16), jnp.cos(phi), jnp.sin(phi)


def _rope_tables(L, hd):
    rows = L // GRID_W
    row = jnp.repeat(jnp.arange(rows, dtype=F32), GRID_W)
    col = jnp.tile(jnp.arange(GRID_W, dtype=F32), rows)
    n_pairs = hd // 4
    inv = ROPE_THETA ** (-jnp.arange(n_pairs, dtype=F32) / n_pairs)
    ang = jnp.concatenate([row[:, None] * inv, col[:, None] * inv], axis=-1)
    cos, sin = jnp.cos(ang), jnp.sin(ang)
    reps = V7X_LANES // hd
    return (jnp.tile(jnp.concatenate([cos, cos], axis=-1), (1, reps)),
            jnp.tile(jnp.concatenate([-sin, sin], axis=-1), (1, reps)))


def _filter_features(L, n_emb):
    t = jnp.linspace(0.0, 1.0, L, dtype=F32)[:, None]
    bands = (n_emb - 1) // 2
    w = 2.0 * math.pi * jnp.arange(L, dtype=F32)[:, None] / L
    f = jnp.linspace(1e-4, bands - 1, bands, dtype=F32)[None, :]
    z = jnp.concatenate([t, jnp.cos(f * w), -jnp.sin(f * w)], axis=-1)
    return jnp.pad(z, ((0, 0), (0, V7X_LANES - n_emb)))


def _moe_block_rows(T):
    return min(1024, max(V7X_MXU_DIM, T // 4))


def _moe(hf, hb, router_w, router_b, wg, wu, wd, j):
    T, D = hf.shape
    E = router_w.shape[1]
    rw_hi, rw_lo = _split_bf16(router_w.T)
    top_e, top_w, rank, cnt = _router(hf, rw_hi, rw_lo, router_b.reshape(E, 1))

    tm = _moe_block_rows(T)
    counts = cnt[:, 0].astype(jnp.int32)
    padded = (counts + tm - 1) // tm * tm
    pad_end = jnp.cumsum(padded)
    pad_start = pad_end - padded
    nblk = -(-(T * TOP_K + E * (tm - 1)) // tm)
    P = nblk * tm
    n_valid = (pad_end[-1] // tm).astype(jnp.int32)
    blk_first = jnp.arange(nblk, dtype=jnp.int32) * tm
    blk_e = jnp.sum((blk_first[:, None] >= pad_end[None, :]).astype(jnp.int32), axis=1)
    blk_e = jnp.minimum(blk_e, E - 1)
    row_end = (pad_start + counts)[blk_e]
    blk_rows = jnp.where(jnp.arange(nblk) < n_valid, jnp.clip(row_end - blk_first, 0, tm), 0)
    blk_e = jnp.where(jnp.arange(nblk) < n_valid, blk_e, blk_e[jnp.maximum(n_valid - 1, 0)])
    start_of = jnp.sum(jnp.where(top_e[:, :, None] == jnp.arange(E, dtype=jnp.int32),
                                 pad_start[None, None, :], 0), axis=-1)
    dest = start_of + rank

    tok = jnp.broadcast_to(jnp.arange(T, dtype=jnp.int32), (TOP_K, T))
    src_tok = jnp.zeros((P,), jnp.int32).at[dest.reshape(-1)].set(tok.reshape(-1))
    yg = None
    blk_rows = blk_rows.astype(jnp.int32)
    for blk0 in range(0, nblk, -(-nblk // MOE_CALLS)):
        nb = min(-(-nblk // MOE_CALLS), nblk - blk0)
        xg = hb[lax.slice_in_dim(src_tok, blk0 * tm, (blk0 + nb) * tm)]
        yg = _ffn_grouped(blk_e, blk_rows, xg, wg, wu, wd, j, tm, blk0, nblk, prev=yg)
    return yg[dest[0]], yg[dest[1]], top_w.T


def kernel(x, ln_in_g, ln_in_b, w_in, hy_conv_w, hy_conv_b, hy_fw1, hy_fb1, hy_fw2, hy_fb2, hy_fw3, hy_freq, hy_bias, q_norm_g, k_norm_g, w_hy_br, w_att_br, w_o, ln_mix_g, ln_mix_b, ffn_w_gate, ffn_w_up, ffn_w_down, router_w, router_b, exp_w_gate, exp_w_up, exp_w_down, ln_ffn_g, ln_ffn_b):
    B, L, D = x.shape
    T = B * L
    depth = w_in.shape[0]
    dh = w_hy_br.shape[1]
    d_attn = w_att_br.shape[1]
    hd = q_norm_g.shape[-1]
    n_q = d_attn // hd
    d_kv = (w_in.shape[2] - 3 * dh - d_attn - 2 * D) // 2
    n_kv = d_kv // hd
    alpha = (2 * depth) ** 0.25
    qkv0 = 3 * dh
    gate0 = qkv0 + d_attn + 2 * d_kv
    n_rot = (n_q + n_kv) * hd
    assert L % GRID_W == 0 and V7X_LANES % hd == 0 and n_rot % V7X_MXU_DIM == 0

    cs_bf, ss_bf, cos_phi, sin_phi = _dft_tables(L)
    cos_t, sin_t = _rope_tables(L, hd)
    z = _filter_features(L, hy_fw1.shape[1])
    min_decay = math.log(DECAY_TARGET) / SLOW_DECAY_PCT
    max_decay = math.log(DECAY_TARGET) / FAST_DECAY_PCT
    deltas = jnp.abs(jnp.linspace(min_decay, max_decay, dh, dtype=F32)).reshape(1, dh)
    lane = jnp.arange(V7X_MXU_DIM)
    bd = jnp.where((lane[:, None] // hd) == (lane[None, :] // hd), 1.0 / hd, 0.0).astype(BF16)
    qk_scale = jnp.concatenate([jnp.full((n_q * hd,), hd ** -0.5, F32),
                                jnp.ones((n_kv * hd,), F32)]).reshape(1, n_rot)

    w_in_bf = w_in.astype(BF16)
    w_hy_bf = w_hy_br.astype(BF16)
    w_att_bf = w_att_br.astype(BF16)
    w_o_bf = w_o.astype(BF16)
    ffn_g_bf = ffn_w_gate.astype(BF16)
    ffn_u_bf = ffn_w_up.astype(BF16)
    ffn_d_bf = ffn_w_down.astype(BF16)
    conv_b3 = hy_conv_b.reshape(depth, 1, 3 * dh)
    ln_mix_g3, ln_mix_b3 = ln_mix_g.reshape(depth, 1, D), ln_mix_b.reshape(depth, 1, D)
    ln_ffn_g3, ln_ffn_b3 = ln_ffn_g.reshape(depth, 1, D), ln_ffn_b.reshape(depth, 1, D)

    hf, hb = _resid_ln(x.reshape(T, D), [], ln_in_g, ln_in_b, alpha)
    for i in range(depth):
        x0, vv = _hy_proj(hb.reshape(B, L, D), w_in_bf, hy_conv_w, conv_b3, i, dh)
        fw1p = jnp.pad(hy_fw1[i], ((0, V7X_LANES - hy_fw1.shape[1]), (0, 0)))
        hs, hdf = _filters(z, fw1p, hy_fb1[i].reshape(1, -1), hy_fw2[i], hy_fb2[i].reshape(1, -1),
                           hy_freq[i], hy_fw3[i], deltas, dh)
        kre, kim = _spectrum(cs_bf, ss_bf, cos_phi, sin_phi, hs, hdf)
        y_hy = _longconv(cs_bf, ss_bf, vv, x0, kre, kim, hy_bias[i].reshape(1, dh))
        gain = jnp.concatenate([jnp.tile(q_norm_g[i], n_q), jnp.tile(k_norm_g[i], n_kv)]).reshape(1, n_rot)
        w_qkv = lax.slice_in_dim(w_in_bf[i], qkv0, gate0, axis=1)
        q, k, v = _qkv_proj(hb, w_qkv, bd, gain, qk_scale, cos_t, sin_t, B, L, n_q, n_kv, hd)
        y_att = _attention(q, k, v)
        merged = _merge(hb, y_hy.reshape(T, dh), y_att.reshape(T, d_attn), w_in_bf, w_hy_bf, w_att_bf,
                        i, gate0)
        hf, hb = _oproj_ln(merged, w_o_bf, hf, ln_mix_g3, ln_mix_b3, i, alpha, packed=i % 2 == 1)
        j = i // 2
        if i % 2 == 0:
            hf, hb = _ffn_dense(hb, ffn_g_bf, ffn_u_bf, ffn_d_bf, hf, ln_ffn_g3, ln_ffn_b3, j, i, alpha)
        else:
            ya, yb, wts = _moe(hf, hb, router_w[j], router_b[j], exp_w_gate, exp_w_up, exp_w_down, j)
            hf, hb = _moe_combine_ln(hf, ya, yb, wts, ln_ffn_g[i], ln_ffn_b[i], alpha)
    return hf.reshape(B, L, D)
```

```python
import functools
import math

import jax
import jax.numpy as jnp
from jax import lax
from jax.experimental import pallas as pl
from jax.experimental.pallas import tpu as pltpu

F32 = jnp.float32
BF16 = jnp.bfloat16

GRID_W = 64
ROPE_THETA = 10000.0
TOP_K = 2
DECAY_TARGET = 1e-2
FAST_DECAY_PCT = 0.3
SLOW_DECAY_PCT = 1.5
LN_EPS = 1e-5
RMS_EPS = 1e-6

V7X_VMEM_BYTES = 64 * 1024 * 1024
V7X_LANES = 128
V7X_MXU_DIM = 256
VMEM_BUDGET = V7X_VMEM_BYTES - 8 * 1024 * 1024
ATTN_SUBTILE = 512
EPILOGUE_SUBTILE = V7X_MXU_DIM
MOE_CALLS = 4
MOE_ROW_STEP = V7X_MXU_DIM


def _params(semantics, vmem_bytes):
    return pltpu.CompilerParams(dimension_semantics=semantics,
                                vmem_limit_bytes=min(int(vmem_bytes), VMEM_BUDGET))


def _dot(a, b):
    return jnp.dot(a, b, preferred_element_type=F32)


def _split_bf16(a):
    hi = a.astype(BF16)
    lo = (a - hi.astype(F32)).astype(BF16)
    return hi, lo


def _dot3(a_hi, a_lo, b_hi, b_lo):
    return _dot(a_hi, b_hi) + (_dot(a_hi, b_lo) + _dot(a_lo, b_hi))


def _layer_norm_rows(x, g, b):
    mu = jnp.mean(x, axis=-1, keepdims=True)
    xc = x - mu
    var = jnp.mean(xc * xc, axis=-1, keepdims=True)
    return xc * lax.rsqrt(var + LN_EPS) * g + b


def _resid_ln_kernel(alpha, n_add, *refs):
    h_ref = refs[0]
    add_refs = refs[1:1 + n_add]
    g_ref, b_ref, of_ref, ob_ref = refs[1 + n_add:]
    x = h_ref[...]
    if n_add:
        acc = add_refs[0][...]
        for r in add_refs[1:]:
            acc = acc + r[...]
        x = alpha * x + acc
    y = _layer_norm_rows(x, g_ref[...], b_ref[...])
    of_ref[...] = y
    ob_ref[...] = y.astype(BF16)


def _resid_ln(h, addends, g, b, alpha, tm=256):
    T, D = h.shape
    n_add = len(addends)
    row = pl.BlockSpec((tm, D), lambda i: (i, 0))
    vec = pl.BlockSpec((1, D), lambda i: (0, 0))
    return pl.pallas_call(
        functools.partial(_resid_ln_kernel, alpha, n_add),
        grid=(T // tm,),
        in_specs=[row] * (1 + n_add) + [vec, vec],
        out_specs=[row, row],
        out_shape=[jax.ShapeDtypeStruct((T, D), F32), jax.ShapeDtypeStruct((T, D), BF16)],
        compiler_params=_params(("parallel",), (2 * (1 + n_add) * 4 + 12 + 16) * tm * D),
        name="resid_ln",
    )(h, *addends, g.reshape(1, D), b.reshape(1, D))


def _hy_proj_kernel(h_ref, w0_ref, w1_ref, w2_ref, cw0_ref, cw1_ref, cw2_ref,
                    cb0_ref, cb1_ref, cb2_ref, x0_ref, vv_ref):
    h = h_ref[...]
    L = h.shape[0]
    row = lax.broadcasted_iota(jnp.int32, (L, 1), 0)

    def stream(w_ref, cw_ref, cb_ref):
        p = _dot(h, w_ref[...])
        prev = jnp.where(row == 0, 0.0, pltpu.roll(p, 1, 0))
        nxt = jnp.where(row == L - 1, 0.0, pltpu.roll(p, L - 1, 0))
        cw = cw_ref[...]
        return prev * cw[0:1] + p * cw[1:2] + nxt * cw[2:3] + cb_ref[...]

    x0_ref[...] = stream(w0_ref, cw0_ref, cb0_ref)
    vv_ref[...] = stream(w1_ref, cw1_ref, cb1_ref) * stream(w2_ref, cw2_ref, cb2_ref)


def _hy_proj(hb3, w_in_bf, conv_w, conv_b3, layer, dh, tn=256):
    B, L, D = hb3.shape
    nj = dh // tn

    def wspec(s):
        return pl.BlockSpec((None, D, tn), lambda j, b: (layer, 0, s * nj + j))

    def cwspec(s):
        return pl.BlockSpec((None, conv_w.shape[1], tn), lambda j, b: (layer, 0, s * nj + j))

    def cbspec(s):
        return pl.BlockSpec((None, 1, tn), lambda j, b: (layer, 0, s * nj + j))

    out = pl.BlockSpec((None, L, tn), lambda j, b: (b, 0, j))
    return pl.pallas_call(
        _hy_proj_kernel,
        grid=(nj, B),
        in_specs=[pl.BlockSpec((None, L, D), lambda j, b: (b, 0, 0)),
                  wspec(0), wspec(1), wspec(2), cwspec(0), cwspec(1), cwspec(2),
                  cbspec(0), cbspec(1), cbspec(2)],
        out_specs=[out, out],
        out_shape=[jax.ShapeDtypeStruct((B, L, dh), F32)] * 2,
        compiler_params=_params(("parallel", "parallel"),
                                4 * L * D + 12 * D * tn + 16 * L * tn + 40 * L * tn),
        name="hyena_proj_conv",
    )(hb3, w_in_bf, w_in_bf, w_in_bf, conv_w, conv_w, conv_w, conv_b3, conv_b3, conv_b3)


def _qkv_kernel(n_q, n_kv, hd, x_ref, w_ref, bd_ref, gain_ref, scale_ref, cos_ref, sin_ref,
                q_ref, k_ref, v_ref):
    n_rot = (n_q + n_kv) * hd
    tm = x_ref.shape[0]
    ts = min(tm, EPILOGUE_SUBTILE)
    w = w_ref[...]
    bd = bd_ref[...]
    gw = bd.shape[0]
    reps = n_rot // V7X_LANES
    lane = lax.broadcasted_iota(jnp.int32, (1, n_rot), 1)
    first_half = (lane % hd) < (hd // 2)
    half = hd // 2
    pad_lane = lax.broadcasted_iota(jnp.int32, (ts, V7X_LANES - hd), 1)
    ones_col = jnp.where(pad_lane == 0, 1.0, 0.0).astype(BF16)
    for r0 in range(0, tm, ts):
        rows = slice(r0, r0 + ts)
        p = _dot(x_ref[rows, :], w)
        pr = p[:, :n_rot]
        sq = (pr * pr).astype(BF16)
        ms = jnp.concatenate([_dot(sq[:, c:c + gw], bd) for c in range(0, n_rot, gw)], axis=1)
        xn = pr * lax.rsqrt(ms + RMS_EPS) * gain_ref[...]
        cos = jnp.concatenate([cos_ref[rows, :]] * reps, axis=1)
        sin = jnp.concatenate([sin_ref[rows, :]] * reps, axis=1)
        swapped = jnp.where(first_half, pltpu.roll(xn, n_rot - half, 1), pltpu.roll(xn, half, 1))
        rot = ((xn * cos + swapped * sin) * scale_ref[...]).astype(BF16)
        for hh in range(n_q):
            q_ref[hh, rows, :] = rot[:, hh * hd:(hh + 1) * hd]
        for hh in range(n_kv):
            k_ref[hh, rows, :] = rot[:, (n_q + hh) * hd:(n_q + hh + 1) * hd]
            v_h = p[:, n_rot + hh * hd:n_rot + (hh + 1) * hd].astype(BF16)
            v_ref[hh, rows, :] = jnp.concatenate([v_h, ones_col], axis=1)


def _qkv_proj(hb, w_qkv, bd, gain, scale, cos_t, sin_t, B, L, n_q, n_kv, hd, tm=1024):
    T, D = hb.shape
    nw = w_qkv.shape[1]
    n_rot = (n_q + n_kv) * hd
    tm = min(tm, L)
    tpb = L // tm
    const = lambda i: (0, 0)
    return pl.pallas_call(
        functools.partial(_qkv_kernel, n_q, n_kv, hd),
        grid=(T // tm,),
        in_specs=[pl.BlockSpec((tm, D), lambda i: (i, 0)),
                  pl.BlockSpec((D, nw), const),
                  pl.BlockSpec(bd.shape, const),
                  pl.BlockSpec((1, n_rot), const),
                  pl.BlockSpec((1, n_rot), const),
                  pl.BlockSpec((tm, V7X_LANES), lambda i: (i % tpb, 0)),
                  pl.BlockSpec((tm, V7X_LANES), lambda i: (i % tpb, 0))],
        out_specs=[pl.BlockSpec((None, n_q, tm, hd), lambda i: (i // tpb, 0, i % tpb, 0)),
                   pl.BlockSpec((None, n_kv, tm, hd), lambda i: (i // tpb, 0, i % tpb, 0)),
                   pl.BlockSpec((None, n_kv, tm, V7X_LANES), lambda i: (i // tpb, 0, i % tpb, 0))],
        out_shape=[jax.ShapeDtypeStruct((B, n_q, L, hd), BF16),
                   jax.ShapeDtypeStruct((B, n_kv, L, hd), BF16),
                   jax.ShapeDtypeStruct((B, n_kv, L, V7X_LANES), BF16)],
        compiler_params=_params(("parallel",),
                                4 * tm * D + 4 * D * nw + 8 * tm * (n_q + 2 * n_kv) * V7X_LANES
                                + 48 * tm * nw),
        name="qkv_proj_norm_rope",
    )(hb, w_qkv, bd, gain, scale, cos_t, sin_t)


def _attn_kernel(group, hd, q_ref, k_ref, v_ref, o_ref):
    k = k_ref[...]
    v = v_ref[...]
    tq = q_ref.shape[1]
    ts = min(tq, ATTN_SUBTILE)
    for r0 in range(0, tq, ts):
        for g in range(group):
            s = lax.dot_general(q_ref[g, r0:r0 + ts, :], k, (((1,), (1,)), ((), ())),
                                preferred_element_type=F32)
            m = jnp.max(s, axis=-1, keepdims=True)
            e = jnp.exp(s - m).astype(BF16)
            o = _dot(e, v)
            o_ref[r0:r0 + ts, g * hd:(g + 1) * hd] = (o[:, :hd] / o[:, hd:hd + 1]).astype(o_ref.dtype)


def _attention(q, k, v, tq=2048):
    B, n_q, L, hd = q.shape
    n_kv = k.shape[1]
    group = n_q // n_kv
    tq = min(tq, L)
    return pl.pallas_call(
        functools.partial(_attn_kernel, group, hd),
        grid=(B, n_kv, L // tq),
        in_specs=[pl.BlockSpec((None, group, tq, hd), lambda b, j, t: (b, j, t, 0)),
                  pl.BlockSpec((None, None, L, hd), lambda b, j, t: (b, j, 0, 0)),
                  pl.BlockSpec((None, None, L, v.shape[-1]), lambda b, j, t: (b, j, 0, 0))],
        out_specs=pl.BlockSpec((None, tq, group * hd), lambda b, j, t: (b, t, j)),
        out_shape=jax.ShapeDtypeStruct((B, L, n_q * hd), BF16),
        compiler_params=_params(("parallel", "parallel", "parallel"),
                                8 * L * V7X_LANES + 8 * group * tq * V7X_LANES + 40 * tq * L),
        name="gqa_attention",
    )(q, k, v)


def _filter_kernel(z_ref, fw1_ref, fb1_ref, fw2_ref, fb2_ref, fr_ref, w3f_ref, w3b_ref, dl_ref,
                   hs_ref, hd_ref):
    z = z_ref[...]
    L = z.shape[0]
    fr = fr_ref[...]

    def hp_dot(a, b):
        a_hi, a_lo = _split_bf16(a)
        b_hi, b_lo = _split_bf16(b)
        return _dot3(a_hi, a_lo, b_hi, b_lo)

    a = jnp.sin(fr[0:1] * (hp_dot(z, fw1_ref[...]) + fb1_ref[...]))
    a = jnp.sin(fr[1:2] * (hp_dot(a, fw2_ref[...]) + fb2_ref[...]))
    window = jnp.exp(-z[:, 0:1] * dl_ref[...])
    h_fwd = hp_dot(a, w3f_ref[...]) * window
    h_bwd = hp_dot(a, w3b_ref[...]) * window
    row = lax.broadcasted_iota(jnp.int32, (L, 1), 0)
    h_bwd0 = jnp.where(row == 0, 0.0, h_bwd)
    hs_ref[...] = h_fwd + h_bwd0
    hd_ref[...] = h_bwd0 - h_fwd


def _filters(z, fw1p, fb1, fw2, fb2, fr, fw3, deltas, dh, tn=256):
    L, zw = z.shape
    nh = fw2.shape[0]
    nj = dh // tn
    const = lambda j: (0, 0)
    out = pl.BlockSpec((L, tn), lambda j: (0, j))
    return pl.pallas_call(
        _filter_kernel,
        grid=(nj,),
        in_specs=[pl.BlockSpec((L, zw), const), pl.BlockSpec((zw, nh), const),
                  pl.BlockSpec((1, nh), const), pl.BlockSpec((nh, nh), const),
                  pl.BlockSpec((1, nh), const), pl.BlockSpec((2, nh), const),
                  pl.BlockSpec((nh, tn), lambda j: (0, j)),
                  pl.BlockSpec((nh, tn), lambda j: (0, nj + j)),
                  pl.BlockSpec((1, tn), lambda j: (0, j))],
        out_specs=[out, out],
        out_shape=[jax.ShapeDtypeStruct((L, dh), F32)] * 2,
        compiler_params=_params(("parallel",), 32 * 1024 * 1024),
        name="hyena_filters",
    )(z, fw1p, fb1, fw2, fb2, fr, fw3, fw3, deltas)


def _spectrum_kernel(cs_ref, ss_ref, cp_ref, sp_ref, hs_ref, hd_ref, kre_ref, kim_ref):
    cs = cs_ref[...]
    ss = ss_ref[...]
    hs = hs_ref[...].astype(BF16)
    hd = hd_ref[...].astype(BF16)
    cp = cp_ref[...]
    sp = sp_ref[...]
    kre_ref[...] = cp * _dot(cs, hs) + sp * _dot(ss, hs)
    kim_ref[...] = cp * _dot(ss, hd) - sp * _dot(cs, hd)


def _spectrum(cs, ss, cos_phi, sin_phi, hs, hd, tk=512, tn=256):
    L, dh = hs.shape
    tk = min(tk, L)
    mat = pl.BlockSpec((tk, L), lambda j, i: (i, 0))
    vec = pl.BlockSpec((tk, 1), lambda j, i: (i, 0))
    col = pl.BlockSpec((L, tn), lambda j, i: (0, j))
    out = pl.BlockSpec((tk, tn), lambda j, i: (i, j))
    return pl.pallas_call(
        _spectrum_kernel,
        grid=(dh // tn, L // tk),
        in_specs=[mat, mat, vec, vec, col, col],
        out_specs=[out, out],
        out_shape=[jax.ShapeDtypeStruct((L, dh), F32)] * 2,
        compiler_params=_params(("parallel", "parallel"), 40 * 1024 * 1024),
        name="filter_spectrum",
    )(cs, ss, cos_phi, sin_phi, hs, hd)


def _longconv_kernel(inv_scale, cs_ref, ss_ref, v_ref, x0_ref, kre_ref, kim_ref, bias_ref, y_ref):
    cs = cs_ref[...]
    ss = ss_ref[...]
    v = v_ref[...]
    vb = v.astype(BF16)
    xc = _dot(cs, vb)
    xs = _dot(ss, vb)
    kre = kre_ref[...]
    kim = kim_ref[...]
    yre = (xc * kre + xs * kim).astype(BF16)
    yim = (xc * kim - xs * kre).astype(BF16)
    conv = (_dot(cs, yre) - _dot(ss, yim)) * inv_scale
    y_ref[...] = (x0_ref[...] * (conv + bias_ref[...] * v)).astype(y_ref.dtype)


def _longconv(cs, ss, vv, x0, kre, kim, bias, tc=256):
    B, L, dh = vv.shape
    mat = pl.BlockSpec((L, L), lambda j, b: (0, 0), pipeline_mode=pl.Buffered(1))
    act = pl.BlockSpec((None, L, tc), lambda j, b: (b, 0, j))
    spec = pl.BlockSpec((L, tc), lambda j, b: (0, j))
    return pl.pallas_call(
        functools.partial(_longconv_kernel, 1.0 / L),
        grid=(dh // tc, B),
        in_specs=[mat, mat, act, act, spec, spec, pl.BlockSpec((1, tc), lambda j, b: (0, j))],
        out_specs=act,
        out_shape=jax.ShapeDtypeStruct((B, L, dh), BF16),
        compiler_params=_params(("parallel", "parallel"),
                                4 * L * L + 16 * L * tc + 16 * L * tc + 4 * L * tc + 40 * L * tc),
        name="hyena_longconv",
    )(cs, ss, vv, x0, kre, kim, bias)


def _merge_kernel(h_ref, yh_ref, ya_ref, wgh_ref, wga_ref, wh_ref, wa_ref, o_ref):
    h = h_ref[...]
    gh = jax.nn.sigmoid(_dot(h, wgh_ref[...]))
    ga = jax.nn.sigmoid(_dot(h, wga_ref[...]))
    o_ref[...] = (gh * _dot(yh_ref[...], wh_ref[...])
                  + ga * _dot(ya_ref[...], wa_ref[...])).astype(o_ref.dtype)


def _merge(hb, y_hy, y_att, w_in_bf, w_hy, w_att, layer, gate_col0, tm=1024, tn=512):
    T, D = hb.shape
    dh, da = y_hy.shape[1], y_att.shape[1]
    tm = min(tm, T)
    g0 = gate_col0 // tn
    nj = D // tn
    return pl.pallas_call(
        _merge_kernel,
        grid=(T // tm, nj),
        in_specs=[pl.BlockSpec((tm, D), lambda i, j: (i, 0)),
                  pl.BlockSpec((tm, dh), lambda i, j: (i, 0)),
                  pl.BlockSpec((tm, da), lambda i, j: (i, 0)),
                  pl.BlockSpec((None, D, tn), lambda i, j: (layer, 0, g0 + j)),
                  pl.BlockSpec((None, D, tn), lambda i, j: (layer, 0, g0 + nj + j)),
                  pl.BlockSpec((None, dh, tn), lambda i, j: (layer, 0, j)),
                  pl.BlockSpec((None, da, tn), lambda i, j: (layer, 0, j))],
        out_specs=pl.BlockSpec((tm, tn), lambda i, j: (i, j)),
        out_shape=jax.ShapeDtypeStruct((T, D), BF16),
        compiler_params=_params(("parallel", "parallel"),
                                4 * tm * (D + dh + da) + 4 * tn * (2 * D + dh + da) + 4 * tm * tn
                                + 32 * tm * tn),
        name="gated_merge",
    )(hb, y_hy, y_att, w_in_bf, w_in_bf, w_hy, w_att)


def _oproj_ln_kernel(alpha, x_ref, w_ref, h_ref, g_ref, b_ref, of_ref, ob_ref):
    tm = x_ref.shape[0]
    ts = min(tm, EPILOGUE_SUBTILE)
    w = w_ref[...]
    for r0 in range(0, tm, ts):
        y = alpha * h_ref[r0:r0 + ts, :] + _dot(x_ref[r0:r0 + ts, :], w)
        y = _layer_norm_rows(y, g_ref[...], b_ref[...])
        of_ref[r0:r0 + ts, :] = y
        ob_ref[r0:r0 + ts, :] = y.astype(BF16)


def _oproj_ln(xb, w_o, h, g, b, layer, alpha, tm=512):
    T, D = h.shape
    row = lambda i: (i, 0)
    vec = pl.BlockSpec((None, 1, D), lambda i: (layer, 0, 0))
    return pl.pallas_call(
        functools.partial(_oproj_ln_kernel, alpha),
        grid=(T // tm,),
        in_specs=[pl.BlockSpec((tm, D), row),
                  pl.BlockSpec((None, D, D), lambda i: (layer, 0, 0)),
                  pl.BlockSpec((tm, D), row), vec, vec],
        out_specs=[pl.BlockSpec((tm, D), row), pl.BlockSpec((tm, D), row)],
        out_shape=[jax.ShapeDtypeStruct((T, D), F32), jax.ShapeDtypeStruct((T, D), BF16)],
        compiler_params=_params(("parallel",), 4 * D * D + 24 * tm * D + 24 * tm * D),
        name="oproj_resid_ln",
    )(xb, w_o, h, g, b)


def _swiglu_partial(x_ref, wg_ref, wu_ref, wd_ref):
    x = x_ref[...]
    g = _dot(x, wg_ref[...].astype(BF16))
    u = _dot(x, wu_ref[...].astype(BF16))
    a = (g * jax.nn.sigmoid(g) * u).astype(BF16)
    return _dot(a, wd_ref[...].astype(BF16))


def _ffn_dense_kernel(alpha, x_ref, wg_ref, wu_ref, wd_ref, h_ref, g_ref, b_ref,
                      of_ref, ob_ref, acc_ref):
    f = pl.program_id(1)
    last = pl.num_programs(1) - 1

    @pl.when(f == 0)
    def _():
        acc_ref[...] = jnp.zeros_like(acc_ref)

    @pl.when(f < last)
    def _():
        acc_ref[...] += _swiglu_partial(x_ref, wg_ref, wu_ref, wd_ref)

    @pl.when(f == last)
    def _():
        tm = x_ref.shape[0]
        ts = min(tm, EPILOGUE_SUBTILE)
        for r0 in range(0, tm, ts):
            total = acc_ref[r0:r0 + ts, :] + _swiglu_partial(x_ref.at[r0:r0 + ts], wg_ref, wu_ref, wd_ref)
            y = _layer_norm_rows(alpha * h_ref[r0:r0 + ts, :] + total, g_ref[...], b_ref[...])
            of_ref[r0:r0 + ts, :] = y
            ob_ref[r0:r0 + ts, :] = y.astype(BF16)


def _ffn_dense(xb, wg, wu, wd, h, g, b, j, layer, alpha, tm=512, tf=512):
    T, D = xb.shape
    F = wg.shape[-1]
    row = lambda i, f: (i, 0)
    vec = pl.BlockSpec((None, 1, D), lambda i, f: (layer, 0, 0))
    return pl.pallas_call(
        functools.partial(_ffn_dense_kernel, alpha),
        grid=(T // tm, F // tf),
        in_specs=[pl.BlockSpec((tm, D), row),
                  pl.BlockSpec((None, D, tf), lambda i, f: (j, 0, f)),
                  pl.BlockSpec((None, D, tf), lambda i, f: (j, 0, f)),
                  pl.BlockSpec((None, tf, D), lambda i, f: (j, f, 0)),
                  pl.BlockSpec((tm, D), row), vec, vec],
        out_specs=[pl.BlockSpec((tm, D), row), pl.BlockSpec((tm, D), row)],
        out_shape=[jax.ShapeDtypeStruct((T, D), F32), jax.ShapeDtypeStruct((T, D), BF16)],
        scratch_shapes=[pltpu.VMEM((tm, D), F32)],
        compiler_params=_params(("parallel", "arbitrary"),
                                4 * tm * D + 12 * D * tf + 8 * tm * D + 12 * tm * D + 4 * tm * D
                                + 16 * tm * tf + 8 * tm * D),
        name="ffn_dense_swiglu_ln",
    )(xb, wg, wu, wd, h, g, b)


def _ffn_group_kernel(blk0, be_ref, rows_ref, x_ref, wg_ref, wu_ref, wd_ref, *rest):
    o_ref = rest[-1]
    f = pl.program_id(1)
    n_rows = rows_ref[blk0 + pl.program_id(0)]
    tm = x_ref.shape[0]

    @pl.when(jnp.logical_and(n_rows > 0, f == 0))
    def _():
        o_ref[...] = jnp.zeros_like(o_ref)

    step = min(tm, MOE_ROW_STEP)
    for hi in range(step, tm + 1, step):
        @pl.when(jnp.logical_and(n_rows > hi - step, n_rows <= hi))
        def _(hi=hi):
            o_ref[0:hi, :] += _swiglu_partial(x_ref.at[0:hi], wg_ref, wu_ref, wd_ref)


def _ffn_grouped(blk_e, blk_rows, xg, wg, wu, wd, j, tm, blk0, nblk, prev=None, tf=512):
    D = xg.shape[1]
    F = wg.shape[-1]
    nf = F // tf
    nb = xg.shape[0] // tm

    def live(i, rows):
        return rows[blk0 + i] > 0

    def x_rows(i, f, be, nv):
        return (jnp.where(live(i, nv), i, 0), 0)

    def out_rows(i, f, be, nv):
        return (jnp.where(live(i, nv), blk0 + i, nblk), 0)

    def hidden(i, f, nv):
        return jnp.where(live(i, nv), f, nf - 1)

    in_specs = [pl.BlockSpec((tm, D), x_rows),
                pl.BlockSpec((None, None, D, tf),
                             lambda i, f, be, nv: (j, be[blk0 + i], 0, hidden(i, f, nv))),
                pl.BlockSpec((None, None, D, tf),
                             lambda i, f, be, nv: (j, be[blk0 + i], 0, hidden(i, f, nv))),
                pl.BlockSpec((None, None, tf, D),
                             lambda i, f, be, nv: (j, be[blk0 + i], hidden(i, f, nv), 0))]
    args = [blk_e, blk_rows, xg, wg, wu, wd]
    aliases = {}
    if prev is not None:
        in_specs.append(pl.BlockSpec(memory_space=pl.ANY))
        aliases = {len(args): 0}
        args.append(prev)
    return pl.pallas_call(
        functools.partial(_ffn_group_kernel, blk0),
        grid_spec=pltpu.PrefetchScalarGridSpec(
            num_scalar_prefetch=2,
            grid=(nb, nf),
            in_specs=in_specs,
            out_specs=pl.BlockSpec((tm, D), out_rows)),
        out_shape=jax.ShapeDtypeStruct(((nblk + 1) * tm, D), F32),
        input_output_aliases=aliases,
        compiler_params=_params(("arbitrary", "arbitrary"),
                                4 * tm * D + 24 * D * tf + 8 * tm * D
                                + 6 * D * tf + 16 * tm * tf + 8 * tm * D),
        name="moe_grouped_swiglu",
    )(*args)


def _router_kernel(x_ref, wh_ref, wl_ref, b_ref, tri_ref, e_ref, w_ref, r_ref, cnt_ref):
    @pl.when(pl.program_id(0) == 0)
    def _():
        cnt_ref[...] = jnp.zeros_like(cnt_ref)

    x_hi, x_lo = _split_bf16(x_ref[...])
    nt = (((1,), (1,)), ((), ()))

    def dg(a, b):
        return lax.dot_general(a, b, nt, preferred_element_type=F32)

    wh = wh_ref[...]
    wl = wl_ref[...]
    logits = dg(wh, x_hi) + (dg(wh, x_lo) + dg(wl, x_hi)) + b_ref[...]
    n_e = logits.shape[0]
    eid = lax.broadcasted_iota(jnp.int32, logits.shape, 0)
    m1 = jnp.max(logits, axis=0, keepdims=True)
    i1 = jnp.min(jnp.where(logits == m1, eid, n_e), axis=0, keepdims=True)
    rest = jnp.where(eid == i1, -jnp.inf, logits)
    m2 = jnp.max(rest, axis=0, keepdims=True)
    i2 = jnp.min(jnp.where(rest == m2, eid, n_e), axis=0, keepdims=True)
    t = jnp.exp(m2 - m1)
    w1 = 1.0 / (1.0 + t)
    e_ref[...] = jnp.concatenate([i1, i2], axis=0)
    w_ref[...] = jnp.concatenate([w1, t * w1], axis=0)

    tri = tri_ref[...]
    oh1 = (eid == i1).astype(F32)
    oh2 = (eid == i2).astype(F32)
    tot1 = jnp.sum(oh1, axis=1, keepdims=True)
    tot2 = jnp.sum(oh2, axis=1, keepdims=True)
    base = cnt_ref[...][:, 0:1]
    c1 = base + _dot(oh1.astype(BF16), tri)
    c2 = base + tot1 + _dot(oh2.astype(BF16), tri)
    r1 = jnp.sum(oh1 * c1, axis=0, keepdims=True)
    r2 = jnp.sum(oh2 * c2, axis=0, keepdims=True)
    r_ref[...] = jnp.concatenate([r1, r2], axis=0).astype(jnp.int32)
    cnt_ref[...] = cnt_ref[...] + (tot1 + tot2)


def _router(h, rw_hi, rw_lo, rb, tm=512):
    T, D = h.shape
    E = rw_hi.shape[0]
    const = lambda i: (0, 0)
    out = pl.BlockSpec((TOP_K, tm), lambda i: (0, i))
    idx = jnp.arange(tm)
    tri = (idx[:, None] < idx[None, :]).astype(BF16)
    return pl.pallas_call(
        _router_kernel,
        grid=(T // tm,),
        in_specs=[pl.BlockSpec((tm, D), lambda i: (i, 0)), pl.BlockSpec((E, D), const),
                  pl.BlockSpec((E, D), const), pl.BlockSpec((E, 1), const),
                  pl.BlockSpec((tm, tm), const)],
        out_specs=[out, out, out, pl.BlockSpec((E, V7X_LANES), const)],
        out_shape=[jax.ShapeDtypeStruct((TOP_K, T), jnp.int32),
                   jax.ShapeDtypeStruct((TOP_K, T), F32),
                   jax.ShapeDtypeStruct((TOP_K, T), jnp.int32),
                   jax.ShapeDtypeStruct((E, V7X_LANES), F32)],
        compiler_params=_params(("arbitrary",), 32 * 1024 * 1024),
        name="moe_router_top2",
    )(h, rw_hi, rw_lo, rb, tri)


def _moe_combine_ln_kernel(alpha, h_ref, ya_ref, yb_ref, w_ref, g_ref, b_ref, of_ref, ob_ref):
    w = w_ref[...]
    f = ya_ref[...] * w[:, 0:1] + yb_ref[...] * w[:, 1:2]
    y = _layer_norm_rows(alpha * h_ref[...] + f, g_ref[...], b_ref[...])
    of_ref[...] = y
    ob_ref[...] = y.astype(BF16)


def _moe_combine_ln(h, ya, yb, w, g, b, alpha, tm=256):
    T, D = h.shape
    row = pl.BlockSpec((tm, D), lambda i: (i, 0))
    vec = pl.BlockSpec((1, D), lambda i: (0, 0))
    return pl.pallas_call(
        functools.partial(_moe_combine_ln_kernel, alpha),
        grid=(T // tm,),
        in_specs=[row, row, row, pl.BlockSpec((tm, w.shape[1]), lambda i: (i, 0)), vec, vec],
        out_specs=[row, row],
        out_shape=[jax.ShapeDtypeStruct((T, D), F32), jax.ShapeDtypeStruct((T, D), BF16)],
        compiler_params=_params(("parallel",), 64 * tm * D),
        name="moe_combine_ln",
    )(h, ya, yb, w, g.reshape(1, D), b.reshape(1, D))


def _dft_tables(L):
    n2 = 2 * L
    k = jnp.arange(L, dtype=jnp.int32)[:, None]
    n = jnp.arange(L, dtype=jnp.int32)[None, :]
    ang = (((2 * k + 1) * (2 * n + 1)) % (4 * n2)).astype(F32) * (math.pi / (2 * n2))
    phi = (2 * k + 1).astype(F32) * (math.pi / (2 * n2))
    return jnp.cos(ang).astype(BF16), jnp.sin(ang).astype(BF16), jnp.cos(phi), jnp.sin(phi)


def _rope_tables(L, hd):
    rows = L // GRID_W
    row = jnp.repeat(jnp.arange(rows, dtype=F32), GRID_W)
    col = jnp.tile(jnp.arange(GRID_W, dtype=F32), rows)
    n_pairs = hd // 4
    inv = ROPE_THETA ** (-jnp.arange(n_pairs, dtype=F32) / n_pairs)
    ang = jnp.concatenate([row[:, None] * inv, col[:, None] * inv], axis=-1)
    cos, sin = jnp.cos(ang), jnp.sin(ang)
    reps = V7X_LANES // hd
    return (jnp.tile(jnp.concatenate([cos, cos], axis=-1), (1, reps)),
            jnp.tile(jnp.concatenate([-sin, sin], axis=-1), (1, reps)))


def _filter_features(L, n_emb):
    t = jnp.linspace(0.0, 1.0, L, dtype=F32)[:, None]
    bands = (n_emb - 1) // 2
    w = 2.0 * math.pi * jnp.arange(L, dtype=F32)[:, None] / L
    f = jnp.linspace(1e-4, bands - 1, bands, dtype=F32)[None, :]
    z = jnp.concatenate([t, jnp.cos(f * w), -jnp.sin(f * w)], axis=-1)
    return jnp.pad(z, ((0, 0), (0, V7X_LANES - n_emb)))


def _moe_block_rows(T):
    return min(1024, max(V7X_MXU_DIM, T // 4))


def _moe(hf, hb, router_w, router_b, wg, wu, wd, j):
    T, D = hf.shape
    E = router_w.shape[1]
    rw_hi, rw_lo = _split_bf16(router_w.T)
    top_e, top_w, rank, cnt = _router(hf, rw_hi, rw_lo, router_b.reshape(E, 1))

    tm = _moe_block_rows(T)
    counts = cnt[:, 0].astype(jnp.int32)
    padded = (counts + tm - 1) // tm * tm
    pad_end = jnp.cumsum(padded)
    pad_start = pad_end - padded
    nblk = -(-(T * TOP_K + E * (tm - 1)) // tm)
    P = nblk * tm
    n_valid = (pad_end[-1] // tm).astype(jnp.int32)
    blk_first = jnp.arange(nblk, dtype=jnp.int32) * tm
    blk_e = jnp.sum((blk_first[:, None] >= pad_end[None, :]).astype(jnp.int32), axis=1)
    blk_e = jnp.minimum(blk_e, E - 1)
    row_end = (pad_start + counts)[blk_e]
    blk_rows = jnp.where(jnp.arange(nblk) < n_valid, jnp.clip(row_end - blk_first, 0, tm), 0)
    blk_e = jnp.where(jnp.arange(nblk) < n_valid, blk_e, blk_e[jnp.maximum(n_valid - 1, 0)])
    start_of = jnp.sum(jnp.where(top_e[:, :, None] == jnp.arange(E, dtype=jnp.int32),
                                 pad_start[None, None, :], 0), axis=-1)
    dest = start_of + rank

    tok = jnp.broadcast_to(jnp.arange(T, dtype=jnp.int32), (TOP_K, T))
    src_tok = jnp.zeros((P,), jnp.int32).at[dest.reshape(-1)].set(tok.reshape(-1))
    yg = None
    blk_rows = blk_rows.astype(jnp.int32)
    for blk0 in range(0, nblk, -(-nblk // MOE_CALLS)):
        nb = min(-(-nblk // MOE_CALLS), nblk - blk0)
        xg = hb[lax.slice_in_dim(src_tok, blk0 * tm, (blk0 + nb) * tm)]
        yg = _ffn_grouped(blk_e, blk_rows, xg, wg, wu, wd, j, tm, blk0, nblk, prev=yg)
    return yg[dest[0]], yg[dest[1]], top_w.T


def kernel(x, ln_in_g, ln_in_b, w_in, hy_conv_w, hy_conv_b, hy_fw1, hy_fb1, hy_fw2, hy_fb2, hy_fw3, hy_freq, hy_bias, q_norm_g, k_norm_g, w_hy_br, w_att_br, w_o, ln_mix_g, ln_mix_b, ffn_w_gate, ffn_w_up, ffn_w_down, router_w, router_b, exp_w_gate, exp_w_up, exp_w_down, ln_ffn_g, ln_ffn_b):
    B, L, D = x.shape
    T = B * L
    depth = w_in.shape[0]
    dh = w_hy_br.shape[1]
    d_attn = w_att_br.shape[1]
    hd = q_norm_g.shape[-1]
    n_q = d_attn // hd
    d_kv = (w_in.shape[2] - 3 * dh - d_attn - 2 * D) // 2
    n_kv = d_kv // hd
    alpha = (2 * depth) ** 0.25
    qkv0 = 3 * dh
    gate0 = qkv0 + d_attn + 2 * d_kv
    n_rot = (n_q + n_kv) * hd
    assert L % GRID_W == 0 and V7X_LANES % hd == 0 and n_rot % V7X_MXU_DIM == 0

    cs_bf, ss_bf, cos_phi, sin_phi = _dft_tables(L)
    cos_t, sin_t = _rope_tables(L, hd)
    z = _filter_features(L, hy_fw1.shape[1])
    min_decay = math.log(DECAY_TARGET) / SLOW_DECAY_PCT
    max_decay = math.log(DECAY_TARGET) / FAST_DECAY_PCT
    deltas = jnp.abs(jnp.linspace(min_decay, max_decay, dh, dtype=F32)).reshape(1, dh)
    lane = jnp.arange(V7X_MXU_DIM)
    bd = jnp.where((lane[:, None] // hd) == (lane[None, :] // hd), 1.0 / hd, 0.0).astype(BF16)
    qk_scale = jnp.concatenate([jnp.full((n_q * hd,), hd ** -0.5, F32),
                                jnp.ones((n_kv * hd,), F32)]).reshape(1, n_rot)

    w_in_bf = w_in.astype(BF16)
    w_hy_bf = w_hy_br.astype(BF16)
    w_att_bf = w_att_br.astype(BF16)
    w_o_bf = w_o.astype(BF16)
    ffn_g_bf = ffn_w_gate.astype(BF16)
    ffn_u_bf = ffn_w_up.astype(BF16)
    ffn_d_bf = ffn_w_down.astype(BF16)
    conv_b3 = hy_conv_b.reshape(depth, 1, 3 * dh)
    ln_mix_g3, ln_mix_b3 = ln_mix_g.reshape(depth, 1, D), ln_mix_b.reshape(depth, 1, D)
    ln_ffn_g3, ln_ffn_b3 = ln_ffn_g.reshape(depth, 1, D), ln_ffn_b.reshape(depth, 1, D)

    hf, hb = _resid_ln(x.reshape(T, D), [], ln_in_g, ln_in_b, alpha)
    for i in range(depth):
        x0, vv = _hy_proj(hb.reshape(B, L, D), w_in_bf, hy_conv_w, conv_b3, i, dh)
        fw1p = jnp.pad(hy_fw1[i], ((0, V7X_LANES - hy_fw1.shape[1]), (0, 0)))
        hs, hdf = _filters(z, fw1p, hy_fb1[i].reshape(1, -1), hy_fw2[i], hy_fb2[i].reshape(1, -1),
                           hy_freq[i], hy_fw3[i], deltas, dh)
        kre, kim = _spectrum(cs_bf, ss_bf, cos_phi, sin_phi, hs, hdf)
        y_hy = _longconv(cs_bf, ss_bf, vv, x0, kre, kim, hy_bias[i].reshape(1, dh))
        gain = jnp.concatenate([jnp.tile(q_norm_g[i], n_q), jnp.tile(k_norm_g[i], n_kv)]).reshape(1, n_rot)
        w_qkv = lax.slice_in_dim(w_in_bf[i], qkv0, gate0, axis=1)
        q, k, v = _qkv_proj(hb, w_qkv, bd, gain, qk_scale, cos_t, sin_t, B, L, n_q, n_kv, hd)
        y_att = _attention(q, k, v)
        merged = _merge(hb, y_hy.reshape(T, dh), y_att.reshape(T, d_attn), w_in_bf, w_hy_bf, w_att_bf,
                        i, gate0)
        hf, hb = _oproj_ln(merged, w_o_bf, hf, ln_mix_g3, ln_mix_b3, i, alpha)
        j = i // 2
        if i % 2 == 0:
            hf, hb = _ffn_dense(hb, ffn_g_bf, ffn_u_bf, ffn_d_bf, hf, ln_ffn_g3, ln_ffn_b3, j, i, alpha)
        else:
            ya, yb, wts = _moe(hf, hb, router_w[j], router_b[j], exp_w_gate, exp_w_up, exp_w_down, j)
            hf, hb = _moe_combine_ln(hf, ya, yb, wts, ln_ffn_g[i], ln_ffn_b[i], alpha)
    return hf.reshape(B, L, D)
```

```python
import functools
import math

import jax
import jax.numpy as jnp
from jax import lax
from jax.experimental import pallas as pl
from jax.experimental.pallas import tpu as pltpu

F32 = jnp.float32
BF16 = jnp.bfloat16

GRID_W = 64
ROPE_THETA = 10000.0
TOP_K = 2
DECAY_TARGET = 1e-2
FAST_DECAY_PCT = 0.3
SLOW_DECAY_PCT = 1.5
LN_EPS = 1e-5
RMS_EPS = 1e-6

V7X_VMEM_BYTES = 64 * 1024 * 1024
V7X_LANES = 128
V7X_MXU_DIM = 256
VMEM_BUDGET = V7X_VMEM_BYTES - 8 * 1024 * 1024
ATTN_SUBTILE = 512
EPILOGUE_SUBTILE = V7X_MXU_DIM
MOE_CALLS = 4
MOE_ROW_STEP = V7X_MXU_DIM


def _params(semantics, vmem_bytes):
    return pltpu.CompilerParams(dimension_semantics=semantics,
                                vmem_limit_bytes=min(int(vmem_bytes), VMEM_BUDGET))


def _dot(a, b):
    return jnp.dot(a, b, preferred_element_type=F32)


def _split_bf16(a):
    hi = a.astype(BF16)
    lo = (a - hi.astype(F32)).astype(BF16)
    return hi, lo


def _dot3(a_hi, a_lo, b_hi, b_lo):
    return _dot(a_hi, b_hi) + (_dot(a_hi, b_lo) + _dot(a_lo, b_hi))


def _layer_norm_rows(x, g, b):
    mu = jnp.mean(x, axis=-1, keepdims=True)
    xc = x - mu
    var = jnp.mean(xc * xc, axis=-1, keepdims=True)
    return xc * lax.rsqrt(var + LN_EPS) * g + b


def _resid_ln_kernel(alpha, n_add, *refs):
    h_ref = refs[0]
    add_refs = refs[1:1 + n_add]
    g_ref, b_ref, of_ref, ob_ref = refs[1 + n_add:]
    x = h_ref[...]
    if n_add:
        acc = add_refs[0][...]
        for r in add_refs[1:]:
            acc = acc + r[...]
        x = alpha * x + acc
    y = _layer_norm_rows(x, g_ref[...], b_ref[...])
    of_ref[...] = y
    ob_ref[...] = y.astype(BF16)


def _resid_ln(h, addends, g, b, alpha, tm=256):
    T, D = h.shape
    n_add = len(addends)
    row = pl.BlockSpec((tm, D), lambda i: (i, 0))
    vec = pl.BlockSpec((1, D), lambda i: (0, 0))
    return pl.pallas_call(
        functools.partial(_resid_ln_kernel, alpha, n_add),
        grid=(T // tm,),
        in_specs=[row] * (1 + n_add) + [vec, vec],
        out_specs=[row, row],
        out_shape=[jax.ShapeDtypeStruct((T, D), F32), jax.ShapeDtypeStruct((T, D), BF16)],
        compiler_params=_params(("parallel",), (2 * (1 + n_add) * 4 + 12 + 16) * tm * D),
        name="resid_ln",
    )(h, *addends, g.reshape(1, D), b.reshape(1, D))


def _hy_proj_kernel(h_ref, w0_ref, w1_ref, w2_ref, cw0_ref, cw1_ref, cw2_ref,
                    cb0_ref, cb1_ref, cb2_ref, x0_ref, vv_ref):
    h = h_ref[...]
    L = h.shape[0]
    row = lax.broadcasted_iota(jnp.int32, (L, 1), 0)

    def stream(w_ref, cw_ref, cb_ref):
        p = _dot(h, w_ref[...])
        prev = jnp.where(row == 0, 0.0, pltpu.roll(p, 1, 0))
        nxt = jnp.where(row == L - 1, 0.0, pltpu.roll(p, L - 1, 0))
        cw = cw_ref[...]
        return prev * cw[0:1] + p * cw[1:2] + nxt * cw[2:3] + cb_ref[...]

    x0_ref[...] = stream(w0_ref, cw0_ref, cb0_ref)
    vv_ref[...] = stream(w1_ref, cw1_ref, cb1_ref) * stream(w2_ref, cw2_ref, cb2_ref)


def _hy_proj(hb3, w_in_bf, conv_w, conv_b3, layer, dh, tn=256):
    B, L, D = hb3.shape
    nj = dh // tn

    def wspec(s):
        return pl.BlockSpec((None, D, tn), lambda j, b: (layer, 0, s * nj + j))

    def cwspec(s):
        return pl.BlockSpec((None, conv_w.shape[1], tn), lambda j, b: (layer, 0, s * nj + j))

    def cbspec(s):
        return pl.BlockSpec((None, 1, tn), lambda j, b: (layer, 0, s * nj + j))

    out = pl.BlockSpec((None, L, tn), lambda j, b: (b, 0, j))
    return pl.pallas_call(
        _hy_proj_kernel,
        grid=(nj, B),
        in_specs=[pl.BlockSpec((None, L, D), lambda j, b: (b, 0, 0)),
                  wspec(0), wspec(1), wspec(2), cwspec(0), cwspec(1), cwspec(2),
                  cbspec(0), cbspec(1), cbspec(2)],
        out_specs=[out, out],
        out_shape=[jax.ShapeDtypeStruct((B, L, dh), F32)] * 2,
        compiler_params=_params(("parallel", "parallel"),
                                4 * L * D + 12 * D * tn + 16 * L * tn + 40 * L * tn),
        name="hyena_proj_conv",
    )(hb3, w_in_bf, w_in_bf, w_in_bf, conv_w, conv_w, conv_w, conv_b3, conv_b3, conv_b3)


def _qkv_kernel(n_q, n_kv, hd, x_ref, w_ref, bd_ref, gain_ref, scale_ref, cos_ref, sin_ref,
                q_ref, k_ref, v_ref):
    n_rot = (n_q + n_kv) * hd
    tm = x_ref.shape[0]
    ts = min(tm, EPILOGUE_SUBTILE)
    w = w_ref[...]
    bd = bd_ref[...]
    gw = bd.shape[0]
    reps = n_rot // V7X_LANES
    lane = lax.broadcasted_iota(jnp.int32, (1, n_rot), 1)
    first_half = (lane % hd) < (hd // 2)
    half = hd // 2
    pad_lane = lax.broadcasted_iota(jnp.int32, (ts, V7X_LANES - hd), 1)
    ones_col = jnp.where(pad_lane == 0, 1.0, 0.0).astype(BF16)
    for r0 in range(0, tm, ts):
        rows = slice(r0, r0 + ts)
        p = _dot(x_ref[rows, :], w)
        pr = p[:, :n_rot]
        sq = (pr * pr).astype(BF16)
        ms = jnp.concatenate([_dot(sq[:, c:c + gw], bd) for c in range(0, n_rot, gw)], axis=1)
        xn = pr * lax.rsqrt(ms + RMS_EPS) * gain_ref[...]
        cos = jnp.concatenate([cos_ref[rows, :]] * reps, axis=1)
        sin = jnp.concatenate([sin_ref[rows, :]] * reps, axis=1)
        swapped = jnp.where(first_half, pltpu.roll(xn, n_rot - half, 1), pltpu.roll(xn, half, 1))
        rot = ((xn * cos + swapped * sin) * scale_ref[...]).astype(BF16)
        for hh in range(n_q):
            q_ref[hh, rows, :] = rot[:, hh * hd:(hh + 1) * hd]
        for hh in range(n_kv):
            k_ref[hh, rows, :] = rot[:, (n_q + hh) * hd:(n_q + hh + 1) * hd]
            v_h = p[:, n_rot + hh * hd:n_rot + (hh + 1) * hd].astype(BF16)
            v_ref[hh, rows, :] = jnp.concatenate([v_h, ones_col], axis=1)


def _qkv_proj(hb, w_qkv, bd, gain, scale, cos_t, sin_t, B, L, n_q, n_kv, hd, tm=1024):
    T, D = hb.shape
    nw = w_qkv.shape[1]
    n_rot = (n_q + n_kv) * hd
    tm = min(tm, L)
    tpb = L // tm
    const = lambda i: (0, 0)
    return pl.pallas_call(
        functools.partial(_qkv_kernel, n_q, n_kv, hd),
        grid=(T // tm,),
        in_specs=[pl.BlockSpec((tm, D), lambda i: (i, 0)),
                  pl.BlockSpec((D, nw), const),
                  pl.BlockSpec(bd.shape, const),
                  pl.BlockSpec((1, n_rot), const),
                  pl.BlockSpec((1, n_rot), const),
                  pl.BlockSpec((tm, V7X_LANES), lambda i: (i % tpb, 0)),
                  pl.BlockSpec((tm, V7X_LANES), lambda i: (i % tpb, 0))],
        out_specs=[pl.BlockSpec((None, n_q, tm, hd), lambda i: (i // tpb, 0, i % tpb, 0)),
                   pl.BlockSpec((None, n_kv, tm, hd), lambda i: (i // tpb, 0, i % tpb, 0)),
                   pl.BlockSpec((None, n_kv, tm, V7X_LANES), lambda i: (i // tpb, 0, i % tpb, 0))],
        out_shape=[jax.ShapeDtypeStruct((B, n_q, L, hd), BF16),
                   jax.ShapeDtypeStruct((B, n_kv, L, hd), BF16),
                   jax.ShapeDtypeStruct((B, n_kv, L, V7X_LANES), BF16)],
        compiler_params=_params(("parallel",),
                                4 * tm * D + 4 * D * nw + 8 * tm * (n_q + 2 * n_kv) * V7X_LANES
                                + 48 * tm * nw),
        name="qkv_proj_norm_rope",
    )(hb, w_qkv, bd, gain, scale, cos_t, sin_t)


def _attn_kernel(group, hd, q_ref, k_ref, v_ref, o_ref):
    k = k_ref[...]
    v = v_ref[...]
    tq = q_ref.shape[1]
    ts = min(tq, ATTN_SUBTILE)
    for r0 in range(0, tq, ts):
        for g in range(group):
            s = lax.dot_general(q_ref[g, r0:r0 + ts, :], k, (((1,), (1,)), ((), ())),
                                preferred_element_type=F32)
            m = jnp.max(s, axis=-1, keepdims=True)
            e = jnp.exp(s - m).astype(BF16)
            o = _dot(e, v)
            o_ref[r0:r0 + ts, g * hd:(g + 1) * hd] = (o[:, :hd] / o[:, hd:hd + 1]).astype(o_ref.dtype)


def _attention(q, k, v, tq=2048):
    B, n_q, L, hd = q.shape
    n_kv = k.shape[1]
    group = n_q // n_kv
    tq = min(tq, L)
    return pl.pallas_call(
        functools.partial(_attn_kernel, group, hd),
        grid=(B, n_kv, L // tq),
        in_specs=[pl.BlockSpec((None, group, tq, hd), lambda b, j, t: (b, j, t, 0)),
                  pl.BlockSpec((None, None, L, hd), lambda b, j, t: (b, j, 0, 0)),
                  pl.BlockSpec((None, None, L, v.shape[-1]), lambda b, j, t: (b, j, 0, 0))],
        out_specs=pl.BlockSpec((None, tq, group * hd), lambda b, j, t: (b, t, j)),
        out_shape=jax.ShapeDtypeStruct((B, L, n_q * hd), BF16),
        compiler_params=_params(("parallel", "parallel", "parallel"),
                                8 * L * V7X_LANES + 8 * group * tq * V7X_LANES + 40 * tq * L),
        name="gqa_attention",
    )(q, k, v)


def _filter_kernel(z_ref, fw1_ref, fb1_ref, fw2_ref, fb2_ref, fr_ref, w3f_ref, w3b_ref, dl_ref,
                   hs_ref, hd_ref):
    z = z_ref[...]
    L = z.shape[0]
    fr = fr_ref[...]

    def hp_dot(a, b):
        a_hi, a_lo = _split_bf16(a)
        b_hi, b_lo = _split_bf16(b)
        return _dot3(a_hi, a_lo, b_hi, b_lo)

    a = jnp.sin(fr[0:1] * (hp_dot(z, fw1_ref[...]) + fb1_ref[...]))
    a = jnp.sin(fr[1:2] * (hp_dot(a, fw2_ref[...]) + fb2_ref[...]))
    window = jnp.exp(-z[:, 0:1] * dl_ref[...])
    h_fwd = hp_dot(a, w3f_ref[...]) * window
    h_bwd = hp_dot(a, w3b_ref[...]) * window
    row = lax.broadcasted_iota(jnp.int32, (L, 1), 0)
    h_bwd0 = jnp.where(row == 0, 0.0, h_bwd)
    hs_ref[...] = h_fwd + h_bwd0
    hd_ref[...] = h_bwd0 - h_fwd


def _filters(z, fw1p, fb1, fw2, fb2, fr, fw3, deltas, dh, tn=256):
    L, zw = z.shape
    nh = fw2.shape[0]
    nj = dh // tn
    const = lambda j: (0, 0)
    out = pl.BlockSpec((L, tn), lambda j: (0, j))
    return pl.pallas_call(
        _filter_kernel,
        grid=(nj,),
        in_specs=[pl.BlockSpec((L, zw), const), pl.BlockSpec((zw, nh), const),
                  pl.BlockSpec((1, nh), const), pl.BlockSpec((nh, nh), const),
                  pl.BlockSpec((1, nh), const), pl.BlockSpec((2, nh), const),
                  pl.BlockSpec((nh, tn), lambda j: (0, j)),
                  pl.BlockSpec((nh, tn), lambda j: (0, nj + j)),
                  pl.BlockSpec((1, tn), lambda j: (0, j))],
        out_specs=[out, out],
        out_shape=[jax.ShapeDtypeStruct((L, dh), F32)] * 2,
        compiler_params=_params(("parallel",), 32 * 1024 * 1024),
        name="hyena_filters",
    )(z, fw1p, fb1, fw2, fb2, fr, fw3, fw3, deltas)


def _spectrum_kernel(cs_ref, ss_ref, cp_ref, sp_ref, hs_ref, hd_ref, kre_ref, kim_ref):
    cs = cs_ref[...]
    ss = ss_ref[...]
    hs = hs_ref[...].astype(BF16)
    hd = hd_ref[...].astype(BF16)
    cp = cp_ref[...]
    sp = sp_ref[...]
    kre_ref[...] = cp * _dot(cs, hs) + sp * _dot(ss, hs)
    kim_ref[...] = cp * _dot(ss, hd) - sp * _dot(cs, hd)


def _spectrum(cs, ss, cos_phi, sin_phi, hs, hd, tk=512, tn=256):
    L, dh = hs.shape
    tk = min(tk, L)
    mat = pl.BlockSpec((tk, L), lambda j, i: (i, 0))
    vec = pl.BlockSpec((tk, 1), lambda j, i: (i, 0))
    col = pl.BlockSpec((L, tn), lambda j, i: (0, j))
    out = pl.BlockSpec((tk, tn), lambda j, i: (i, j))
    return pl.pallas_call(
        _spectrum_kernel,
        grid=(dh // tn, L // tk),
        in_specs=[mat, mat, vec, vec, col, col],
        out_specs=[out, out],
        out_shape=[jax.ShapeDtypeStruct((L, dh), F32)] * 2,
        compiler_params=_params(("parallel", "parallel"), 40 * 1024 * 1024),
        name="filter_spectrum",
    )(cs, ss, cos_phi, sin_phi, hs, hd)


def _longconv_kernel(inv_scale, cs_ref, ss_ref, v_ref, x0_ref, kre_ref, kim_ref, bias_ref, y_ref):
    cs = cs_ref[...]
    ss = ss_ref[...]
    v = v_ref[...]
    vb = v.astype(BF16)
    xc = _dot(cs, vb)
    xs = _dot(ss, vb)
    kre = kre_ref[...]
    kim = kim_ref[...]
    yre = (xc * kre + xs * kim).astype(BF16)
    yim = (xc * kim - xs * kre).astype(BF16)
    conv = (_dot(cs, yre) - _dot(ss, yim)) * inv_scale
    y_ref[...] = (x0_ref[...] * (conv + bias_ref[...] * v)).astype(y_ref.dtype)


def _longconv(cs, ss, vv, x0, kre, kim, bias, tc=256):
    B, L, dh = vv.shape
    mat = pl.BlockSpec((L, L), lambda j, b: (0, 0), pipeline_mode=pl.Buffered(1))
    act = pl.BlockSpec((None, L, tc), lambda j, b: (b, 0, j))
    spec = pl.BlockSpec((L, tc), lambda j, b: (0, j))
    return pl.pallas_call(
        functools.partial(_longconv_kernel, 1.0 / L),
        grid=(dh // tc, B),
        in_specs=[mat, mat, act, act, spec, spec, pl.BlockSpec((1, tc), lambda j, b: (0, j))],
        out_specs=act,
        out_shape=jax.ShapeDtypeStruct((B, L, dh), BF16),
        compiler_params=_params(("parallel", "parallel"),
                                4 * L * L + 16 * L * tc + 16 * L * tc + 4 * L * tc + 40 * L * tc),
        name="hyena_longconv",
    )(cs, ss, vv, x0, kre, kim, bias)


def _merge_kernel(h_ref, yh_ref, ya_ref, wgh_ref, wga_ref, wh_ref, wa_ref, o_ref):
    h = h_ref[...]
    gh = jax.nn.sigmoid(_dot(h, wgh_ref[...]))
    ga = jax.nn.sigmoid(_dot(h, wga_ref[...]))
    o_ref[...] = (gh * _dot(yh_ref[...], wh_ref[...])
                  + ga * _dot(ya_ref[...], wa_ref[...])).astype(o_ref.dtype)


def _merge(hb, y_hy, y_att, w_in_bf, w_hy, w_att, layer, gate_col0, tm=1024, tn=512):
    T, D = hb.shape
    dh, da = y_hy.shape[1], y_att.shape[1]
    tm = min(tm, T)
    g0 = gate_col0 // tn
    nj = D // tn
    return pl.pallas_call(
        _merge_kernel,
        grid=(T // tm, nj),
        in_specs=[pl.BlockSpec((tm, D), lambda i, j: (i, 0)),
                  pl.BlockSpec((tm, dh), lambda i, j: (i, 0)),
                  pl.BlockSpec((tm, da), lambda i, j: (i, 0)),
                  pl.BlockSpec((None, D, tn), lambda i, j: (layer, 0, g0 + j)),
                  pl.BlockSpec((None, D, tn), lambda i, j: (layer, 0, g0 + nj + j)),
                  pl.BlockSpec((None, dh, tn), lambda i, j: (layer, 0, j)),
                  pl.BlockSpec((None, da, tn), lambda i, j: (layer, 0, j))],
        out_specs=pl.BlockSpec((tm, tn), lambda i, j: (i, j)),
        out_shape=jax.ShapeDtypeStruct((T, D), BF16),
        compiler_params=_params(("parallel", "parallel"),
                                4 * tm * (D + dh + da) + 4 * tn * (2 * D + dh + da) + 4 * tm * tn
                                + 32 * tm * tn),
        name="gated_merge",
    )(hb, y_hy, y_att, w_in_bf, w_in_bf, w_hy, w_att)


def _oproj_ln_kernel(alpha, x_ref, w_ref, h_ref, g_ref, b_ref, of_ref, ob_ref):
    tm = x_ref.shape[0]
    ts = min(tm, EPILOGUE_SUBTILE)
    w = w_ref[...]
    for r0 in range(0, tm, ts):
        y = alpha * h_ref[r0:r0 + ts, :] + _dot(x_ref[r0:r0 + ts, :], w)
        y = _layer_norm_rows(y, g_ref[...], b_ref[...])
        of_ref[r0:r0 + ts, :] = y
        ob_ref[r0:r0 + ts, :] = y.astype(BF16)


def _oproj_ln(xb, w_o, h, g, b, layer, alpha, tm=512):
    T, D = h.shape
    row = lambda i: (i, 0)
    vec = pl.BlockSpec((None, 1, D), lambda i: (layer, 0, 0))
    return pl.pallas_call(
        functools.partial(_oproj_ln_kernel, alpha),
        grid=(T // tm,),
        in_specs=[pl.BlockSpec((tm, D), row),
                  pl.BlockSpec((None, D, D), lambda i: (layer, 0, 0)),
                  pl.BlockSpec((tm, D), row), vec, vec],
        out_specs=[pl.BlockSpec((tm, D), row), pl.BlockSpec((tm, D), row)],
        out_shape=[jax.ShapeDtypeStruct((T, D), F32), jax.ShapeDtypeStruct((T, D), BF16)],
        compiler_params=_params(("parallel",), 4 * D * D + 24 * tm * D + 24 * tm * D),
        name="oproj_resid_ln",
    )(xb, w_o, h, g, b)


def _swiglu_partial(x_ref, wg_ref, wu_ref, wd_ref):
    x = x_ref[...]
    g = _dot(x, wg_ref[...].astype(BF16))
    u = _dot(x, wu_ref[...].astype(BF16))
    a = (g * jax.nn.sigmoid(g) * u).astype(BF16)
    return _dot(a, wd_ref[...].astype(BF16))


def _ffn_dense_kernel(alpha, x_ref, wg_ref, wu_ref, wd_ref, h_ref, g_ref, b_ref,
                      of_ref, ob_ref, acc_ref):
    f = pl.program_id(1)

    @pl.when(f == 0)
    def _():
        acc_ref[...] = jnp.zeros_like(acc_ref)

    acc_ref[...] += _swiglu_partial(x_ref, wg_ref, wu_ref, wd_ref)

    @pl.when(f == pl.num_programs(1) - 1)
    def _():
        y = _layer_norm_rows(alpha * h_ref[...] + acc_ref[...], g_ref[...], b_ref[...])
        of_ref[...] = y
        ob_ref[...] = y.astype(BF16)


def _ffn_dense(xb, wg, wu, wd, h, g, b, j, layer, alpha, tm=512, tf=512):
    T, D = xb.shape
    F = wg.shape[-1]
    row = lambda i, f: (i, 0)
    vec = pl.BlockSpec((None, 1, D), lambda i, f: (layer, 0, 0))
    return pl.pallas_call(
        functools.partial(_ffn_dense_kernel, alpha),
        grid=(T // tm, F // tf),
        in_specs=[pl.BlockSpec((tm, D), row),
                  pl.BlockSpec((None, D, tf), lambda i, f: (j, 0, f)),
                  pl.BlockSpec((None, D, tf), lambda i, f: (j, 0, f)),
                  pl.BlockSpec((None, tf, D), lambda i, f: (j, f, 0)),
                  pl.BlockSpec((tm, D), row), vec, vec],
        out_specs=[pl.BlockSpec((tm, D), row), pl.BlockSpec((tm, D), row)],
        out_shape=[jax.ShapeDtypeStruct((T, D), F32), jax.ShapeDtypeStruct((T, D), BF16)],
        scratch_shapes=[pltpu.VMEM((tm, D), F32)],
        compiler_params=_params(("parallel", "arbitrary"),
                                4 * tm * D + 12 * D * tf + 8 * tm * D + 12 * tm * D + 4 * tm * D
                                + 16 * tm * tf + 8 * tm * D),
        name="ffn_dense_swiglu_ln",
    )(xb, wg, wu, wd, h, g, b)


def _ffn_group_kernel(blk0, be_ref, rows_ref, x_ref, wg_ref, wu_ref, wd_ref, *rest):
    o_ref = rest[-1]
    f = pl.program_id(1)
    n_rows = rows_ref[blk0 + pl.program_id(0)]
    tm = x_ref.shape[0]

    @pl.when(jnp.logical_and(n_rows > 0, f == 0))
    def _():
        o_ref[...] = jnp.zeros_like(o_ref)

    step = min(tm, MOE_ROW_STEP)
    for hi in range(step, tm + 1, step):
        @pl.when(jnp.logical_and(n_rows > hi - step, n_rows <= hi))
        def _(hi=hi):
            o_ref[0:hi, :] += _swiglu_partial(x_ref.at[0:hi], wg_ref, wu_ref, wd_ref)


def _ffn_grouped(blk_e, blk_rows, xg, wg, wu, wd, j, tm, blk0, nblk, prev=None, tf=512):
    D = xg.shape[1]
    F = wg.shape[-1]
    nf = F // tf
    nb = xg.shape[0] // tm

    def live(i, rows):
        return rows[blk0 + i] > 0

    def x_rows(i, f, be, nv):
        return (jnp.where(live(i, nv), i, 0), 0)

    def out_rows(i, f, be, nv):
        return (jnp.where(live(i, nv), blk0 + i, nblk), 0)

    def hidden(i, f, nv):
        return jnp.where(live(i, nv), f, nf - 1)

    in_specs = [pl.BlockSpec((tm, D), x_rows),
                pl.BlockSpec((None, None, D, tf),
                             lambda i, f, be, nv: (j, be[blk0 + i], 0, hidden(i, f, nv))),
                pl.BlockSpec((None, None, D, tf),
                             lambda i, f, be, nv: (j, be[blk0 + i], 0, hidden(i, f, nv))),
                pl.BlockSpec((None, None, tf, D),
                             lambda i, f, be, nv: (j, be[blk0 + i], hidden(i, f, nv), 0))]
    args = [blk_e, blk_rows, xg, wg, wu, wd]
    aliases = {}
    if prev is not None:
        in_specs.append(pl.BlockSpec(memory_space=pl.ANY))
        aliases = {len(args): 0}
        args.append(prev)
    return pl.pallas_call(
        functools.partial(_ffn_group_kernel, blk0),
        grid_spec=pltpu.PrefetchScalarGridSpec(
            num_scalar_prefetch=2,
            grid=(nb, nf),
            in_specs=in_specs,
            out_specs=pl.BlockSpec((tm, D), out_rows)),
        out_shape=jax.ShapeDtypeStruct(((nblk + 1) * tm, D), F32),
        input_output_aliases=aliases,
        compiler_params=_params(("arbitrary", "arbitrary"),
                                4 * tm * D + 24 * D * tf + 8 * tm * D
                                + 6 * D * tf + 16 * tm * tf + 8 * tm * D),
        name="moe_grouped_swiglu",
    )(*args)


def _router_kernel(x_ref, wh_ref, wl_ref, b_ref, tri_ref, e_ref, w_ref, r_ref, cnt_ref):
    @pl.when(pl.program_id(0) == 0)
    def _():
        cnt_ref[...] = jnp.zeros_like(cnt_ref)

    x_hi, x_lo = _split_bf16(x_ref[...])
    nt = (((1,), (1,)), ((), ()))

    def dg(a, b):
        return lax.dot_general(a, b, nt, preferred_element_type=F32)

    wh = wh_ref[...]
    wl = wl_ref[...]
    logits = dg(wh, x_hi) + (dg(wh, x_lo) + dg(wl, x_hi)) + b_ref[...]
    n_e = logits.shape[0]
    eid = lax.broadcasted_iota(jnp.int32, logits.shape, 0)
    m1 = jnp.max(logits, axis=0, keepdims=True)
    i1 = jnp.min(jnp.where(logits == m1, eid, n_e), axis=0, keepdims=True)
    rest = jnp.where(eid == i1, -jnp.inf, logits)
    m2 = jnp.max(rest, axis=0, keepdims=True)
    i2 = jnp.min(jnp.where(rest == m2, eid, n_e), axis=0, keepdims=True)
    t = jnp.exp(m2 - m1)
    w1 = 1.0 / (1.0 + t)
    e_ref[...] = jnp.concatenate([i1, i2], axis=0)
    w_ref[...] = jnp.concatenate([w1, t * w1], axis=0)

    tri = tri_ref[...]
    oh1 = (eid == i1).astype(F32)
    oh2 = (eid == i2).astype(F32)
    tot1 = jnp.sum(oh1, axis=1, keepdims=True)
    tot2 = jnp.sum(oh2, axis=1, keepdims=True)
    base = cnt_ref[...][:, 0:1]
    c1 = base + _dot(oh1.astype(BF16), tri)
    c2 = base + tot1 + _dot(oh2.astype(BF16), tri)
    r1 = jnp.sum(oh1 * c1, axis=0, keepdims=True)
    r2 = jnp.sum(oh2 * c2, axis=0, keepdims=True)
    r_ref[...] = jnp.concatenate([r1, r2], axis=0).astype(jnp.int32)
    cnt_ref[...] = cnt_ref[...] + (tot1 + tot2)


def _router(h, rw_hi, rw_lo, rb, tm=512):
    T, D = h.shape
    E = rw_hi.shape[0]
    const = lambda i: (0, 0)
    out = pl.BlockSpec((TOP_K, tm), lambda i: (0, i))
    idx = jnp.arange(tm)
    tri = (idx[:, None] < idx[None, :]).astype(BF16)
    return pl.pallas_call(
        _router_kernel,
        grid=(T // tm,),
        in_specs=[pl.BlockSpec((tm, D), lambda i: (i, 0)), pl.BlockSpec((E, D), const),
                  pl.BlockSpec((E, D), const), pl.BlockSpec((E, 1), const),
                  pl.BlockSpec((tm, tm), const)],
        out_specs=[out, out, out, pl.BlockSpec((E, V7X_LANES), const)],
        out_shape=[jax.ShapeDtypeStruct((TOP_K, T), jnp.int32),
                   jax.ShapeDtypeStruct((TOP_K, T), F32),
                   jax.ShapeDtypeStruct((TOP_K, T), jnp.int32),
                   jax.ShapeDtypeStruct((E, V7X_LANES), F32)],
        compiler_params=_params(("arbitrary",), 32 * 1024 * 1024),
        name="moe_router_top2",
    )(h, rw_hi, rw_lo, rb, tri)


def _moe_combine_ln_kernel(alpha, h_ref, ya_ref, yb_ref, w_ref, g_ref, b_ref, of_ref, ob_ref):
    w = w_ref[...]
    f = ya_ref[...] * w[:, 0:1] + yb_ref[...] * w[:, 1:2]
    y = _layer_norm_rows(alpha * h_ref[...] + f, g_ref[...], b_ref[...])
    of_ref[...] = y
    ob_ref[...] = y.astype(BF16)


def _moe_combine_ln(h, ya, yb, w, g, b, alpha, tm=256):
    T, D = h.shape
    row = pl.BlockSpec((tm, D), lambda i: (i, 0))
    vec = pl.BlockSpec((1, D), lambda i: (0, 0))
    return pl.pallas_call(
        functools.partial(_moe_combine_ln_kernel, alpha),
        grid=(T // tm,),
        in_specs=[row, row, row, pl.BlockSpec((tm, w.shape[1]), lambda i: (i, 0)), vec, vec],
        out_specs=[row, row],
        out_shape=[jax.ShapeDtypeStruct((T, D), F32), jax.ShapeDtypeStruct((T, D), BF16)],
        compiler_params=_params(("parallel",), 64 * tm * D),
        name="moe_combine_ln",
    )(h, ya, yb, w, g.reshape(1, D), b.reshape(1, D))


def _dft_tables(L):
    n2 = 2 * L
    k = jnp.arange(L, dtype=jnp.int32)[:, None]
    n = jnp.arange(L, dtype=jnp.int32)[None, :]
    ang = (((2 * k + 1) * (2 * n + 1)) % (4 * n2)).astype(F32) * (math.pi / (2 * n2))
    phi = (2 * k + 1).astype(F32) * (math.pi / (2 * n2))
    return jnp.cos(ang).astype(BF16), jnp.sin(ang).astype(BF16), jnp.cos(phi), jnp.sin(phi)


def _rope_tables(L, hd):
    rows = L // GRID_W
    row = jnp.repeat(jnp.arange(rows, dtype=F32), GRID_W)
    col = jnp.tile(jnp.arange(GRID_W, dtype=F32), rows)
    n_pairs = hd // 4
    inv = ROPE_THETA ** (-jnp.arange(n_pairs, dtype=F32) / n_pairs)
    ang = jnp.concatenate([row[:, None] * inv, col[:, None] * inv], axis=-1)
    cos, sin = jnp.cos(ang), jnp.sin(ang)
    reps = V7X_LANES // hd
    return (jnp.tile(jnp.concatenate([cos, cos], axis=-1), (1, reps)),
            jnp.tile(jnp.concatenate([-sin, sin], axis=-1), (1, reps)))


def _filter_features(L, n_emb):
    t = jnp.linspace(0.0, 1.0, L, dtype=F32)[:, None]
    bands = (n_emb - 1) // 2
    w = 2.0 * math.pi * jnp.arange(L, dtype=F32)[:, None] / L
    f = jnp.linspace(1e-4, bands - 1, bands, dtype=F32)[None, :]
    z = jnp.concatenate([t, jnp.cos(f * w), -jnp.sin(f * w)], axis=-1)
    return jnp.pad(z, ((0, 0), (0, V7X_LANES - n_emb)))


def _moe_block_rows(T):
    return min(1024, max(V7X_MXU_DIM, T // 4))


def _moe(hf, hb, router_w, router_b, wg, wu, wd, j):
    T, D = hf.shape
    E = router_w.shape[1]
    rw_hi, rw_lo = _split_bf16(router_w.T)
    top_e, top_w, rank, cnt = _router(hf, rw_hi, rw_lo, router_b.reshape(E, 1))

    tm = _moe_block_rows(T)
    counts = cnt[:, 0].astype(jnp.int32)
    padded = (counts + tm - 1) // tm * tm
    pad_end = jnp.cumsum(padded)
    pad_start = pad_end - padded
    nblk = -(-(T * TOP_K + E * (tm - 1)) // tm)
    P = nblk * tm
    n_valid = (pad_end[-1] // tm).astype(jnp.int32)
    blk_first = jnp.arange(nblk, dtype=jnp.int32) * tm
    blk_e = jnp.sum((blk_first[:, None] >= pad_end[None, :]).astype(jnp.int32), axis=1)
    blk_e = jnp.minimum(blk_e, E - 1)
    row_end = (pad_start + counts)[blk_e]
    blk_rows = jnp.where(jnp.arange(nblk) < n_valid, jnp.clip(row_end - blk_first, 0, tm), 0)
    blk_e = jnp.where(jnp.arange(nblk) < n_valid, blk_e, blk_e[jnp.maximum(n_valid - 1, 0)])
    start_of = jnp.sum(jnp.where(top_e[:, :, None] == jnp.arange(E, dtype=jnp.int32),
                                 pad_start[None, None, :], 0), axis=-1)
    dest = start_of + rank

    tok = jnp.broadcast_to(jnp.arange(T, dtype=jnp.int32), (TOP_K, T))
    src_tok = jnp.zeros((P,), jnp.int32).at[dest.reshape(-1)].set(tok.reshape(-1))
    yg = None
    blk_rows = blk_rows.astype(jnp.int32)
    for blk0 in range(0, nblk, -(-nblk // MOE_CALLS)):
        nb = min(-(-nblk // MOE_CALLS), nblk - blk0)
        xg = hb[lax.slice_in_dim(src_tok, blk0 * tm, (blk0 + nb) * tm)]
        yg = _ffn_grouped(blk_e, blk_rows, xg, wg, wu, wd, j, tm, blk0, nblk, prev=yg)
    return yg[dest[0]], yg[dest[1]], top_w.T


def kernel(x, ln_in_g, ln_in_b, w_in, hy_conv_w, hy_conv_b, hy_fw1, hy_fb1, hy_fw2, hy_fb2, hy_fw3, hy_freq, hy_bias, q_norm_g, k_norm_g, w_hy_br, w_att_br, w_o, ln_mix_g, ln_mix_b, ffn_w_gate, ffn_w_up, ffn_w_down, router_w, router_b, exp_w_gate, exp_w_up, exp_w_down, ln_ffn_g, ln_ffn_b):
    B, L, D = x.shape
    T = B * L
    depth = w_in.shape[0]
    dh = w_hy_br.shape[1]
    d_attn = w_att_br.shape[1]
    hd = q_norm_g.shape[-1]
    n_q = d_attn // hd
    d_kv = (w_in.shape[2] - 3 * dh - d_attn - 2 * D) // 2
    n_kv = d_kv // hd
    alpha = (2 * depth) ** 0.25
    qkv0 = 3 * dh
    gate0 = qkv0 + d_attn + 2 * d_kv
    n_rot = (n_q + n_kv) * hd
    assert L % GRID_W == 0 and V7X_LANES % hd == 0 and n_rot % V7X_MXU_DIM == 0

    cs_bf, ss_bf, cos_phi, sin_phi = _dft_tables(L)
    cos_t, sin_t = _rope_tables(L, hd)
    z = _filter_features(L, hy_fw1.shape[1])
    min_decay = math.log(DECAY_TARGET) / SLOW_DECAY_PCT
    max_decay = math.log(DECAY_TARGET) / FAST_DECAY_PCT
    deltas = jnp.abs(jnp.linspace(min_decay, max_decay, dh, dtype=F32)).reshape(1, dh)
    lane = jnp.arange(V7X_MXU_DIM)
    bd = jnp.where((lane[:, None] // hd) == (lane[None, :] // hd), 1.0 / hd, 0.0).astype(BF16)
    qk_scale = jnp.concatenate([jnp.full((n_q * hd,), hd ** -0.5, F32),
                                jnp.ones((n_kv * hd,), F32)]).reshape(1, n_rot)

    w_in_bf = w_in.astype(BF16)
    w_hy_bf = w_hy_br.astype(BF16)
    w_att_bf = w_att_br.astype(BF16)
    w_o_bf = w_o.astype(BF16)
    ffn_g_bf = ffn_w_gate.astype(BF16)
    ffn_u_bf = ffn_w_up.astype(BF16)
    ffn_d_bf = ffn_w_down.astype(BF16)
    conv_b3 = hy_conv_b.reshape(depth, 1, 3 * dh)
    ln_mix_g3, ln_mix_b3 = ln_mix_g.reshape(depth, 1, D), ln_mix_b.reshape(depth, 1, D)
    ln_ffn_g3, ln_ffn_b3 = ln_ffn_g.reshape(depth, 1, D), ln_ffn_b.reshape(depth, 1, D)

    hf, hb = _resid_ln(x.reshape(T, D), [], ln_in_g, ln_in_b, alpha)
    for i in range(depth):
        x0, vv = _hy_proj(hb.reshape(B, L, D), w_in_bf, hy_conv_w, conv_b3, i, dh)
        fw1p = jnp.pad(hy_fw1[i], ((0, V7X_LANES - hy_fw1.shape[1]), (0, 0)))
        hs, hdf = _filters(z, fw1p, hy_fb1[i].reshape(1, -1), hy_fw2[i], hy_fb2[i].reshape(1, -1),
                           hy_freq[i], hy_fw3[i], deltas, dh)
        kre, kim = _spectrum(cs_bf, ss_bf, cos_phi, sin_phi, hs, hdf)
        y_hy = _longconv(cs_bf, ss_bf, vv, x0, kre, kim, hy_bias[i].reshape(1, dh))
        gain = jnp.concatenate([jnp.tile(q_norm_g[i], n_q), jnp.tile(k_norm_g[i], n_kv)]).reshape(1, n_rot)
        w_qkv = lax.slice_in_dim(w_in_bf[i], qkv0, gate0, axis=1)
        q, k, v = _qkv_proj(hb, w_qkv, bd, gain, qk_scale, cos_t, sin_t, B, L, n_q, n_kv, hd)
        y_att = _attention(q, k, v)
        merged = _merge(hb, y_hy.reshape(T, dh), y_att.reshape(T, d_attn), w_in_bf, w_hy_bf, w_att_bf,
                        i, gate0)
        hf, hb = _oproj_ln(merged, w_o_bf, hf, ln_mix_g3, ln_mix_b3, i, alpha)
        j = i // 2
        if i % 2 == 0:
            hf, hb = _ffn_dense(hb, ffn_g_bf, ffn_u_bf, ffn_d_bf, hf, ln_ffn_g3, ln_ffn_b3, j, i, alpha)
        else:
            ya, yb, wts = _moe(hf, hb, router_w[j], router_b[j], exp_w_gate, exp_w_up, exp_w_down, j)
            hf, hb = _moe_combine_ln(hf, ya, yb, wts, ln_ffn_g[i], ln_ffn_b[i], alpha)
    return hf.reshape(B, L, D)
```
